```python
import math
import jax, jax.numpy as jnp
from jax import lax
import numpy as np


D_MODEL = 1024
BATCH = 16
SEQ = 2048
DEPTH = 1

MEM_LEN = 256
EPS = 1e-6
SSD_INNER = 2 * D_MODEL
SSD_HEAD_DIM = 64
SSD_HEADS = SSD_INNER // SSD_HEAD_DIM
SSD_GROUPS = 4
SSD_HPG = SSD_HEADS // SSD_GROUPS
SSD_STATE = 128
SSD_CONV = 5
SSD_CHUNK = 128
SSD_XBC = SSD_INNER + 2 * SSD_GROUPS * SSD_STATE
ATTN_HEAD_DIM = 64
ATTN_Q_HEADS = D_MODEL // ATTN_HEAD_DIM
ATTN_KV_HEADS = 4
ATTN_GQA = ATTN_Q_HEADS // ATTN_KV_HEADS
ATTN_WIDTH = ATTN_Q_HEADS * ATTN_HEAD_DIM
ATTN_KV_WIDTH = ATTN_KV_HEADS * ATTN_HEAD_DIM
ATTN_WINDOW = 128
ATTN_BLOCK = 128
ROPE_THETA = 10000.0
XATTN_HEADS = 4
XATTN_HEAD_DIM = D_MODEL // XATTN_HEADS
FFN_HIDDEN = ((8 * D_MODEL + 3 * 256 - 1) // (3 * 256)) * 256
IN_SPLITS = (SSD_INNER, SSD_XBC, SSD_HEADS, SSD_HEADS, ATTN_WIDTH, ATTN_KV_WIDTH, ATTN_KV_WIDTH, 2 * D_MODEL)
IN_WIDTH = SSD_INNER + SSD_XBC + 2 * SSD_HEADS + ATTN_WIDTH + 2 * ATTN_KV_WIDTH + 2 * D_MODEL

kernel_name = 'bidir_hybrid_ssd_swa_gated_block'


def rms_norm(x, g):
    xf = x.astype(jnp.float32)
    y = xf * lax.rsqrt(jnp.mean(xf * xf, axis=-1, keepdims=True) + EPS)
    return (y * g.astype(jnp.float32)).astype(x.dtype)


def split_cols(a, sizes):
    return jnp.split(a, np.cumsum(sizes)[:-1].tolist(), axis=-1)


def depthwise_centred_conv(u, w, b):
    c = u.shape[-1]
    pad = (SSD_CONV - 1) // 2
    out = lax.conv_general_dilated(u, w[:, None, :].astype(u.dtype), (1,), [(pad, pad)],
                                   dimension_numbers=('NWC', 'WIO', 'NWC'), feature_group_count=c)
    return out + b.astype(u.dtype)


def ssd_chunked(xh, dt, a_head, b_in, c_in):
    bsz, s, g, r, p = xh.shape
    n = b_in.shape[-1]
    nc = s // SSD_CHUNK
    dtype = xh.dtype
    a_cum = jnp.cumsum((dt * a_head).reshape(bsz, nc, SSD_CHUNK, g, r), axis=2)
    x_c = (xh * dt[..., None].astype(dtype)).reshape(bsz, nc, SSD_CHUNK, g, r, p)
    b_c = b_in.reshape(bsz, nc, SSD_CHUNK, g, n)
    c_c = c_in.reshape(bsz, nc, SSD_CHUNK, g, n)
    lower = jnp.tril(jnp.ones((SSD_CHUNK, SSD_CHUNK), bool))
    seg = a_cum[:, :, :, None] - a_cum[:, :, None]
    decay = jnp.exp(jnp.where(lower[:, :, None, None], seg, -jnp.inf)).astype(dtype)
    cb = jnp.einsum('bclgn,bcsgn->bclsg', c_c, b_c)
    y_diag = jnp.einsum('bclsg,bclsgr,bcsgrp->bclgrp', cb, decay, x_c)
    to_end = jnp.exp(a_cum[:, :, -1:] - a_cum).astype(dtype)
    states = jnp.einsum('bclgn,bclgr,bclgrp->bcgrpn', b_c, to_end, x_c)
    chunk_decay = jnp.exp(a_cum[:, :, -1]).astype(dtype)

    def carry_state(state, inp):
        st, dec = inp
        return state * dec[..., None, None] + st, state

    h0 = jnp.zeros((bsz, g, r, p, n), dtype)
    _, h_in = lax.scan(carry_state, h0, (jnp.moveaxis(states, 1, 0), jnp.moveaxis(chunk_decay, 1, 0)))
    h_in = jnp.moveaxis(h_in, 0, 1)
    y_off = jnp.einsum('bclgn,bcgrpn,bclgr->bclgrp', c_c, h_in, jnp.exp(a_cum).astype(dtype))
    return (y_diag + y_off).reshape(bsz, s, g, r, p)


def gated_group_rmsnorm(y, z, g):
    bsz, s, d = y.shape
    v = (y * jax.nn.silu(z)).astype(jnp.float32).reshape(bsz, s, SSD_GROUPS, d // SSD_GROUPS)
    v = v * lax.rsqrt(jnp.mean(v * v, axis=-1, keepdims=True) + EPS)
    return (v.reshape(bsz, s, d) * g.astype(jnp.float32)).astype(y.dtype)


def rotary(u, pos):
    half = u.shape[-1] // 2
    inv_freq = ROPE_THETA ** (-jnp.arange(half, dtype=jnp.float32) / half)
    ang = pos.astype(jnp.float32)[:, None] * inv_freq[None]
    cos = jnp.cos(ang)[None, :, None, :]
    sin = jnp.sin(ang)[None, :, None, :]
    uf = u.astype(jnp.float32)
    u1, u2 = uf[..., :half], uf[..., half:]
    return jnp.concatenate([u1 * cos - u2 * sin, u2 * cos + u1 * sin], axis=-1).astype(u.dtype)


def band_blocks(u):
    b, s, h, d = u.shape
    nb = s // ATTN_BLOCK
    up = jnp.pad(u, ((0, 0), (ATTN_BLOCK, ATTN_BLOCK), (0, 0), (0, 0))).reshape(b, nb + 2, ATTN_BLOCK, h, d)
    return jnp.concatenate([up[:, :-2], up[:, 1:-1], up[:, 2:]], axis=2)


def windowed_gqa_with_sink(q, k, v, sink):
    b, s = q.shape[:2]
    nb = s // ATTN_BLOCK
    qb = q.reshape(b, nb, ATTN_BLOCK, ATTN_KV_HEADS, ATTN_GQA, ATTN_HEAD_DIM)
    kb = band_blocks(k)
    vb = band_blocks(v)
    scores = jnp.einsum('bnqhrd,bnkhd->bnhrqk', qb, kb).astype(jnp.float32) * (ATTN_HEAD_DIM ** -0.5)
    qpos = jnp.arange(s).reshape(nb, ATTN_BLOCK)
    kpos = (jnp.arange(nb)[:, None] - 1) * ATTN_BLOCK + jnp.arange(3 * ATTN_BLOCK)[None]
    valid = ((kpos[:, None, :] >= 0) & (kpos[:, None, :] < s)
             & (jnp.abs(qpos[:, :, None] - kpos[:, None, :]) <= ATTN_WINDOW))
    scores = jnp.where(valid[None, :, None, None], scores, -jnp.inf)
    sink_col = jnp.broadcast_to(sink.astype(jnp.float32).reshape(1, 1, ATTN_KV_HEADS, ATTN_GQA, 1, 1),
                                scores.shape[:-1] + (1,))
    probs = jax.nn.softmax(jnp.concatenate([scores, sink_col], axis=-1), axis=-1)[..., :-1]
    out = jnp.einsum('bnhrqk,bnkhd->bnqhrd', probs.astype(v.dtype), vb)
    return out.reshape(b, s, ATTN_WIDTH)


def hybrid_mixer(h, pos, w_in, conv_w, conv_b, dt_bias_f, dt_bias_b, a_log_f, a_log_b, d_skip,
                 ssd_norm_g, sink, w_br_ssd, w_br_attn, w_out):
    b, s, _ = h.shape
    z, xbc, dt_f, dt_b, q, k, v, gates = split_cols(h @ w_in, IN_SPLITS)
    xbc = jax.nn.silu(depthwise_centred_conv(xbc, conv_w, conv_b))
    xs, b_in, c_in = split_cols(xbc, (SSD_INNER, SSD_GROUPS * SSD_STATE, SSD_GROUPS * SSD_STATE))
    xh = xs.reshape(b, s, SSD_GROUPS, SSD_HPG, SSD_HEAD_DIM)
    b_in = b_in.reshape(b, s, SSD_GROUPS, SSD_STATE)
    c_in = c_in.reshape(b, s, SSD_GROUPS, SSD_STATE)
    dtf = jax.nn.softplus((dt_f + dt_bias_f).astype(jnp.float32)).reshape(b, s, SSD_GROUPS, SSD_HPG)
    dtb = jax.nn.softplus((dt_b + dt_bias_b).astype(jnp.float32)).reshape(b, s, SSD_GROUPS, SSD_HPG)
    af = -jnp.exp(a_log_f.astype(jnp.float32)).reshape(SSD_GROUPS, SSD_HPG)
    ab = -jnp.exp(a_log_b.astype(jnp.float32)).reshape(SSD_GROUPS, SSD_HPG)
    y_fwd = ssd_chunked(xh, dtf, af, b_in, c_in)
    y_bwd = jnp.flip(ssd_chunked(jnp.flip(xh, 1), jnp.flip(dtb, 1), ab,
                                 jnp.flip(b_in, 1), jnp.flip(c_in, 1)), 1)
    y = y_fwd + y_bwd + d_skip.reshape(SSD_GROUPS, SSD_HPG, 1).astype(xh.dtype) * xh
    y = gated_group_rmsnorm(y.reshape(b, s, SSD_INNER), z, ssd_norm_g)
    branch_ssd = y @ w_br_ssd
    q = rotary(q.reshape(b, s, ATTN_Q_HEADS, ATTN_HEAD_DIM), pos)
    k = rotary(k.reshape(b, s, ATTN_KV_HEADS, ATTN_HEAD_DIM), pos)
    v = v.reshape(b, s, ATTN_KV_HEADS, ATTN_HEAD_DIM)
    branch_attn = windowed_gqa_with_sink(q, k, v, sink) @ w_br_attn
    g_ssd, g_attn = jnp.split(jax.nn.sigmoid(gates.astype(jnp.float32)).astype(h.dtype), 2, axis=-1)
    return (g_ssd * branch_ssd + g_attn * branch_attn) @ w_out


def memory_cross_attention(h, mem_n, w_q, w_kv, w_o):
    b, s, _ = h.shape
    m = mem_n.shape[1]
    q = (h @ w_q).reshape(b, s, XATTN_HEADS, XATTN_HEAD_DIM)
    k, v = jnp.split(mem_n @ w_kv, 2, axis=-1)
    k = k.reshape(b, m, XATTN_HEADS, XATTN_HEAD_DIM)
    v = v.reshape(b, m, XATTN_HEADS, XATTN_HEAD_DIM)
    scores = jnp.einsum('bqhd,bkhd->bhqk', q, k).astype(jnp.float32) * (XATTN_HEAD_DIM ** -0.5)
    probs = jax.nn.softmax(scores, axis=-1).astype(v.dtype)
    out = jnp.einsum('bhqk,bkhd->bqhd', probs, v).reshape(b, s, D_MODEL)
    return out @ w_o


def swiglu_ffn(h, w_in, w_out):
    gate, up = jnp.split(h @ w_in, 2, axis=-1)
    return (jax.nn.silu(gate) * up) @ w_out


def setup_inputs(seed: int = 0) -> dict:
    key = jax.random.key(seed)
    ks = jax.random.split(key, 32)
    L = DEPTH

    def normal(k, shape, scale):
        return jax.random.normal(k, shape, jnp.float32) * scale

    def gain(k, n):
        return 1.0 + 0.02 * jax.random.normal(k, (L, n), jnp.float32)

    def dt_bias(k):
        dt0 = jnp.exp(jax.random.uniform(k, (L, SSD_HEADS), jnp.float32, math.log(1e-3), math.log(1e-1)))
        return dt0 + jnp.log(-jnp.expm1(-dt0))

    def a_log(k):
        return jnp.log(jax.random.uniform(k, (L, SSD_HEADS), jnp.float32, 1.0, 16.0))

    return {
        'x': normal(ks[0], (BATCH, SEQ, D_MODEL), 1.0),
        'mem': normal(ks[1], (BATCH, MEM_LEN, D_MODEL), 1.0),
        'norm_mix_g': gain(ks[2], D_MODEL),
        'w_in': normal(ks[3], (L, D_MODEL, IN_WIDTH), D_MODEL ** -0.5),
        'conv_w': normal(ks[4], (L, SSD_CONV, SSD_XBC), SSD_CONV ** -0.5),
        'conv_b': normal(ks[5], (L, SSD_XBC), 0.02),
        'dt_bias_fwd': dt_bias(ks[6]),
        'dt_bias_bwd': dt_bias(ks[7]),
        'a_log_fwd': a_log(ks[8]),
        'a_log_bwd': a_log(ks[9]),
        'd_skip': 1.0 + normal(ks[10], (L, SSD_HEADS), 0.1),
        'ssd_norm_g': gain(ks[11], SSD_INNER),
        'attn_sink': normal(ks[12], (L, ATTN_Q_HEADS), 0.5),
        'w_branch_ssd': normal(ks[13], (L, SSD_INNER, D_MODEL), SSD_INNER ** -0.5),
        'w_branch_attn': normal(ks[14], (L, ATTN_WIDTH, D_MODEL), ATTN_WIDTH ** -0.5),
        'w_mix_out': normal(ks[15], (L, D_MODEL, D_MODEL), D_MODEL ** -0.5),
        'norm_xattn_g': gain(ks[16], D_MODEL),
        'norm_mem_g': gain(ks[17], D_MODEL),
        'w_xattn_q': normal(ks[18], (L, D_MODEL, D_MODEL), D_MODEL ** -0.5),
        'w_xattn_kv': normal(ks[19], (L, D_MODEL, 2 * D_MODEL), D_MODEL ** -0.5),
        'w_xattn_out': normal(ks[20], (L, D_MODEL, D_MODEL), D_MODEL ** -0.5),
        'norm_ffn_g': gain(ks[21], D_MODEL),
        'w_ffn_in': normal(ks[22], (L, D_MODEL, 2 * FFN_HIDDEN), D_MODEL ** -0.5),
        'w_ffn_out': normal(ks[23], (L, FFN_HIDDEN, D_MODEL), FFN_HIDDEN ** -0.5),
        'norm_final_g': 1.0 + 0.02 * jax.random.normal(ks[24], (D_MODEL,), jnp.float32),
    }


def reference(x, mem, norm_mix_g, w_in, conv_w, conv_b, dt_bias_fwd, dt_bias_bwd, a_log_fwd, a_log_bwd,
              d_skip, ssd_norm_g, attn_sink, w_branch_ssd, w_branch_attn, w_mix_out, norm_xattn_g,
              norm_mem_g, w_xattn_q, w_xattn_kv, w_xattn_out, norm_ffn_g, w_ffn_in, w_ffn_out, norm_final_g):
    pos = jnp.arange(x.shape[1], dtype=jnp.int32)
    for layer in range(DEPTH):
        x = x + hybrid_mixer(rms_norm(x, norm_mix_g[layer]), pos, w_in[layer], conv_w[layer], conv_b[layer],
                             dt_bias_fwd[layer], dt_bias_bwd[layer], a_log_fwd[layer], a_log_bwd[layer],
                             d_skip[layer], ssd_norm_g[layer], attn_sink[layer], w_branch_ssd[layer],
                             w_branch_attn[layer], w_mix_out[layer])
        x = x + memory_cross_attention(rms_norm(x, norm_xattn_g[layer]), rms_norm(mem, norm_mem_g[layer]),
                                       w_xattn_q[layer], w_xattn_kv[layer], w_xattn_out[layer])
        x = x + swiglu_ffn(rms_norm(x, norm_ffn_g[layer]), w_ffn_in[layer], w_ffn_out[layer])
    return rms_norm(x, norm_final_g)
```

```python
import functools

import jax
import jax.numpy as jnp
from jax import lax
from jax.experimental import pallas as pl
from jax.experimental.pallas import tpu as pltpu

F32 = jnp.float32
BF16 = jnp.bfloat16

D_MODEL = 1024
EPS = 1e-6
SSD_INNER = 2048
SSD_HEAD_DIM = 64
SSD_HEADS = 32
SSD_GROUPS = 4
SSD_HPG = 8
SSD_STATE = 128
SSD_CONV = 5
CHUNK = 128
SSD_BC = SSD_GROUPS * SSD_STATE
SSD_XBC = SSD_INNER + 2 * SSD_BC
GROUP_W = SSD_HPG * SSD_HEAD_DIM
ATTN_HEAD_DIM = 64
ATTN_Q_HEADS = 16
ATTN_KV_HEADS = 4
ATTN_GQA = 4
ATTN_WIDTH = 1024
ATTN_KV_WIDTH = 256
ATTN_BLOCK = 128
ROPE_THETA = 10000.0
XATTN_HEADS = 4
XATTN_HEAD_DIM = 256
FFN_HIDDEN = 2816

LANES = 128
HALO_ROWS = 16
CONV_PAD = (SSD_CONV - 1) // 2
NEG_BIG = -1e30

OFF_Z = 0
OFF_XBC = OFF_Z + SSD_INNER
OFF_Q = OFF_XBC + SSD_XBC
OFF_K = OFF_Q + ATTN_WIDTH
OFF_V = OFF_K + ATTN_KV_WIDTH
OFF_G = OFF_V + ATTN_KV_WIDTH
OFF_DT = OFF_G + 2 * D_MODEL
IN_PACKED = OFF_DT + LANES

VMEM_LIMIT = 56 * 1024 * 1024


def _params(n_axes):
    return pltpu.CompilerParams(dimension_semantics=("arbitrary",) * n_axes,
                                vmem_limit_bytes=VMEM_LIMIT)


def _resident(shape):
    nd = len(shape)
    return pl.BlockSpec(shape, lambda *_: (0,) * nd, pipeline_mode=pl.Buffered(1))


def _rms(x, g):
    return x * lax.rsqrt(jnp.mean(x * x, axis=-1, keepdims=True) + EPS) * g


def _dot(a, b):
    return jnp.dot(a, b, preferred_element_type=F32)


def _dot_nt(a, b):
    return lax.dot_general(a, b, (((1,), (1,)), ((), ())), preferred_element_type=F32)


def _rotary(u, cos, sin_signed):
    n = u.shape[1]
    lane = lax.broadcasted_iota(jnp.int32, u.shape, 1)
    first_half = (lane % ATTN_HEAD_DIM) < (ATTN_HEAD_DIM // 2)
    partner = jnp.where(first_half, pltpu.roll(u, n - ATTN_HEAD_DIM // 2, axis=1),
                        pltpu.roll(u, ATTN_HEAD_DIM // 2, axis=1))
    reps = n // LANES
    cos_t = jnp.concatenate([cos] * reps, axis=1) if reps > 1 else cos
    sin_t = jnp.concatenate([sin_signed] * reps, axis=1) if reps > 1 else sin_signed
    return u * cos_t + partner * sin_t


def _in_proj_kernel(x_ref, g_ref, w_ref, cos_ref, sin_ref,
                    z_ref, xbc_ref, q_ref, k_ref, v_ref, gate_ref, dt_ref):
    h = _rms(x_ref[...], g_ref[...]).astype(BF16)

    def proj(off, width):
        return _dot(h, w_ref[:, off:off + width])

    step = 512
    for c in range(0, SSD_INNER, step):
        z_ref[:, c:c + step] = proj(OFF_Z + c, step).astype(BF16)
    for c in range(0, SSD_XBC, step):
        xbc_ref[:, c:c + step] = proj(OFF_XBC + c, step).astype(BF16)
    cos = cos_ref[...]
    sin = sin_ref[...]
    for c in range(0, ATTN_WIDTH, step):
        q = _rotary(proj(OFF_Q + c, step), cos, sin)
        q_ref[:, c:c + step] = (q * (ATTN_HEAD_DIM ** -0.5)).astype(BF16)
    k_ref[...] = _rotary(proj(OFF_K, ATTN_KV_WIDTH), cos, sin).astype(BF16)
    v_ref[...] = proj(OFF_V, ATTN_KV_WIDTH).astype(BF16)
    for c in range(0, 2 * D_MODEL, step):
        gate_ref[:, c:c + step] = jax.nn.sigmoid(proj(OFF_G + c, step)).astype(BF16)
    dt_ref[...] = proj(OFF_DT, LANES)


def _in_proj(x2, g, w_packed, cos, sin_signed, seq):
    t = x2.shape[0]
    tm = 256
    per_seq = seq // tm
    row = lambda i: (i, 0)
    pos = lambda i: (i % per_seq, 0)
    outs = [(SSD_INNER, BF16), (SSD_XBC, BF16), (ATTN_WIDTH, BF16), (ATTN_KV_WIDTH, BF16),
            (ATTN_KV_WIDTH, BF16), (2 * D_MODEL, BF16), (LANES, F32)]
    return pl.pallas_call(
        _in_proj_kernel,
        grid=(t // tm,),
        in_specs=[pl.BlockSpec((tm, D_MODEL), row),
                  _resident((1, D_MODEL)),
                  _resident((D_MODEL, IN_PACKED)),
                  pl.BlockSpec((tm, LANES), pos),
                  pl.BlockSpec((tm, LANES), pos)],
        out_specs=[pl.BlockSpec((tm, w), row) for w, _ in outs],
        out_shape=[jax.ShapeDtypeStruct((t, w), d) for w, d in outs],
        compiler_params=_params(1),
        name="in_proj",
    )(x2, g, w_packed, cos, sin_signed)


def _cumsum_rows(a):
    row = lax.broadcasted_iota(jnp.int32, a.shape, 0)
    v = a
    k = 1
    while k < a.shape[0]:
        v = v + jnp.where(row >= k, pltpu.roll(v, k, axis=0), 0.0)
        k *= 2
    return v


def _expand(v, e2_ref):
    hi = v.astype(BF16)
    lo = (v - hi.astype(F32)).astype(BF16)
    return _dot(jnp.concatenate([hi, lo], axis=1), e2_ref[...])


def _softplus(x):
    return jnp.maximum(x, 0.0) + jnp.log1p(jnp.exp(-jnp.abs(x)))


def _conv_silu(ext_ref, cur_ref, prev_ref, next_ref, w_ref, b_ref, has_prev, has_next):
    width = cur_ref.shape[1]
    prev = prev_ref[...].astype(F32)[HALO_ROWS - 8:, :]
    nxt = next_ref[...].astype(F32)[:8, :]
    ext_ref[0:8, :] = jnp.where(has_prev, prev, 0.0)
    ext_ref[8:8 + CHUNK, :] = cur_ref[...].astype(F32)
    ext_ref[8 + CHUNK:16 + CHUNK, :] = jnp.where(has_next, nxt, 0.0)
    acc = jnp.broadcast_to(b_ref[...], (CHUNK, width))
    for k in range(SSD_CONV):
        start = 8 - CONV_PAD + k
        acc = acc + w_ref[k:k + 1, :] * ext_ref[start:start + CHUNK, :]
    return acc * jax.nn.sigmoid(acc)


def _head_scalars(dt_ref, hp_ref):
    dtv = _softplus(dt_ref[...] + hp_ref[0:1, :])
    a = dtv * (-jnp.exp(hp_ref[1:2, :]))
    return dtv, a


def _ssd_bwd_kernel(cur_ref, prev_ref, next_ref, dt_ref, cw_ref, cb_ref, hp_ref, e2b_ref,
                    hb_ref, state_ref, ext_ref):
    c = pl.program_id(1)
    nc = pl.num_programs(1)

    @pl.when(c == 0)
    def _():
        state_ref[...] = jnp.zeros_like(state_ref)

    hb_ref[0] = state_ref[...].astype(BF16)

    act = _conv_silu(ext_ref, cur_ref, prev_ref, next_ref, cw_ref, cb_ref, c < nc - 1, c > 0)
    xs = act[:, :SSD_INNER]
    bm = act[:, SSD_INNER:]
    dtv, a = _head_scalars(dt_ref, hp_ref)
    incl = _cumsum_rows(a)
    excl = incl - a
    xw = (xs * _expand(jnp.exp(excl) * dtv, e2b_ref)).astype(BF16)
    decay = _expand(jnp.exp(incl[CHUNK - 8:, :]), e2b_ref)[7:8, :]
    for g in range(SSD_GROUPS):
        gs = slice(g * GROUP_W, (g + 1) * GROUP_W)
        bt = bm[:, g * SSD_STATE:(g + 1) * SSD_STATE].T.astype(BF16)
        state_ref[:, gs] = state_ref[:, gs] * decay[:, gs] + _dot(bt, xw[:, gs])


def _ssd_bwd(xbc, dt, conv_w, conv_b, hp, e2b, bsz, nc):
    width = SSD_INNER + SSD_BC
    hpb = CHUNK // HALO_ROWS
    last_halo = xbc.shape[0] // HALO_ROWS - 1
    chunk_of = lambda b, c: b * nc + (nc - 1 - c)
    return pl.pallas_call(
        _ssd_bwd_kernel,
        grid=(bsz, nc),
        in_specs=[pl.BlockSpec((CHUNK, width), lambda b, c: (chunk_of(b, c), 0)),
                  pl.BlockSpec((HALO_ROWS, width), lambda b, c: (jnp.maximum(chunk_of(b, c) * hpb - 1, 0), 0)),
                  pl.BlockSpec((HALO_ROWS, width),
                               lambda b, c: (jnp.minimum((chunk_of(b, c) + 1) * hpb, last_halo), 0)),
                  pl.BlockSpec((CHUNK, LANES), lambda b, c: (chunk_of(b, c), 0)),
                  pl.BlockSpec((8, width), lambda b, c: (0, 0)),
                  pl.BlockSpec((1, width), lambda b, c: (0, 0)),
                  _resident((8, LANES)),
                  _resident((2 * LANES, SSD_INNER))],
        out_specs=pl.BlockSpec((1, SSD_STATE, SSD_INNER), lambda b, c: (chunk_of(b, c), 0, 0)),
        out_shape=jax.ShapeDtypeStruct((bsz * nc, SSD_STATE, SSD_INNER), BF16),
        scratch_shapes=[pltpu.VMEM((SSD_STATE, SSD_INNER), F32),
                        pltpu.VMEM((CHUNK + 16, width), F32)],
        compiler_params=_params(2),
        name="ssd_bwd",
    )(xbc, xbc, xbc, dt, conv_w, conv_b, hp, e2b)


def _ssd_main_kernel(cur_ref, prev_ref, next_ref, dt_ref, z_ref, hb_ref, cw_ref, cb_ref, hp_ref,
                     dskip_ref, ng_ref, e2f_ref, e2b_ref, y_ref, state_ref, ext_ref, ybuf_ref):
    c = pl.program_id(1)
    nc = pl.num_programs(1)

    @pl.when(c == 0)
    def _():
        state_ref[...] = jnp.zeros_like(state_ref)

    act = _conv_silu(ext_ref, cur_ref, prev_ref, next_ref, cw_ref, cb_ref, c > 0, c < nc - 1)
    xs = act[:, :SSD_INNER]
    xs16 = xs.astype(BF16)
    bm = act[:, SSD_INNER:SSD_INNER + SSD_BC]
    cm16 = act[:, SSD_INNER + SSD_BC:].astype(BF16)

    dtv, a = _head_scalars(dt_ref, hp_ref)
    incl = _cumsum_rows(a)
    excl = incl - a
    tot = incl[CHUNK - 1:CHUNK, :]
    incl_t = incl.T
    excl_t = excl.T
    dtv_t = dtv.T

    row = lax.broadcasted_iota(jnp.int32, (CHUNK, CHUNK), 0)
    col = lax.broadcasted_iota(jnp.int32, (CHUNK, CHUNK), 1)
    lower = row >= col
    strict_lower = row > col
    strict_upper = row < col

    for g in range(SSD_GROUPS):
        bg = bm[:, g * SSD_STATE:(g + 1) * SSD_STATE].astype(BF16)
        cg = cm16[:, g * SSD_STATE:(g + 1) * SSD_STATE]
        cb = _dot_nt(cg, bg)
        for r in range(SSD_HPG):
            h = g * SSD_HPG + r
            hb_col = SSD_HEADS + h
            seg_f = incl[:, h:h + 1] - incl_t[h:h + 1, :]
            seg_b = excl_t[hb_col:hb_col + 1, :] - excl[:, hb_col:hb_col + 1]
            w = jnp.exp(jnp.where(lower, seg_f, seg_b))
            dt_f = dtv_t[h:h + 1, :]
            dt_b = dtv_t[hb_col:hb_col + 1, :]
            d = jnp.where(strict_lower, dt_f, jnp.where(strict_upper, dt_b, dt_f + dt_b))
            m = (cb * w * d).astype(BF16)
            hs = slice(h * SSD_HEAD_DIM, (h + 1) * SSD_HEAD_DIM)
            ybuf_ref[:, hs] = _dot(m, xs16[:, hs])

    scale_f = _expand(jnp.exp(incl), e2f_ref)
    scale_b = _expand(jnp.exp(tot - excl), e2b_ref)
    xw = (xs * _expand(jnp.exp(tot - incl) * dtv, e2f_ref)).astype(BF16)
    decay = _expand(jnp.exp(incl[CHUNK - 8:, :]), e2f_ref)[7:8, :]
    zf = z_ref[...].astype(F32)
    zgate = zf * jax.nn.sigmoid(zf)
    for g in range(SSD_GROUPS):
        gs = slice(g * GROUP_W, (g + 1) * GROUP_W)
        ns = slice(g * SSD_STATE, (g + 1) * SSD_STATE)
        cg = cm16[:, ns]
        st = state_ref[:, gs]
        y = (ybuf_ref[:, gs]
             + scale_f[:, gs] * _dot(cg, st.astype(BF16))
             + scale_b[:, gs] * _dot(cg, hb_ref[0, :, gs])
             + dskip_ref[:, gs] * xs[:, gs])
        v = y * zgate[:, gs]
        v = v * lax.rsqrt(jnp.mean(v * v, axis=-1, keepdims=True) + EPS)
        y_ref[:, gs] = (v * ng_ref[:, gs]).astype(BF16)
        bt = bm[:, ns].T.astype(BF16)
        state_ref[:, gs] = st * decay[:, gs] + _dot(bt, xw[:, gs])


def _ssd_main(xbc, dt, z, hb, conv_w, conv_b, hp, dskip, norm_g, e2f, e2b, bsz, nc):
    hpb = CHUNK // HALO_ROWS
    last_halo = xbc.shape[0] // HALO_ROWS - 1
    chunk_of = lambda b, c: b * nc + c
    return pl.pallas_call(
        _ssd_main_kernel,
        grid=(bsz, nc),
        in_specs=[pl.BlockSpec((CHUNK, SSD_XBC), lambda b, c: (chunk_of(b, c), 0)),
                  pl.BlockSpec((HALO_ROWS, SSD_XBC), lambda b, c: (jnp.maximum(chunk_of(b, c) * hpb - 1, 0), 0)),
                  pl.BlockSpec((HALO_ROWS, SSD_XBC),
                               lambda b, c: (jnp.minimum((chunk_of(b, c) + 1) * hpb, last_halo), 0)),
                  pl.BlockSpec((CHUNK, LANES), lambda b, c: (chunk_of(b, c), 0)),
                  pl.BlockSpec((CHUNK, SSD_INNER), lambda b, c: (chunk_of(b, c), 0)),
                  pl.BlockSpec((1, SSD_STATE, SSD_INNER), lambda b, c: (chunk_of(b, c), 0, 0)),
                  _resident((8, SSD_XBC)),
                  _resident((1, SSD_XBC)),
                  _resident((8, LANES)),
                  _resident((1, SSD_INNER)),
                  _resident((1, SSD_INNER)),
                  _resident((2 * LANES, SSD_INNER)),
                  _resident((2 * LANES, SSD_INNER))],
        out_specs=pl.BlockSpec((CHUNK, SSD_INNER), lambda b, c: (chunk_of(b, c), 0)),
        out_shape=jax.ShapeDtypeStruct((bsz * nc * CHUNK, SSD_INNER), BF16),
        scratch_shapes=[pltpu.VMEM((SSD_STATE, SSD_INNER), F32),
                        pltpu.VMEM((CHUNK + 16, SSD_XBC), F32),
                        pltpu.VMEM((CHUNK, SSD_INNER), F32)],
        compiler_params=_params(2),
        name="ssd_main",
    )(xbc, xbc, xbc, dt, z, hb, conv_w, conv_b, hp, dskip, norm_g, e2f, e2b)


def _swa_kernel(q_ref, kp_ref, kc_ref, kn_ref, vp_ref, vc_ref, vn_ref, sink_ref, o_ref):
    n = pl.program_id(1)
    nb = pl.num_programs(1)
    qi = lax.broadcasted_iota(jnp.int32, (ATTN_BLOCK, 3 * ATTN_BLOCK), 0)
    kj = lax.broadcasted_iota(jnp.int32, (ATTN_BLOCK, 3 * ATTN_BLOCK), 1)
    rel = kj - qi
    valid = (rel >= 0) & (rel <= 2 * ATTN_BLOCK)
    valid = valid & ((kj >= ATTN_BLOCK) | (n > 0)) & ((kj < 2 * ATTN_BLOCK) | (n < nb - 1))
    bias = jnp.where(valid, 0.0, NEG_BIG)
    for kv in range(ATTN_KV_HEADS):
        ks = slice(kv * ATTN_HEAD_DIM, (kv + 1) * ATTN_HEAD_DIM)
        kcat = jnp.concatenate([kp_ref[:, ks], kc_ref[:, ks], kn_ref[:, ks]], axis=0)
        vcat = jnp.concatenate([vp_ref[:, ks], vc_ref[:, ks], vn_ref[:, ks]], axis=0)
        for r in range(ATTN_GQA):
            h = kv * ATTN_GQA + r
            hs = slice(h * ATTN_HEAD_DIM, (h + 1) * ATTN_HEAD_DIM)
            s = _dot_nt(q_ref[:, hs], kcat) + bias
            sink = sink_ref[h:h + 1, 0:1]
            m = jnp.maximum(jnp.max(s, axis=-1, keepdims=True), sink)
            p = jnp.exp(s - m)
            denom = jnp.sum(p, axis=-1, keepdims=True) + jnp.exp(sink - m)
            o = _dot(p.astype(BF16), vcat) * (1.0 / denom)
            o_ref[:, hs] = o.astype(BF16)


def _swa(q, k, v, sink_rows, bsz, nb):
    blk = lambda b, n: b * nb + n
    cur = lambda b, n: (blk(b, n), 0)
    prev = lambda b, n: (blk(b, jnp.maximum(n - 1, 0)), 0)
    nxt = lambda b, n: (blk(b, jnp.minimum(n + 1, nb - 1)), 0)
    kvspec = lambda im: pl.BlockSpec((ATTN_BLOCK, ATTN_KV_WIDTH), im)
    return pl.pallas_call(
        _swa_kernel,
        grid=(bsz, nb),
        in_specs=[pl.BlockSpec((ATTN_BLOCK, ATTN_WIDTH), cur),
                  kvspec(prev), kvspec(cur), kvspec(nxt),
                  kvspec(prev), kvspec(cur), kvspec(nxt),
                  _resident((ATTN_Q_HEADS, LANES))],
        out_specs=pl.BlockSpec((ATTN_BLOCK, ATTN_WIDTH), cur),
        out_shape=jax.ShapeDtypeStruct(q.shape, BF16),
        compiler_params=_params(2),
        name="swa",
    )(q, k, k, k, v, v, v, sink_rows)


def _merge_kernel(x_ref, y_ref, a_ref, gate_ref, ws_ref, wa_ref, wo_ref, o_ref):
    bs = _dot(y_ref[...], ws_ref[...])
    ba = _dot(a_ref[...], wa_ref[...])
    g = gate_ref[...].astype(F32)
    mix = (g[:, :D_MODEL] * bs + g[:, D_MODEL:] * ba).astype(BF16)
    o_ref[...] = x_ref[...] + _dot(mix, wo_ref[...])


def _merge(x2, y, attn, gates, ws, wa, wo):
    t = x2.shape[0]
    tm = 512
    row = lambda i: (i, 0)
    return pl.pallas_call(
        _merge_kernel,
        grid=(t // tm,),
        in_specs=[pl.BlockSpec((tm, D_MODEL), row),
                  pl.BlockSpec((tm, SSD_INNER), row),
                  pl.BlockSpec((tm, ATTN_WIDTH), row),
                  pl.BlockSpec((tm, 2 * D_MODEL), row),
                  _resident((SSD_INNER, D_MODEL)),
                  _resident((ATTN_WIDTH, D_MODEL)),
                  _resident((D_MODEL, D_MODEL))],
        out_specs=pl.BlockSpec((tm, D_MODEL), row),
        out_shape=jax.ShapeDtypeStruct((t, D_MODEL), F32),
        compiler_params=_params(1),
        name="merge",
    )(x2, y, attn, gates, ws, wa, wo)


def _mem_kv_kernel(m_ref, g_ref, w_ref, o_ref):
    h = _rms(m_ref[...], g_ref[...]).astype(BF16)
    o_ref[...] = _dot(h, w_ref[...]).astype(BF16)


def _mem_kv(mem2, g, w_kv):
    t = mem2.shape[0]
    tm = 256
    row = lambda i: (i, 0)
    return pl.pallas_call(
        _mem_kv_kernel,
        grid=(t // tm,),
        in_specs=[pl.BlockSpec((tm, D_MODEL), row),
                  _resident((1, D_MODEL)),
                  _resident((D_MODEL, 2 * D_MODEL))],
        out_specs=pl.BlockSpec((tm, 2 * D_MODEL), row),
        out_shape=jax.ShapeDtypeStruct((t, 2 * D_MODEL), BF16),
        compiler_params=_params(1),
        name="mem_kv",
    )(mem2, g, w_kv)


def _xattn_kernel(x_ref, g_ref, kv_ref, wq_ref, wo_ref, o_ref, ctx_ref):
    x = x_ref[...]
    h = _rms(x, g_ref[...]).astype(BF16)
    q = (_dot(h, wq_ref[...]) * (XATTN_HEAD_DIM ** -0.5)).astype(BF16)
    for hd in range(XATTN_HEADS):
        ks = slice(hd * XATTN_HEAD_DIM, (hd + 1) * XATTN_HEAD_DIM)
        vs = slice(D_MODEL + hd * XATTN_HEAD_DIM, D_MODEL + (hd + 1) * XATTN_HEAD_DIM)
        s = _dot_nt(q[:, ks], kv_ref[:, ks])
        m = jnp.max(s, axis=-1, keepdims=True)
        p = jnp.exp(s - m)
        denom = jnp.sum(p, axis=-1, keepdims=True)
        ctx = _dot(p.astype(BF16), kv_ref[:, vs]) * (1.0 / denom)
        ctx_ref[:, ks] = ctx.astype(BF16)
    o_ref[...] = x + _dot(ctx_ref[...], wo_ref[...])


def _xattn(x2, g, kv, wq, wo, seq, mem_len):
    t = x2.shape[0]
    tm = 512
    per_seq = seq // tm
    row = lambda i: (i, 0)
    return pl.pallas_call(
        _xattn_kernel,
        grid=(t // tm,),
        in_specs=[pl.BlockSpec((tm, D_MODEL), row),
                  _resident((1, D_MODEL)),
                  pl.BlockSpec((mem_len, 2 * D_MODEL), lambda i: (i // per_seq, 0)),
                  _resident((D_MODEL, D_MODEL)),
                  _resident((D_MODEL, D_MODEL))],
        out_specs=pl.BlockSpec((tm, D_MODEL), row),
        out_shape=jax.ShapeDtypeStruct((t, D_MODEL), F32),
        scratch_shapes=[pltpu.VMEM((tm, D_MODEL), BF16)],
        compiler_params=_params(1),
        name="xattn",
    )(x2, g, kv, wq, wo)


FFN_STEP = 1408


def _ffn_kernel(x_ref, g_ref, wg_ref, wu_ref, wd_ref, gf_ref, o_ref):
    x = x_ref[...]
    h = _rms(x, g_ref[...]).astype(BF16)
    acc = x
    for c in range(0, FFN_HIDDEN, FFN_STEP):
        gate = _dot(h, wg_ref[:, c:c + FFN_STEP])
        up = _dot(h, wu_ref[:, c:c + FFN_STEP])
        act = (gate * jax.nn.sigmoid(gate) * up).astype(BF16)
        acc = acc + _dot(act, wd_ref[c:c + FFN_STEP, :])
    o_ref[...] = _rms(acc, gf_ref[...])


def _ffn(x2, g, wg, wu, wd, gf):
    t = x2.shape[0]
    tm = 512
    row = lambda i: (i, 0)
    return pl.pallas_call(
        _ffn_kernel,
        grid=(t // tm,),
        in_specs=[pl.BlockSpec((tm, D_MODEL), row),
                  _resident((1, D_MODEL)),
                  _resident((D_MODEL, FFN_HIDDEN)),
                  _resident((D_MODEL, FFN_HIDDEN)),
                  _resident((FFN_HIDDEN, D_MODEL)),
                  _resident((1, D_MODEL))],
        out_specs=pl.BlockSpec((tm, D_MODEL), row),
        out_shape=jax.ShapeDtypeStruct((t, D_MODEL), F32),
        compiler_params=_params(1),
        name="ffn",
    )(x2, g, wg, wu, wd, gf)


def _pad_rows(a, rows):
    return jnp.pad(a, ((0, rows - a.shape[0]), (0, 0)))


def _pad_cols(a, cols):
    return jnp.pad(a, ((0, 0), (0, cols - a.shape[1])))


def _select_matrix(slot):
    j = jnp.arange(LANES)[:, None]
    ch = jnp.arange(SSD_INNER)[None, :]
    e = (j == slot * SSD_HEADS + ch // SSD_HEAD_DIM).astype(BF16)
    return jnp.concatenate([e, e], axis=0)


def _layer(x2, mem2, bsz, seq, mem_len, norm_mix_g, w_in, conv_w, conv_b, dt_bias_fwd, dt_bias_bwd,
           a_log_fwd, a_log_bwd, d_skip, ssd_norm_g, attn_sink, w_branch_ssd, w_branch_attn, w_mix_out,
           norm_xattn_g, norm_mem_g, w_xattn_q, w_xattn_kv, w_xattn_out, norm_ffn_g, w_ffn_in, w_ffn_out,
           final_g):
    nc = seq // CHUNK
    s0 = SSD_INNER
    s1 = s0 + SSD_XBC
    s2 = s1 + 2 * SSD_HEADS
    s3 = s2 + ATTN_WIDTH
    s4 = s3 + ATTN_KV_WIDTH
    s5 = s4 + ATTN_KV_WIDTH
    w_packed = jnp.concatenate(
        [w_in[:, :s1], w_in[:, s2:s3], w_in[:, s3:s4], w_in[:, s4:s5], w_in[:, s5:],
         _pad_cols(w_in[:, s1:s2], LANES)], axis=1).astype(BF16)

    half = ATTN_HEAD_DIM // 2
    inv_freq = ROPE_THETA ** (-jnp.arange(half, dtype=F32) / half)
    ang = jnp.arange(seq, dtype=F32)[:, None] * inv_freq[None]
    cos = jnp.tile(jnp.cos(ang), (1, LANES // half))
    sin_signed = jnp.tile(jnp.concatenate([-jnp.sin(ang), jnp.sin(ang)], axis=1), (1, LANES // ATTN_HEAD_DIM))

    z, xbc, q, k, v, gates, dt = _in_proj(x2, norm_mix_g[None], w_packed, cos, sin_signed, seq)

    hp = _pad_rows(jnp.stack([_pad_cols(jnp.concatenate([dt_bias_fwd, dt_bias_bwd])[None], LANES)[0],
                              _pad_cols(jnp.concatenate([a_log_fwd, a_log_bwd])[None], LANES)[0]]), 8)
    cw = _pad_rows(conv_w, 8)
    cb = conv_b[None]
    e2f = _select_matrix(0)
    e2b = _select_matrix(1)
    width_b = SSD_INNER + SSD_BC
    hb = _ssd_bwd(xbc, dt, cw[:, :width_b], cb[:, :width_b], hp, e2b, bsz, nc)
    dskip = jnp.repeat(d_skip, SSD_HEAD_DIM)[None]
    y = _ssd_main(xbc, dt, z, hb, cw, cb, hp, dskip, ssd_norm_g[None], e2f, e2b, bsz, nc)

    sink_rows = jnp.broadcast_to(attn_sink[:, None], (ATTN_Q_HEADS, LANES))
    attn = _swa(q, k, v, sink_rows, bsz, seq // ATTN_BLOCK)

    x2 = _merge(x2, y, attn, gates, w_branch_ssd.astype(BF16), w_branch_attn.astype(BF16),
                w_mix_out.astype(BF16))

    kv = _mem_kv(mem2, norm_mem_g[None], w_xattn_kv.astype(BF16))
    x2 = _xattn(x2, norm_xattn_g[None], kv, w_xattn_q.astype(BF16), w_xattn_out.astype(BF16), seq, mem_len)

    wf = w_ffn_in.astype(BF16)
    return _ffn(x2, norm_ffn_g[None], wf[:, :FFN_HIDDEN], wf[:, FFN_HIDDEN:], w_ffn_out.astype(BF16), final_g)


def kernel(x, mem, norm_mix_g, w_in, conv_w, conv_b, dt_bias_fwd, dt_bias_bwd, a_log_fwd, a_log_bwd, d_skip,
           ssd_norm_g, attn_sink, w_branch_ssd, w_branch_attn, w_mix_out, norm_xattn_g, norm_mem_g, w_xattn_q,
           w_xattn_kv, w_xattn_out, norm_ffn_g, w_ffn_in, w_ffn_out, norm_final_g):
    bsz, seq, _ = x.shape
    mem_len = mem.shape[1]
    assert w_in.shape[0] == 1, "single-layer stack expected"
    layer = 0
    x2 = x.reshape(bsz * seq, D_MODEL)
    mem2 = mem.reshape(bsz * mem_len, D_MODEL)
    out = _layer(x2, mem2, bsz, seq, mem_len, norm_mix_g[layer], w_in[layer], conv_w[layer], conv_b[layer],
                 dt_bias_fwd[layer], dt_bias_bwd[layer], a_log_fwd[layer], a_log_bwd[layer], d_skip[layer],
                 ssd_norm_g[layer], attn_sink[layer], w_branch_ssd[layer], w_branch_attn[layer],
                 w_mix_out[layer], norm_xattn_g[layer], norm_mem_g[layer], w_xattn_q[layer],
                 w_xattn_kv[layer], w_xattn_out[layer], norm_ffn_g[layer], w_ffn_in[layer],
                 w_ffn_out[layer], norm_final_g[None])
    return out.reshape(bsz, seq, D_MODEL)
```

```python
import functools

import jax
import jax.numpy as jnp
from jax import lax
from jax.experimental import pallas as pl
from jax.experimental.pallas import tpu as pltpu

F32 = jnp.float32
BF16 = jnp.bfloat16

D_MODEL = 1024
EPS = 1e-6
SSD_INNER = 2048
SSD_HEAD_DIM = 64
SSD_HEADS = 32
SSD_GROUPS = 4
SSD_HPG = 8
SSD_STATE = 128
SSD_CONV = 5
CHUNK = 128
SSD_BC = SSD_GROUPS * SSD_STATE
SSD_XBC = SSD_INNER + 2 * SSD_BC
GROUP_W = SSD_HPG * SSD_HEAD_DIM
ATTN_HEAD_DIM = 64
ATTN_Q_HEADS = 16
ATTN_KV_HEADS = 4
ATTN_GQA = 4
ATTN_WIDTH = 1024
ATTN_KV_WIDTH = 256
ATTN_BLOCK = 128
ROPE_THETA = 10000.0
XATTN_HEADS = 4
XATTN_HEAD_DIM = 256
FFN_HIDDEN = 2816

LANES = 128
HALO_ROWS = 16
CONV_PAD = (SSD_CONV - 1) // 2
NEG_BIG = -1e30

OFF_Z = 0
OFF_XBC = OFF_Z + SSD_INNER
OFF_Q = OFF_XBC + SSD_XBC
OFF_K = OFF_Q + ATTN_WIDTH
OFF_V = OFF_K + ATTN_KV_WIDTH
OFF_G = OFF_V + ATTN_KV_WIDTH
OFF_DT = OFF_G + 2 * D_MODEL
IN_PACKED = OFF_DT + LANES

VMEM_LIMIT = 56 * 1024 * 1024


def _params(n_axes, flags=None):
    return pltpu.CompilerParams(dimension_semantics=("arbitrary",) * n_axes,
                                vmem_limit_bytes=VMEM_LIMIT, flags=flags)


def _resident(shape):
    nd = len(shape)
    return pl.BlockSpec(shape, lambda *_: (0,) * nd, pipeline_mode=pl.Buffered(1))


def _rms(x, g):
    return x * lax.rsqrt(jnp.mean(x * x, axis=-1, keepdims=True) + EPS) * g


def _dot(a, b):
    return jnp.dot(a, b, preferred_element_type=F32)


def _dot_nt(a, b):
    return lax.dot_general(a, b, (((1,), (1,)), ((), ())), preferred_element_type=F32)


def _rotary(u, cos, sin_signed):
    n = u.shape[1]
    lane = lax.broadcasted_iota(jnp.int32, u.shape, 1)
    first_half = (lane % ATTN_HEAD_DIM) < (ATTN_HEAD_DIM // 2)
    partner = jnp.where(first_half, pltpu.roll(u, n - ATTN_HEAD_DIM // 2, axis=1),
                        pltpu.roll(u, ATTN_HEAD_DIM // 2, axis=1))
    reps = n // LANES
    cos_t = jnp.concatenate([cos] * reps, axis=1) if reps > 1 else cos
    sin_t = jnp.concatenate([sin_signed] * reps, axis=1) if reps > 1 else sin_signed
    return u * cos_t + partner * sin_t


def _in_proj_kernel(x_ref, xp_ref, xn_ref, g_ref, w_ref, cos_ref, sin_ref, cw_ref, cb_ref,
                    z_ref, act_ref, q_ref, k_ref, v_ref, gate_ref, dt_ref, ext_ref, *, per_seq):
    i = pl.program_id(0)
    tm = x_ref.shape[0]
    g = g_ref[...]
    h_ext = _rms(jnp.concatenate([x_ref[...], xp_ref[...], xn_ref[...]], axis=0), g).astype(BF16)
    h = h_ext[:tm]

    def proj(off, width):
        return _dot(h, w_ref[:, off:off + width])

    step = 512
    has_prev = (i % per_seq) > 0
    has_next = (i % per_seq) < per_seq - 1
    cos = cos_ref[...]
    sin = sin_ref[...]

    def xbc_proj(c):
        u = _dot(h_ext, w_ref[:, OFF_XBC + c:OFF_XBC + c + step])
        cs = slice(c, c + step)
        ext_ref[0:8, cs] = jnp.where(has_prev, u[tm:tm + 8], 0.0)
        ext_ref[8:8 + tm, cs] = u[:tm]
        ext_ref[8 + tm:16 + tm, cs] = jnp.where(has_next, u[tm + 8:], 0.0)

    def conv_silu(c):
        cs = slice(c, c + step)
        e = ext_ref[:, cs]
        rows = e.shape[0]
        tap = lambda k: cw_ref[k:k + 1, cs] * e
        down = lambda a: pltpu.roll(a, 1, axis=0)
        up = lambda a: pltpu.roll(a, rows - 1, axis=0)
        left = down(down(tap(0)) + tap(1)) + tap(2)
        right = up(up(tap(4)) + tap(3))
        acc = (left + right)[8:8 + tm] + cb_ref[:, cs]
        act_ref[:, cs] = (acc * jax.nn.sigmoid(acc)).astype(BF16)

    def z_proj(c):
        z_ref[:, c:c + step] = proj(OFF_Z + c, step).astype(BF16)

    def q_proj(c):
        q = _rotary(proj(OFF_Q + c, step), cos, sin)
        q_ref[:, c:c + step] = (q * (ATTN_HEAD_DIM ** -0.5)).astype(BF16)

    def kv_proj(_):
        k_ref[...] = _rotary(proj(OFF_K, ATTN_KV_WIDTH), cos, sin).astype(BF16)
        v_ref[...] = proj(OFF_V, ATTN_KV_WIDTH).astype(BF16)
        dt_ref[...] = proj(OFF_DT, LANES)

    def gate_proj(c):
        gate_ref[:, c:c + step] = jax.nn.sigmoid(proj(OFF_G + c, step)).astype(BF16)

    mxu_tasks = ([(z_proj, c) for c in range(0, SSD_INNER, step)]
                 + [(q_proj, c) for c in range(0, ATTN_WIDTH, step)] + [(kv_proj, 0)]
                 + [(gate_proj, c) for c in range(0, 2 * D_MODEL, step)])
    conv_chunks = list(range(0, SSD_XBC, step))
    xbc_proj(conv_chunks[0])
    for j, c in enumerate(conv_chunks):
        if j + 1 < len(conv_chunks):
            xbc_proj(conv_chunks[j + 1])
        for fn, arg in mxu_tasks[2 * j:2 * j + 2]:
            fn(arg)
        conv_silu(c)
    for fn, arg in mxu_tasks[2 * len(conv_chunks):]:
        fn(arg)


def _in_proj(x2, g, w_packed, cos, sin_signed, conv_w, conv_b, seq):
    t = x2.shape[0]
    tm = 256
    per_seq = seq // tm
    halo_per_tile = tm // 8
    last_halo = t // 8 - 1
    row = lambda i: (i, 0)
    pos = lambda i: (i % per_seq, 0)
    outs = [(SSD_INNER, BF16), (SSD_XBC, BF16), (ATTN_WIDTH, BF16), (ATTN_KV_WIDTH, BF16),
            (ATTN_KV_WIDTH, BF16), (2 * D_MODEL, BF16), (LANES, F32)]
    return pl.pallas_call(
        functools.partial(_in_proj_kernel, per_seq=per_seq),
        grid=(t // tm,),
        in_specs=[pl.BlockSpec((tm, D_MODEL), row),
                  pl.BlockSpec((8, D_MODEL), lambda i: (jnp.maximum(i * halo_per_tile - 1, 0), 0)),
                  pl.BlockSpec((8, D_MODEL), lambda i: (jnp.minimum((i + 1) * halo_per_tile, last_halo), 0)),
                  _resident((1, D_MODEL)),
                  _resident((D_MODEL, IN_PACKED)),
                  pl.BlockSpec((tm, LANES), pos),
                  pl.BlockSpec((tm, LANES), pos),
                  _resident((8, SSD_XBC)),
                  _resident((1, SSD_XBC))],
        out_specs=[pl.BlockSpec((tm, w), row) for w, _ in outs],
        out_shape=[jax.ShapeDtypeStruct((t, w), d) for w, d in outs],
        scratch_shapes=[pltpu.VMEM((tm + 16, SSD_XBC), F32)],
        compiler_params=_params(1),
        name="in_proj",
    )(x2, x2, x2, g, w_packed, cos, sin_signed, conv_w, conv_b)


def _cumsum_rows(a):
    row = lax.broadcasted_iota(jnp.int32, a.shape, 0)
    v = a
    k = 1
    while k < a.shape[0]:
        v = v + jnp.where(row >= k, pltpu.roll(v, k, axis=0), 0.0)
        k *= 2
    return v


def _expand(v, e2_ref):
    hi = v.astype(BF16)
    lo = (v - hi.astype(F32)).astype(BF16)
    return _dot(jnp.concatenate([hi, lo], axis=1), e2_ref[...])


def _softplus(x):
    return jnp.maximum(x, 0.0) + jnp.log1p(jnp.exp(-jnp.abs(x)))


def _head_scalars(dt_ref, hp_ref):
    dtv = _softplus(dt_ref[...] + hp_ref[0:1, :])
    a = dtv * (-jnp.exp(hp_ref[1:2, :]))
    return dtv, a


def _ssd_bwd_kernel(act_ref, dt_ref, hp_ref, e2b_ref, hb_ref, state_ref):
    c = pl.program_id(1)

    @pl.when(c == 0)
    def _():
        state_ref[...] = jnp.zeros_like(state_ref)

    hb_ref[0] = state_ref[...].astype(BF16)

    xs = act_ref[:, :SSD_INNER].astype(F32)
    bm = act_ref[:, SSD_INNER:].astype(F32)
    dtv, a = _head_scalars(dt_ref, hp_ref)
    incl = _cumsum_rows(a)
    excl = incl - a
    xw = (xs * _expand(jnp.exp(excl) * dtv, e2b_ref)).astype(BF16)
    decay = _expand(jnp.exp(incl[CHUNK - 8:, :]), e2b_ref)[7:8, :]
    for g in range(SSD_GROUPS):
        gs = slice(g * GROUP_W, (g + 1) * GROUP_W)
        bt = bm[:, g * SSD_STATE:(g + 1) * SSD_STATE].T.astype(BF16)
        state_ref[:, gs] = state_ref[:, gs] * decay[:, gs] + _dot(bt, xw[:, gs])


def _ssd_bwd(act, dt, hp, e2b, bsz, nc):
    width = SSD_INNER + SSD_BC
    chunk_of = lambda b, c: b * nc + (nc - 1 - c)
    return pl.pallas_call(
        _ssd_bwd_kernel,
        grid=(bsz, nc),
        in_specs=[pl.BlockSpec((CHUNK, width), lambda b, c: (chunk_of(b, c), 0)),
                  pl.BlockSpec((CHUNK, LANES), lambda b, c: (chunk_of(b, c), 0)),
                  _resident((8, LANES)),
                  _resident((2 * LANES, SSD_INNER))],
        out_specs=pl.BlockSpec((1, SSD_STATE, SSD_INNER), lambda b, c: (chunk_of(b, c), 0, 0)),
        out_shape=jax.ShapeDtypeStruct((bsz * nc, SSD_STATE, SSD_INNER), BF16),
        scratch_shapes=[pltpu.VMEM((SSD_STATE, SSD_INNER), F32)],
        compiler_params=_params(2),
        name="ssd_bwd",
    )(act, dt, hp, e2b)


def _ssd_main_kernel(act_ref, dt_ref, z_ref, hb_ref, hp_ref, dskip_ref, ng_ref, e2f_ref, e2b_ref, sel_ref,
                     y_ref, state_ref, ybuf_ref):
    c = pl.program_id(1)

    @pl.when(c == 0)
    def _():
        state_ref[...] = jnp.zeros_like(state_ref)

    xs16 = act_ref[:, :SSD_INNER]
    xs = xs16.astype(F32)
    bm16 = act_ref[:, SSD_INNER:SSD_INNER + SSD_BC]
    cm16 = act_ref[:, SSD_INNER + SSD_BC:]

    dtv, a = _head_scalars(dt_ref, hp_ref)
    incl = _cumsum_rows(a)
    excl = incl - a
    tot = incl[CHUNK - 1:CHUNK, :]
    incl_t = incl.T
    excl_t = excl.T
    dtv_t = dtv.T

    row = lax.broadcasted_iota(jnp.int32, (CHUNK, CHUNK), 0)
    col = lax.broadcasted_iota(jnp.int32, (CHUNK, CHUNK), 1)
    lower = row >= col
    strict_lower = row > col
    strict_upper = row < col

    lane = lax.broadcasted_iota(jnp.int32, incl.shape, 1)
    vmix = jnp.where(lane < SSD_HEADS, incl, excl)
    vhi = vmix.astype(BF16)
    pieces = jnp.concatenate([vhi, (vmix - vhi.astype(F32)).astype(BF16)], axis=1)
    first_head = lax.broadcasted_iota(jnp.int32, (CHUNK, 2 * SSD_HEAD_DIM), 1) < SSD_HEAD_DIM
    gw = SSD_HPG * CHUNK

    for g in range(SSD_GROUPS):
        bg = bm16[:, g * SSD_STATE:(g + 1) * SSD_STATE]
        cg = cm16[:, g * SSD_STATE:(g + 1) * SSD_STATE]
        cb = _dot_nt(cg, bg)
        col_f = _dot(pieces, sel_ref[:, g * gw:(g + 1) * gw])
        col_b = _dot(pieces, sel_ref[:, (SSD_GROUPS + g) * gw:(SSD_GROUPS + g + 1) * gw])
        for pair in range(SSD_HPG // 2):
            ms = []
            for r in (2 * pair, 2 * pair + 1):
                h = g * SSD_HPG + r
                hb_col = SSD_HEADS + h
                rs = slice(r * CHUNK, (r + 1) * CHUNK)
                seg_f = col_f[:, rs] - incl_t[h:h + 1, :]
                seg_b = excl_t[hb_col:hb_col + 1, :] - col_b[:, rs]
                w = jnp.exp(jnp.where(lower, seg_f, seg_b))
                dt_f = dtv_t[h:h + 1, :]
                dt_b = dtv_t[hb_col:hb_col + 1, :]
                d = jnp.where(strict_lower, dt_f, jnp.where(strict_upper, dt_b, dt_f + dt_b))
                ms.append((cb * w * d).astype(BF16))
            ps = slice((g * SSD_HPG + 2 * pair) * SSD_HEAD_DIM, (g * SSD_HPG + 2 * pair + 2) * SSD_HEAD_DIM)
            x2 = xs16[:, ps]
            zero = jnp.zeros_like(x2)
            rhs = jnp.concatenate([jnp.where(first_head, x2, zero), jnp.where(first_head, zero, x2)], axis=0)
            ybuf_ref[:, ps] = _dot(jnp.concatenate(ms, axis=1), rhs)

    scale_f = _expand(jnp.exp(incl), e2f_ref)
    scale_b = _expand(jnp.exp(tot - excl), e2b_ref)
    xw = (xs * _expand(jnp.exp(tot - incl) * dtv, e2f_ref)).astype(BF16)
    decay = _expand(jnp.exp(incl[CHUNK - 8:, :]), e2f_ref)[7:8, :]
    zf = z_ref[...].astype(F32)
    zgate = zf * jax.nn.sigmoid(zf)
    for g in range(SSD_GROUPS):
        gs = slice(g * GROUP_W, (g + 1) * GROUP_W)
        ns = slice(g * SSD_STATE, (g + 1) * SSD_STATE)
        cg = cm16[:, ns]
        st = state_ref[:, gs]
        y = (ybuf_ref[:, gs]
             + scale_f[:, gs] * _dot(cg, st.astype(BF16))
             + scale_b[:, gs] * _dot(cg, hb_ref[0, :, gs])
             + dskip_ref[:, gs] * xs[:, gs])
        v = y * zgate[:, gs]
        v = v * lax.rsqrt(jnp.mean(v * v, axis=-1, keepdims=True) + EPS)
        y_ref[:, gs] = (v * ng_ref[:, gs]).astype(BF16)
        bt = bm16[:, ns].astype(F32).T.astype(BF16)
        state_ref[:, gs] = st * decay[:, gs] + _dot(bt, xw[:, gs])


def _ssd_main(act, dt, z, hb, hp, dskip, norm_g, e2f, e2b, sel, bsz, nc):
    chunk_of = lambda b, c: b * nc + c
    return pl.pallas_call(
        _ssd_main_kernel,
        grid=(bsz, nc),
        in_specs=[pl.BlockSpec((CHUNK, SSD_XBC), lambda b, c: (chunk_of(b, c), 0)),
                  pl.BlockSpec((CHUNK, LANES), lambda b, c: (chunk_of(b, c), 0)),
                  pl.BlockSpec((CHUNK, SSD_INNER), lambda b, c: (chunk_of(b, c), 0)),
                  pl.BlockSpec((1, SSD_STATE, SSD_INNER), lambda b, c: (chunk_of(b, c), 0, 0)),
                  _resident((8, LANES)),
                  _resident((1, SSD_INNER)),
                  _resident((1, SSD_INNER)),
                  _resident((2 * LANES, SSD_INNER)),
                  _resident((2 * LANES, SSD_INNER)),
                  _resident((2 * LANES, 2 * SSD_HEADS * CHUNK))],
        out_specs=pl.BlockSpec((CHUNK, SSD_INNER), lambda b, c: (chunk_of(b, c), 0)),
        out_shape=jax.ShapeDtypeStruct((bsz * nc * CHUNK, SSD_INNER), BF16),
        scratch_shapes=[pltpu.VMEM((SSD_STATE, SSD_INNER), F32),
                        pltpu.VMEM((CHUNK, SSD_INNER), F32)],
        compiler_params=_params(2),
        name="ssd_main",
    )(act, dt, z, hb, hp, dskip, norm_g, e2f, e2b, sel)


def _swa_kernel(q_ref, kp_ref, kc_ref, kn_ref, vp_ref, vc_ref, vn_ref, sink_ref, o_ref):
    n = pl.program_id(1)
    nb = pl.num_programs(1)
    kj = lax.broadcasted_iota(jnp.int32, (3 * ATTN_BLOCK, ATTN_BLOCK), 0)
    qi = lax.broadcasted_iota(jnp.int32, (3 * ATTN_BLOCK, ATTN_BLOCK), 1)
    rel = kj - qi
    valid = (rel >= 0) & (rel <= 2 * ATTN_BLOCK)
    valid = valid & ((kj >= ATTN_BLOCK) | (n > 0)) & ((kj < 2 * ATTN_BLOCK) | (n < nb - 1))
    bias1 = jnp.where(valid, 0.0, NEG_BIG)
    bias = jnp.concatenate([bias1] * ATTN_GQA, axis=1)
    kcat = jnp.concatenate([kp_ref[...], kc_ref[...], kn_ref[...]], axis=0)
    vcat_t = jnp.concatenate([vp_ref[...], vc_ref[...], vn_ref[...]], axis=0).astype(F32).T.astype(BF16)
    outs = []
    for kv in range(ATTN_KV_HEADS):
        ks = slice(kv * ATTN_HEAD_DIM, (kv + 1) * ATTN_HEAD_DIM)
        q_stack = jnp.concatenate(
            [q_ref[:, (kv * ATTN_GQA + r) * ATTN_HEAD_DIM:(kv * ATTN_GQA + r + 1) * ATTN_HEAD_DIM]
             for r in range(ATTN_GQA)], axis=0)
        s = _dot_nt(kcat[:, ks], q_stack) + bias
        sink = sink_ref[kv:kv + 1, :]
        m = jnp.maximum(jnp.max(s, axis=0, keepdims=True), sink)
        p = jnp.exp(s - m)
        denom = jnp.sum(p, axis=0, keepdims=True) + jnp.exp(sink - m)
        o_t = _dot(vcat_t[ks, :], p.astype(BF16)) * (1.0 / denom)
        outs.extend(o_t[:, r * ATTN_BLOCK:(r + 1) * ATTN_BLOCK] for r in range(ATTN_GQA))
    o_ref[...] = jnp.concatenate(outs, axis=0).T.astype(BF16)


def _swa(q, k, v, sink_rows, bsz, nb):
    blk = lambda b, n: b * nb + n
    cur = lambda b, n: (blk(b, n), 0)
    prev = lambda b, n: (blk(b, jnp.maximum(n - 1, 0)), 0)
    nxt = lambda b, n: (blk(b, jnp.minimum(n + 1, nb - 1)), 0)
    kvspec = lambda im: pl.BlockSpec((ATTN_BLOCK, ATTN_KV_WIDTH), im)
    return pl.pallas_call(
        _swa_kernel,
        grid=(bsz, nb),
        in_specs=[pl.BlockSpec((ATTN_BLOCK, ATTN_WIDTH), cur),
                  kvspec(prev), kvspec(cur), kvspec(nxt),
                  kvspec(prev), kvspec(cur), kvspec(nxt),
                  _resident((8, ATTN_GQA * ATTN_BLOCK))],
        out_specs=pl.BlockSpec((ATTN_BLOCK, ATTN_WIDTH), cur),
        out_shape=jax.ShapeDtypeStruct(q.shape, BF16),
        compiler_params=_params(2),
        name="swa",
    )(q, k, k, k, v, v, v, sink_rows)


def _merge_kernel(x_ref, y_ref, a_ref, gate_ref, ws_ref, wa_ref, wo_ref, o_ref):
    bs = _dot(y_ref[...], ws_ref[...])
    ba = _dot(a_ref[...], wa_ref[...])
    g = gate_ref[...].astype(F32)
    mix = (g[:, :D_MODEL] * bs + g[:, D_MODEL:] * ba).astype(BF16)
    o_ref[...] = x_ref[...] + _dot(mix, wo_ref[...])


def _merge(x2, y, attn, gates, ws, wa, wo):
    t = x2.shape[0]
    tm = 512
    row = lambda i: (i, 0)
    return pl.pallas_call(
        _merge_kernel,
        grid=(t // tm,),
        in_specs=[pl.BlockSpec((tm, D_MODEL), row),
                  pl.BlockSpec((tm, SSD_INNER), row),
                  pl.BlockSpec((tm, ATTN_WIDTH), row),
                  pl.BlockSpec((tm, 2 * D_MODEL), row),
                  _resident((SSD_INNER, D_MODEL)),
                  _resident((ATTN_WIDTH, D_MODEL)),
                  _resident((D_MODEL, D_MODEL))],
        out_specs=pl.BlockSpec((tm, D_MODEL), row),
        out_shape=jax.ShapeDtypeStruct((t, D_MODEL), F32),
        compiler_params=_params(1),
        name="merge",
    )(x2, y, attn, gates, ws, wa, wo)


def _mem_kv_kernel(m_ref, g_ref, w_ref, o_ref):
    h = _rms(m_ref[...], g_ref[...]).astype(BF16)
    o_ref[...] = _dot(h, w_ref[...]).astype(BF16)


def _mem_kv(mem2, g, w_kv):
    t = mem2.shape[0]
    tm = 256
    row = lambda i: (i, 0)
    return pl.pallas_call(
        _mem_kv_kernel,
        grid=(t // tm,),
        in_specs=[pl.BlockSpec((tm, D_MODEL), row),
                  _resident((1, D_MODEL)),
                  _resident((D_MODEL, 2 * D_MODEL))],
        out_specs=pl.BlockSpec((tm, 2 * D_MODEL), row),
        out_shape=jax.ShapeDtypeStruct((t, 2 * D_MODEL), BF16),
        compiler_params=_params(1),
        name="mem_kv",
    )(mem2, g, w_kv)


def _xattn_kernel(x_ref, g_ref, kv_ref, wq_ref, wo_ref, o_ref, ctx_ref):
    x = x_ref[...]
    h = _rms(x, g_ref[...]).astype(BF16)
    q = (_dot(h, wq_ref[...]) * (XATTN_HEAD_DIM ** -0.5)).astype(BF16)
    for hd in range(XATTN_HEADS):
        ks = slice(hd * XATTN_HEAD_DIM, (hd + 1) * XATTN_HEAD_DIM)
        vs = slice(D_MODEL + hd * XATTN_HEAD_DIM, D_MODEL + (hd + 1) * XATTN_HEAD_DIM)
        s = _dot_nt(q[:, ks], kv_ref[:, ks])
        m = jnp.max(s, axis=-1, keepdims=True)
        p = jnp.exp(s - m)
        denom = jnp.sum(p, axis=-1, keepdims=True)
        ctx = _dot(p.astype(BF16), kv_ref[:, vs]) * (1.0 / denom)
        ctx_ref[:, ks] = ctx.astype(BF16)
    o_ref[...] = x + _dot(ctx_ref[...], wo_ref[...])


def _xattn(x2, g, kv, wq, wo, seq, mem_len):
    t = x2.shape[0]
    tm = 512
    per_seq = seq // tm
    row = lambda i: (i, 0)
    return pl.pallas_call(
        _xattn_kernel,
        grid=(t // tm,),
        in_specs=[pl.BlockSpec((tm, D_MODEL), row),
                  _resident((1, D_MODEL)),
                  pl.BlockSpec((mem_len, 2 * D_MODEL), lambda i: (i // per_seq, 0)),
                  _resident((D_MODEL, D_MODEL)),
                  _resident((D_MODEL, D_MODEL))],
        out_specs=pl.BlockSpec((tm, D_MODEL), row),
        out_shape=jax.ShapeDtypeStruct((t, D_MODEL), F32),
        scratch_shapes=[pltpu.VMEM((tm, D_MODEL), BF16)],
        compiler_params=_params(1),
        name="xattn",
    )(x2, g, kv, wq, wo)


FFN_STEP = 1408


def _ffn_kernel(x_ref, g_ref, wg_ref, wu_ref, wd_ref, gf_ref, o_ref):
    x = x_ref[...]
    h = _rms(x, g_ref[...]).astype(BF16)
    acc = x
    for c in range(0, FFN_HIDDEN, FFN_STEP):
        gate = _dot(h, wg_ref[:, c:c + FFN_STEP])
        up = _dot(h, wu_ref[:, c:c + FFN_STEP])
        act = (gate * jax.nn.sigmoid(gate) * up).astype(BF16)
        acc = acc + _dot(act, wd_ref[c:c + FFN_STEP, :])
    o_ref[...] = _rms(acc, gf_ref[...])


def _ffn(x2, g, wg, wu, wd, gf):
    t = x2.shape[0]
    tm = 512
    row = lambda i: (i, 0)
    return pl.pallas_call(
        _ffn_kernel,
        grid=(t // tm,),
        in_specs=[pl.BlockSpec((tm, D_MODEL), row),
                  _resident((1, D_MODEL)),
                  _resident((D_MODEL, FFN_HIDDEN)),
                  _resident((D_MODEL, FFN_HIDDEN)),
                  _resident((FFN_HIDDEN, D_MODEL)),
                  _resident((1, D_MODEL))],
        out_specs=pl.BlockSpec((tm, D_MODEL), row),
        out_shape=jax.ShapeDtypeStruct((t, D_MODEL), F32),
        compiler_params=_params(1),
        name="ffn",
    )(x2, g, wg, wu, wd, gf)


def _pad_rows(a, rows):
    return jnp.pad(a, ((0, rows - a.shape[0]), (0, 0)))


def _pad_cols(a, cols):
    return jnp.pad(a, ((0, 0), (0, cols - a.shape[1])))


def _column_select_matrix():
    j = jnp.arange(LANES)[:, None]
    blk = jnp.arange(2 * SSD_HEADS * CHUNK)[None, :] // CHUNK
    e = (j == blk).astype(BF16)
    return jnp.concatenate([e, e], axis=0)


def _select_matrix(slot):
    j = jnp.arange(LANES)[:, None]
    ch = jnp.arange(SSD_INNER)[None, :]
    e = (j == slot * SSD_HEADS + ch // SSD_HEAD_DIM).astype(BF16)
    return jnp.concatenate([e, e], axis=0)


def _layer(x2, mem2, bsz, seq, mem_len, norm_mix_g, w_in, conv_w, conv_b, dt_bias_fwd, dt_bias_bwd,
           a_log_fwd, a_log_bwd, d_skip, ssd_norm_g, attn_sink, w_branch_ssd, w_branch_attn, w_mix_out,
           norm_xattn_g, norm_mem_g, w_xattn_q, w_xattn_kv, w_xattn_out, norm_ffn_g, w_ffn_in, w_ffn_out,
           final_g):
    nc = seq // CHUNK
    s0 = SSD_INNER
    s1 = s0 + SSD_XBC
    s2 = s1 + 2 * SSD_HEADS
    s3 = s2 + ATTN_WIDTH
    s4 = s3 + ATTN_KV_WIDTH
    s5 = s4 + ATTN_KV_WIDTH
    w_packed = jnp.concatenate(
        [w_in[:, :s1], w_in[:, s2:s3], w_in[:, s3:s4], w_in[:, s4:s5], w_in[:, s5:],
         _pad_cols(w_in[:, s1:s2], LANES)], axis=1).astype(BF16)

    half = ATTN_HEAD_DIM // 2
    inv_freq = ROPE_THETA ** (-jnp.arange(half, dtype=F32) / half)
    ang = jnp.arange(seq, dtype=F32)[:, None] * inv_freq[None]
    cos = jnp.tile(jnp.cos(ang), (1, LANES // half))
    sin_signed = jnp.tile(jnp.concatenate([-jnp.sin(ang), jnp.sin(ang)], axis=1), (1, LANES // ATTN_HEAD_DIM))

    cw = _pad_rows(conv_w, 8)
    z, act, q, k, v, gates, dt = _in_proj(x2, norm_mix_g[None], w_packed, cos, sin_signed, cw, conv_b[None], seq)

    hp = _pad_rows(jnp.stack([_pad_cols(jnp.concatenate([dt_bias_fwd, dt_bias_bwd])[None], LANES)[0],
                              _pad_cols(jnp.concatenate([a_log_fwd, a_log_bwd])[None], LANES)[0]]), 8)
    e2f = _select_matrix(0)
    e2b = _select_matrix(1)
    hb = _ssd_bwd(act, dt, hp, e2b, bsz, nc)
    dskip = jnp.repeat(d_skip, SSD_HEAD_DIM)[None]
    y = _ssd_main(act, dt, z, hb, hp, dskip, ssd_norm_g[None], e2f, e2b, _column_select_matrix(), bsz, nc)

    sink_rows = _pad_rows(jnp.repeat(attn_sink, ATTN_BLOCK).reshape(ATTN_KV_HEADS, ATTN_GQA * ATTN_BLOCK), 8)
    attn = _swa(q, k, v, sink_rows, bsz, seq // ATTN_BLOCK)

    x2 = _merge(x2, y, attn, gates, w_branch_ssd.astype(BF16), w_branch_attn.astype(BF16),
                w_mix_out.astype(BF16))

    kv = _mem_kv(mem2, norm_mem_g[None], w_xattn_kv.astype(BF16))
    x2 = _xattn(x2, norm_xattn_g[None], kv, w_xattn_q.astype(BF16), w_xattn_out.astype(BF16), seq, mem_len)

    wf = w_ffn_in.astype(BF16)
    return _ffn(x2, norm_ffn_g[None], wf[:, :FFN_HIDDEN], wf[:, FFN_HIDDEN:], w_ffn_out.astype(BF16), final_g)


def kernel(x, mem, norm_mix_g, w_in, conv_w, conv_b, dt_bias_fwd, dt_bias_bwd, a_log_fwd, a_log_bwd, d_skip,
           ssd_norm_g, attn_sink, w_branch_ssd, w_branch_attn, w_mix_out, norm_xattn_g, norm_mem_g, w_xattn_q,
           w_xattn_kv, w_xattn_out, norm_ffn_g, w_ffn_in, w_ffn_out, norm_final_g):
    bsz, seq, _ = x.shape
    mem_len = mem.shape[1]
    assert w_in.shape[0] == 1, "single-layer stack expected"
    layer = 0
    x2 = x.reshape(bsz * seq, D_MODEL)
    mem2 = mem.reshape(bsz * mem_len, D_MODEL)
    out = _layer(x2, mem2, bsz, seq, mem_len, norm_mix_g[layer], w_in[layer], conv_w[layer], conv_b[layer],
                 dt_bias_fwd[layer], dt_bias_bwd[layer], a_log_fwd[layer], a_log_bwd[layer], d_skip[layer],
                 ssd_norm_g[layer], attn_sink[layer], w_branch_ssd[layer], w_branch_attn[layer],
                 w_mix_out[layer], norm_xattn_g[layer], norm_mem_g[layer], w_xattn_q[layer],
                 w_xattn_kv[layer], w_xattn_out[layer], norm_ffn_g[layer], w_ffn_in[layer],
                 w_ffn_out[layer], norm_final_g[None])
    return out.reshape(bsz, seq, D_MODEL)
```

```python
import functools

import jax
import jax.numpy as jnp
from jax import lax
from jax.experimental import pallas as pl
from jax.experimental.pallas import tpu as pltpu

F32 = jnp.float32
BF16 = jnp.bfloat16

D_MODEL = 1024
EPS = 1e-6
SSD_INNER = 2048
SSD_HEAD_DIM = 64
SSD_HEADS = 32
SSD_GROUPS = 4
SSD_HPG = 8
SSD_STATE = 128
SSD_CONV = 5
CHUNK = 128
SSD_BC = SSD_GROUPS * SSD_STATE
SSD_XBC = SSD_INNER + 2 * SSD_BC
GROUP_W = SSD_HPG * SSD_HEAD_DIM
ATTN_HEAD_DIM = 64
ATTN_Q_HEADS = 16
ATTN_KV_HEADS = 4
ATTN_GQA = 4
ATTN_WIDTH = 1024
ATTN_KV_WIDTH = 256
ATTN_BLOCK = 128
ROPE_THETA = 10000.0
XATTN_HEADS = 4
XATTN_HEAD_DIM = 256
FFN_HIDDEN = 2816

LANES = 128
HALO_ROWS = 16
CONV_PAD = (SSD_CONV - 1) // 2
NEG_BIG = -1e30
LOG2_E = 1.4426950408889634

OFF_Z = 0
OFF_XBC = OFF_Z + SSD_INNER
OFF_Q = OFF_XBC + SSD_XBC
OFF_K = OFF_Q + ATTN_WIDTH
OFF_V = OFF_K + ATTN_KV_WIDTH
OFF_G = OFF_V + ATTN_KV_WIDTH
OFF_DT = OFF_G + 2 * D_MODEL
IN_PACKED = OFF_DT + LANES

VMEM_LIMIT = 56 * 1024 * 1024


def _params(n_axes, flags=None):
    return pltpu.CompilerParams(dimension_semantics=("arbitrary",) * n_axes,
                                vmem_limit_bytes=VMEM_LIMIT, flags=flags)


def _resident(shape):
    nd = len(shape)
    return pl.BlockSpec(shape, lambda *_: (0,) * nd, pipeline_mode=pl.Buffered(1))


def _rms(x, g):
    return x * lax.rsqrt(jnp.mean(x * x, axis=-1, keepdims=True) + EPS) * g


def _dot(a, b):
    return jnp.dot(a, b, preferred_element_type=F32)


def _dot_nt(a, b):
    return lax.dot_general(a, b, (((1,), (1,)), ((), ())), preferred_element_type=F32)


def _rotary(u, cos, sin_signed):
    n = u.shape[1]
    lane = lax.broadcasted_iota(jnp.int32, u.shape, 1)
    first_half = (lane % ATTN_HEAD_DIM) < (ATTN_HEAD_DIM // 2)
    partner = jnp.where(first_half, pltpu.roll(u, n - ATTN_HEAD_DIM // 2, axis=1),
                        pltpu.roll(u, ATTN_HEAD_DIM // 2, axis=1))
    reps = n // LANES
    cos_t = jnp.concatenate([cos] * reps, axis=1) if reps > 1 else cos
    sin_t = jnp.concatenate([sin_signed] * reps, axis=1) if reps > 1 else sin_signed
    return u * cos_t + partner * sin_t


def _in_proj_kernel(x_ref, xp_ref, xn_ref, g_ref, w_ref, cos_ref, sin_ref, cw_ref, cb_ref,
                    z_ref, act_ref, q_ref, kv_ref, gate_ref, dt_ref, ext_ref, *, per_seq):
    i = pl.program_id(0)
    tm = x_ref.shape[0]
    g = g_ref[...]
    h_ext = _rms(jnp.concatenate([x_ref[...], xp_ref[...], xn_ref[...]], axis=0), g).astype(BF16)
    step = 256
    rblk = 128
    row_blocks = [slice(r, r + rblk) for r in range(0, tm, rblk)]
    has_prev = (i % per_seq) > 0
    has_next = (i % per_seq) < per_seq - 1

    def proj(rs, off, width):
        return _dot(h_ext[rs], w_ref[:, off:off + width])

    def xbc_proj(c):
        cs = slice(c, c + step)
        for rs in row_blocks[:-1]:
            ext_ref[8 + rs.start:8 + rs.stop, cs] = proj(rs, OFF_XBC + c, step)
        last = row_blocks[-1]
        u = proj(slice(last.start, tm + 16), OFF_XBC + c, step)
        ext_ref[8 + last.start:8 + tm, cs] = u[:rblk]
        ext_ref[0:8, cs] = jnp.where(has_prev, u[rblk:rblk + 8], 0.0)
        ext_ref[8 + tm:16 + tm, cs] = jnp.where(has_next, u[rblk + 8:], 0.0)

    def conv_silu(c):
        cs = slice(c, c + step)
        e = ext_ref[:, cs]
        rows = e.shape[0]
        tap = lambda k: cw_ref[k:k + 1, cs] * e
        down = lambda a: pltpu.roll(a, 1, axis=0)
        up = lambda a: pltpu.roll(a, rows - 1, axis=0)
        left = down(down(tap(0)) + tap(1)) + tap(2)
        right = up(up(tap(4)) + tap(3))
        acc = (left + right)[8:8 + tm] + cb_ref[:, cs]
        act_ref[:, cs] = (acc * jax.nn.sigmoid(acc)).astype(BF16)

    def z_proj(c):
        for rs in row_blocks:
            z_ref[rs, c:c + step] = proj(rs, OFF_Z + c, step).astype(BF16)

    def q_proj(c):
        for rs in row_blocks:
            q = _rotary(proj(rs, OFF_Q + c, step), cos_ref[rs, :], sin_ref[rs, :])
            q_ref[rs, c:c + step] = (q * (ATTN_HEAD_DIM ** -0.5)).astype(BF16)

    def kv_proj(_):
        for rs in row_blocks:
            kv = proj(rs, OFF_K, 2 * ATTN_KV_WIDTH)
            kv_ref[rs, :ATTN_KV_WIDTH] = _rotary(kv[:, :ATTN_KV_WIDTH], cos_ref[rs, :], sin_ref[rs, :]).astype(BF16)
            kv_ref[rs, ATTN_KV_WIDTH:] = kv[:, ATTN_KV_WIDTH:].astype(BF16)
            dt_ref[rs, :] = proj(rs, OFF_DT, LANES)

    def gate_proj(c):
        for rs in row_blocks:
            gate_ref[rs, c:c + step] = proj(rs, OFF_G + c, step).astype(BF16)

    mxu_tasks = ([(z_proj, c) for c in range(0, SSD_INNER, step)]
                 + [(q_proj, c) for c in range(0, ATTN_WIDTH, step)] + [(kv_proj, 0)]
                 + [(gate_proj, c) for c in range(0, 2 * D_MODEL, step)])
    conv_chunks = list(range(0, SSD_XBC, step))
    xbc_proj(conv_chunks[0])
    for j, c in enumerate(conv_chunks):
        if j + 1 < len(conv_chunks):
            xbc_proj(conv_chunks[j + 1])
        for fn, arg in mxu_tasks[2 * j:2 * j + 2]:
            fn(arg)
        conv_silu(c)
    for fn, arg in mxu_tasks[2 * len(conv_chunks):]:
        fn(arg)


def _in_proj(x2, g, w_packed, cos, sin_signed, conv_w, conv_b, seq):
    t = x2.shape[0]
    tm = 512
    per_seq = seq // tm
    halo_per_tile = tm // 8
    last_halo = t // 8 - 1
    row = lambda i: (i, 0)
    pos = lambda i: (i % per_seq, 0)
    outs = [(SSD_INNER, BF16), (SSD_XBC, BF16), (ATTN_WIDTH, BF16), (2 * ATTN_KV_WIDTH, BF16),
            (2 * D_MODEL, BF16), (LANES, F32)]
    return pl.pallas_call(
        functools.partial(_in_proj_kernel, per_seq=per_seq),
        grid=(t // tm,),
        in_specs=[pl.BlockSpec((tm, D_MODEL), row),
                  pl.BlockSpec((8, D_MODEL), lambda i: (jnp.maximum(i * halo_per_tile - 1, 0), 0)),
                  pl.BlockSpec((8, D_MODEL), lambda i: (jnp.minimum((i + 1) * halo_per_tile, last_halo), 0)),
                  _resident((1, D_MODEL)),
                  _resident((D_MODEL, IN_PACKED)),
                  pl.BlockSpec((tm, LANES), pos),
                  pl.BlockSpec((tm, LANES), pos),
                  _resident((8, SSD_XBC)),
                  _resident((1, SSD_XBC))],
        out_specs=[pl.BlockSpec((tm, w), row) for w, _ in outs],
        out_shape=[jax.ShapeDtypeStruct((t, w), d) for w, d in outs],
        scratch_shapes=[pltpu.VMEM((tm + 16, SSD_XBC), F32)],
        compiler_params=_params(1),
        name="in_proj",
    )(x2, x2, x2, g, w_packed, cos, sin_signed, conv_w, conv_b)


def _cumsum_rows(a):
    row = lax.broadcasted_iota(jnp.int32, a.shape, 0)
    v = a
    k = 1
    while k < a.shape[0]:
        v = v + jnp.where(row >= k, pltpu.roll(v, k, axis=0), 0.0)
        k *= 2
    return v


def _expand(v, e2_ref):
    hi = v.astype(BF16)
    lo = (v - hi.astype(F32)).astype(BF16)
    return _dot(jnp.concatenate([hi, lo], axis=1), e2_ref[...])


def _softplus(x):
    return jnp.maximum(x, 0.0) + jnp.log1p(jnp.exp(-jnp.abs(x)))


def _head_scalars(dt_ref, hp_ref):
    dtv = _softplus(dt_ref[...] + hp_ref[0:1, :])
    a = dtv * (-LOG2_E * jnp.exp(hp_ref[1:2, :]))
    return dtv, a


def _ssd_bwd_kernel(act_ref, dt_ref, hp_ref, e2b_ref, hb_ref, state_ref):
    c = pl.program_id(1)

    @pl.when(c == 0)
    def _():
        state_ref[...] = jnp.zeros_like(state_ref)

    hb_ref[0] = state_ref[...].astype(BF16)

    xs = act_ref[:, :SSD_INNER].astype(F32)
    bm = act_ref[:, SSD_INNER:].astype(F32)
    dtv, a = _head_scalars(dt_ref, hp_ref)
    incl = _cumsum_rows(a)
    excl = incl - a
    xw = (xs * _expand(jnp.exp2(excl) * dtv, e2b_ref)).astype(BF16)
    decay = _expand(jnp.exp2(incl[CHUNK - 8:, :]), e2b_ref)[7:8, :]
    for g in range(SSD_GROUPS):
        gs = slice(g * GROUP_W, (g + 1) * GROUP_W)
        bt = bm[:, g * SSD_STATE:(g + 1) * SSD_STATE].T.astype(BF16)
        state_ref[:, gs] = state_ref[:, gs] * decay[:, gs] + _dot(bt, xw[:, gs])


def _ssd_bwd(act, dt, hp, e2b, bsz, nc):
    width = SSD_INNER + SSD_BC
    chunk_of = lambda b, c: b * nc + (nc - 1 - c)
    return pl.pallas_call(
        _ssd_bwd_kernel,
        grid=(bsz, nc),
        in_specs=[pl.BlockSpec((CHUNK, width), lambda b, c: (chunk_of(b, c), 0)),
                  pl.BlockSpec((CHUNK, LANES), lambda b, c: (chunk_of(b, c), 0)),
                  _resident((8, LANES)),
                  _resident((2 * LANES, SSD_INNER))],
        out_specs=pl.BlockSpec((1, SSD_STATE, SSD_INNER), lambda b, c: (chunk_of(b, c), 0, 0)),
        out_shape=jax.ShapeDtypeStruct((bsz * nc, SSD_STATE, SSD_INNER), BF16),
        scratch_shapes=[pltpu.VMEM((SSD_STATE, SSD_INNER), F32)],
        compiler_params=_params(2),
        name="ssd_bwd",
    )(act, dt, hp, e2b)


def _ssd_main_kernel(act_ref, dt_ref, z_ref, hb_ref, hp_ref, dskip_ref, ng_ref, e2f_ref, e2b_ref, sel_ref,
                     y_ref, state_ref, ybuf_ref):
    c = pl.program_id(1)

    @pl.when(c == 0)
    def _():
        state_ref[...] = jnp.zeros_like(state_ref)

    xs16 = act_ref[:, :SSD_INNER]
    xs = xs16.astype(F32)
    bm16 = act_ref[:, SSD_INNER:SSD_INNER + SSD_BC]
    cm16 = act_ref[:, SSD_INNER + SSD_BC:]

    dtv, a = _head_scalars(dt_ref, hp_ref)
    incl = _cumsum_rows(a)
    excl = incl - a
    tot = incl[CHUNK - 1:CHUNK, :]
    incl_t = incl.T
    excl_t = excl.T
    dtv_t = dtv.T

    row = lax.broadcasted_iota(jnp.int32, (CHUNK, CHUNK), 0)
    col = lax.broadcasted_iota(jnp.int32, (CHUNK, CHUNK), 1)
    lower = row >= col
    strict_lower = row > col
    strict_upper = row < col

    lane = lax.broadcasted_iota(jnp.int32, incl.shape, 1)
    vmix = jnp.where(lane < SSD_HEADS, incl, excl)
    vhi = vmix.astype(BF16)
    pieces = jnp.concatenate([vhi, (vmix - vhi.astype(F32)).astype(BF16)], axis=1)
    first_head = lax.broadcasted_iota(jnp.int32, (CHUNK, 2 * SSD_HEAD_DIM), 1) < SSD_HEAD_DIM
    gw = SSD_HPG * CHUNK

    groups = range(SSD_GROUPS)
    ns = [slice(g * SSD_STATE, (g + 1) * SSD_STATE) for g in groups]
    gs = [slice(g * GROUP_W, (g + 1) * GROUP_W) for g in groups]
    cbs = [_dot_nt(cm16[:, ns[g]], bm16[:, ns[g]]) for g in groups]
    col_f = [_dot(pieces, sel_ref[:, g * gw:(g + 1) * gw]) for g in groups]
    col_b = [_dot(pieces, sel_ref[:, (SSD_GROUPS + g) * gw:(SSD_GROUPS + g + 1) * gw]) for g in groups]
    scale_f = _expand(jnp.exp2(incl), e2f_ref)
    scale_b = _expand(jnp.exp2(tot - excl), e2b_ref)
    xw = (xs * _expand(jnp.exp2(tot - incl) * dtv, e2f_ref)).astype(BF16)
    decay = _expand(jnp.exp2(incl[CHUNK - 8:, :]), e2f_ref)[7:8, :]
    st = [state_ref[:, gs[g]] for g in groups]
    off_f = [_dot(cm16[:, ns[g]], st[g].astype(BF16)) for g in groups]
    off_b = [_dot(cm16[:, ns[g]], hb_ref[0, :, gs[g]]) for g in groups]
    for g in groups:
        bt = bm16[:, ns[g]].astype(F32).T.astype(BF16)
        state_ref[:, gs[g]] = st[g] * decay[:, gs[g]] + _dot(bt, xw[:, gs[g]])

    for g in groups:
        for pair in range(SSD_HPG // 2):
            ms = []
            for r in (2 * pair, 2 * pair + 1):
                h = g * SSD_HPG + r
                hb_col = SSD_HEADS + h
                rs = slice(r * CHUNK, (r + 1) * CHUNK)
                seg_f = col_f[g][:, rs] - incl_t[h:h + 1, :]
                seg_b = excl_t[hb_col:hb_col + 1, :] - col_b[g][:, rs]
                w = jnp.exp2(jnp.where(lower, seg_f, seg_b))
                dt_f = dtv_t[h:h + 1, :]
                dt_b = dtv_t[hb_col:hb_col + 1, :]
                d = jnp.where(strict_lower, dt_f, jnp.where(strict_upper, dt_b, dt_f + dt_b))
                ms.append((cbs[g] * w * d).astype(BF16))
            ps = slice((g * SSD_HPG + 2 * pair) * SSD_HEAD_DIM, (g * SSD_HPG + 2 * pair + 2) * SSD_HEAD_DIM)
            x2 = xs16[:, ps]
            zero = jnp.zeros_like(x2)
            rhs = jnp.concatenate([jnp.where(first_head, x2, zero), jnp.where(first_head, zero, x2)], axis=0)
            ybuf_ref[:, ps] = _dot(jnp.concatenate(ms, axis=1), rhs)

    zf = z_ref[...].astype(F32)
    zgate = zf * jax.nn.sigmoid(zf)
    for g in groups:
        y = (ybuf_ref[:, gs[g]] + scale_f[:, gs[g]] * off_f[g] + scale_b[:, gs[g]] * off_b[g]
             + dskip_ref[:, gs[g]] * xs[:, gs[g]])
        v = y * zgate[:, gs[g]]
        v = v * lax.rsqrt(jnp.mean(v * v, axis=-1, keepdims=True) + EPS)
        y_ref[:, gs[g]] = (v * ng_ref[:, gs[g]]).astype(BF16)


def _ssd_main(act, dt, z, hb, hp, dskip, norm_g, e2f, e2b, sel, bsz, nc):
    chunk_of = lambda b, c: b * nc + c
    return pl.pallas_call(
        _ssd_main_kernel,
        grid=(bsz, nc),
        in_specs=[pl.BlockSpec((CHUNK, SSD_XBC), lambda b, c: (chunk_of(b, c), 0)),
                  pl.BlockSpec((CHUNK, LANES), lambda b, c: (chunk_of(b, c), 0)),
                  pl.BlockSpec((CHUNK, SSD_INNER), lambda b, c: (chunk_of(b, c), 0)),
                  pl.BlockSpec((1, SSD_STATE, SSD_INNER), lambda b, c: (chunk_of(b, c), 0, 0)),
                  _resident((8, LANES)),
                  _resident((1, SSD_INNER)),
                  _resident((1, SSD_INNER)),
                  _resident((2 * LANES, SSD_INNER)),
                  _resident((2 * LANES, SSD_INNER)),
                  _resident((2 * LANES, 2 * SSD_HEADS * CHUNK))],
        out_specs=pl.BlockSpec((CHUNK, SSD_INNER), lambda b, c: (chunk_of(b, c), 0)),
        out_shape=jax.ShapeDtypeStruct((bsz * nc * CHUNK, SSD_INNER), BF16),
        scratch_shapes=[pltpu.VMEM((SSD_STATE, SSD_INNER), F32),
                        pltpu.VMEM((CHUNK, SSD_INNER), F32)],
        compiler_params=_params(2),
        name="ssd_main",
    )(act, dt, z, hb, hp, dskip, norm_g, e2f, e2b, sel)


def _swa_kernel(q_ref, kvp_ref, kvc_ref, kvn_ref, sink_ref, o_ref):
    n = pl.program_id(1)
    nb = pl.num_programs(1)
    kj = lax.broadcasted_iota(jnp.int32, (3 * ATTN_BLOCK, ATTN_BLOCK), 0)
    qi = lax.broadcasted_iota(jnp.int32, (3 * ATTN_BLOCK, ATTN_BLOCK), 1)
    rel = kj - qi
    valid = (rel >= 0) & (rel <= 2 * ATTN_BLOCK)
    valid = valid & ((kj >= ATTN_BLOCK) | (n > 0)) & ((kj < 2 * ATTN_BLOCK) | (n < nb - 1))
    bias1 = jnp.where(valid, 0.0, NEG_BIG)
    bias = jnp.concatenate([bias1] * ATTN_GQA, axis=1)
    kvcat = jnp.concatenate([kvp_ref[...], kvc_ref[...], kvn_ref[...]], axis=0)
    kcat = kvcat[:, :ATTN_KV_WIDTH]
    vcat_t = kvcat[:, ATTN_KV_WIDTH:].astype(F32).T.astype(BF16)
    outs = []
    scores = []
    for kv in range(ATTN_KV_HEADS):
        ks = slice(kv * ATTN_HEAD_DIM, (kv + 1) * ATTN_HEAD_DIM)
        q_stack = jnp.concatenate(
            [q_ref[:, (kv * ATTN_GQA + r) * ATTN_HEAD_DIM:(kv * ATTN_GQA + r + 1) * ATTN_HEAD_DIM]
             for r in range(ATTN_GQA)], axis=0)
        scores.append(_dot_nt(kcat[:, ks], q_stack) + bias)
    probs = []
    for kv in range(ATTN_KV_HEADS):
        s = scores[kv]
        sink = sink_ref[kv:kv + 1, :]
        m = jnp.maximum(jnp.max(s, axis=0, keepdims=True), sink)
        p = jnp.exp(s - m)
        denom = jnp.sum(p, axis=0, keepdims=True) + jnp.exp(sink - m)
        probs.append((p.astype(BF16), 1.0 / denom))
    for kv in range(ATTN_KV_HEADS):
        ks = slice(kv * ATTN_HEAD_DIM, (kv + 1) * ATTN_HEAD_DIM)
        p16, inv = probs[kv]
        o_t = _dot(vcat_t[ks, :], p16) * inv
        outs.extend(o_t[:, r * ATTN_BLOCK:(r + 1) * ATTN_BLOCK] for r in range(ATTN_GQA))
    o_ref[...] = jnp.concatenate(outs, axis=0).T.astype(BF16)


def _swa(q, kv, sink_rows, bsz, nb):
    blk = lambda b, n: b * nb + n
    cur = lambda b, n: (blk(b, n), 0)
    prev = lambda b, n: (blk(b, jnp.maximum(n - 1, 0)), 0)
    nxt = lambda b, n: (blk(b, jnp.minimum(n + 1, nb - 1)), 0)
    kvspec = lambda im: pl.BlockSpec((ATTN_BLOCK, 2 * ATTN_KV_WIDTH), im)
    return pl.pallas_call(
        _swa_kernel,
        grid=(bsz, nb),
        in_specs=[pl.BlockSpec((ATTN_BLOCK, ATTN_WIDTH), cur),
                  kvspec(prev), kvspec(cur), kvspec(nxt),
                  _resident((8, ATTN_GQA * ATTN_BLOCK))],
        out_specs=pl.BlockSpec((ATTN_BLOCK, ATTN_WIDTH), cur),
        out_shape=jax.ShapeDtypeStruct(q.shape, BF16),
        compiler_params=_params(2),
        name="swa",
    )(q, kv, kv, kv, sink_rows)


def _merge_kernel(x_ref, y_ref, a_ref, gate_ref, ws_ref, wa_ref, wo_ref, o_ref):
    bs = _dot(y_ref[...], ws_ref[...])
    ba = _dot(a_ref[...], wa_ref[...])
    g = jax.nn.sigmoid(gate_ref[...].astype(F32))
    mix = (g[:, :D_MODEL] * bs + g[:, D_MODEL:] * ba).astype(BF16)
    o_ref[...] = x_ref[...] + _dot(mix, wo_ref[...])


def _merge(x2, y, attn, gates, ws, wa, wo):
    t = x2.shape[0]
    tm = 512
    row = lambda i: (i, 0)
    return pl.pallas_call(
        _merge_kernel,
        grid=(t // tm,),
        in_specs=[pl.BlockSpec((tm, D_MODEL), row),
                  pl.BlockSpec((tm, SSD_INNER), row),
                  pl.BlockSpec((tm, ATTN_WIDTH), row),
                  pl.BlockSpec((tm, 2 * D_MODEL), row),
                  _resident((SSD_INNER, D_MODEL)),
                  _resident((ATTN_WIDTH, D_MODEL)),
                  _resident((D_MODEL, D_MODEL))],
        out_specs=pl.BlockSpec((tm, D_MODEL), row),
        out_shape=jax.ShapeDtypeStruct((t, D_MODEL), F32),
        compiler_params=_params(1),
        name="merge",
    )(x2, y, attn, gates, ws, wa, wo)


def _mem_kv_kernel(m_ref, g_ref, w_ref, o_ref):
    h = _rms(m_ref[...], g_ref[...]).astype(BF16)
    o_ref[...] = _dot(h, w_ref[...]).astype(BF16)


def _mem_kv(mem2, g, w_kv):
    t = mem2.shape[0]
    tm = 256
    row = lambda i: (i, 0)
    return pl.pallas_call(
        _mem_kv_kernel,
        grid=(t // tm,),
        in_specs=[pl.BlockSpec((tm, D_MODEL), row),
                  _resident((1, D_MODEL)),
                  _resident((D_MODEL, 2 * D_MODEL))],
        out_specs=pl.BlockSpec((tm, 2 * D_MODEL), row),
        out_shape=jax.ShapeDtypeStruct((t, 2 * D_MODEL), BF16),
        compiler_params=_params(1),
        name="mem_kv",
    )(mem2, g, w_kv)


def _xattn_kernel(x_ref, g_ref, kv_ref, wq_ref, wo_ref, o_ref, ctx_ref):
    x = x_ref[...]
    h = _rms(x, g_ref[...]).astype(BF16)
    q = (_dot(h, wq_ref[...]) * (XATTN_HEAD_DIM ** -0.5)).astype(BF16)
    for hd in range(XATTN_HEADS):
        ks = slice(hd * XATTN_HEAD_DIM, (hd + 1) * XATTN_HEAD_DIM)
        vs = slice(D_MODEL + hd * XATTN_HEAD_DIM, D_MODEL + (hd + 1) * XATTN_HEAD_DIM)
        s = _dot_nt(q[:, ks], kv_ref[:, ks])
        m = jnp.max(s, axis=-1, keepdims=True)
        p = jnp.exp(s - m)
        denom = jnp.sum(p, axis=-1, keepdims=True)
        ctx = _dot(p.astype(BF16), kv_ref[:, vs]) * (1.0 / denom)
        ctx_ref[:, ks] = ctx.astype(BF16)
    o_ref[...] = x + _dot(ctx_ref[...], wo_ref[...])


def _xattn(x2, g, kv, wq, wo, seq, mem_len):
    t = x2.shape[0]
    tm = 512
    per_seq = seq // tm
    row = lambda i: (i, 0)
    return pl.pallas_call(
        _xattn_kernel,
        grid=(t // tm,),
        in_specs=[pl.BlockSpec((tm, D_MODEL), row),
                  _resident((1, D_MODEL)),
                  pl.BlockSpec((mem_len, 2 * D_MODEL), lambda i: (i // per_seq, 0)),
                  _resident((D_MODEL, D_MODEL)),
                  _resident((D_MODEL, D_MODEL))],
        out_specs=pl.BlockSpec((tm, D_MODEL), row),
        out_shape=jax.ShapeDtypeStruct((t, D_MODEL), F32),
        scratch_shapes=[pltpu.VMEM((tm, D_MODEL), BF16)],
        compiler_params=_params(1),
        name="xattn",
    )(x2, g, kv, wq, wo)


FFN_STEP = 1408


def _ffn_kernel(x_ref, g_ref, wi_ref, wd_ref, gf_ref, o_ref):
    x = x_ref[...]
    h = _rms(x, g_ref[...]).astype(BF16)
    acc = x
    for c in range(0, FFN_HIDDEN, FFN_STEP):
        gate = _dot(h, wi_ref[:, c:c + FFN_STEP])
        up = _dot(h, wi_ref[:, FFN_HIDDEN + c:FFN_HIDDEN + c + FFN_STEP])
        act = (gate * jax.nn.sigmoid(gate) * up).astype(BF16)
        acc = acc + _dot(act, wd_ref[c:c + FFN_STEP, :])
    o_ref[...] = _rms(acc, gf_ref[...])


def _ffn(x2, g, wi, wd, gf):
    t = x2.shape[0]
    tm = 512
    row = lambda i: (i, 0)
    return pl.pallas_call(
        _ffn_kernel,
        grid=(t // tm,),
        in_specs=[pl.BlockSpec((tm, D_MODEL), row),
                  _resident((1, D_MODEL)),
                  _resident((D_MODEL, 2 * FFN_HIDDEN)),
                  _resident((FFN_HIDDEN, D_MODEL)),
                  _resident((1, D_MODEL))],
        out_specs=pl.BlockSpec((tm, D_MODEL), row),
        out_shape=jax.ShapeDtypeStruct((t, D_MODEL), F32),
        compiler_params=_params(1),
        name="ffn",
    )(x2, g, wi, wd, gf)


def _pad_rows(a, rows):
    return jnp.pad(a, ((0, rows - a.shape[0]), (0, 0)))


def _pad_cols(a, cols):
    return jnp.pad(a, ((0, 0), (0, cols - a.shape[1])))


def _column_select_matrix():
    j = jnp.arange(LANES)[:, None]
    blk = jnp.arange(2 * SSD_HEADS * CHUNK)[None, :] // CHUNK
    e = (j == blk).astype(BF16)
    return jnp.concatenate([e, e], axis=0)


def _select_matrix(slot):
    j = jnp.arange(LANES)[:, None]
    ch = jnp.arange(SSD_INNER)[None, :]
    e = (j == slot * SSD_HEADS + ch // SSD_HEAD_DIM).astype(BF16)
    return jnp.concatenate([e, e], axis=0)


def _layer(x2, mem2, bsz, seq, mem_len, norm_mix_g, w_in, conv_w, conv_b, dt_bias_fwd, dt_bias_bwd,
           a_log_fwd, a_log_bwd, d_skip, ssd_norm_g, attn_sink, w_branch_ssd, w_branch_attn, w_mix_out,
           norm_xattn_g, norm_mem_g, w_xattn_q, w_xattn_kv, w_xattn_out, norm_ffn_g, w_ffn_in, w_ffn_out,
           final_g):
    nc = seq // CHUNK
    s0 = SSD_INNER
    s1 = s0 + SSD_XBC
    s2 = s1 + 2 * SSD_HEADS
    s3 = s2 + ATTN_WIDTH
    s4 = s3 + ATTN_KV_WIDTH
    s5 = s4 + ATTN_KV_WIDTH
    w_packed = jnp.concatenate(
        [w_in[:, :s1], w_in[:, s2:s3], w_in[:, s3:s4], w_in[:, s4:s5], w_in[:, s5:],
         _pad_cols(w_in[:, s1:s2], LANES)], axis=1).astype(BF16)

    half = ATTN_HEAD_DIM // 2
    inv_freq = ROPE_THETA ** (-jnp.arange(half, dtype=F32) / half)
    ang = jnp.arange(seq, dtype=F32)[:, None] * inv_freq[None]
    cos = jnp.tile(jnp.cos(ang), (1, LANES // half))
    sin_signed = jnp.tile(jnp.concatenate([-jnp.sin(ang), jnp.sin(ang)], axis=1), (1, LANES // ATTN_HEAD_DIM))

    cw = _pad_rows(conv_w, 8)
    z, act, q, kv_attn, gates, dt = _in_proj(x2, norm_mix_g[None], w_packed, cos, sin_signed, cw, conv_b[None], seq)

    hp = _pad_rows(jnp.stack([_pad_cols(jnp.concatenate([dt_bias_fwd, dt_bias_bwd])[None], LANES)[0],
                              _pad_cols(jnp.concatenate([a_log_fwd, a_log_bwd])[None], LANES)[0]]), 8)
    e2f = _select_matrix(0)
    e2b = _select_matrix(1)
    hb = _ssd_bwd(act, dt, hp, e2b, bsz, nc)
    dskip = jnp.repeat(d_skip, SSD_HEAD_DIM)[None]
    y = _ssd_main(act, dt, z, hb, hp, dskip, ssd_norm_g[None], e2f, e2b, _column_select_matrix(), bsz, nc)

    sink_rows = _pad_rows(jnp.repeat(attn_sink, ATTN_BLOCK).reshape(ATTN_KV_HEADS, ATTN_GQA * ATTN_BLOCK), 8)
    attn = _swa(q, kv_attn, sink_rows, bsz, seq // ATTN_BLOCK)

    x2 = _merge(x2, y, attn, gates, w_branch_ssd.astype(BF16), w_branch_attn.astype(BF16),
                w_mix_out.astype(BF16))

    kv = _mem_kv(mem2, norm_mem_g[None], w_xattn_kv.astype(BF16))
    x2 = _xattn(x2, norm_xattn_g[None], kv, w_xattn_q.astype(BF16), w_xattn_out.astype(BF16), seq, mem_len)

    return _ffn(x2, norm_ffn_g[None], w_ffn_in.astype(BF16), w_ffn_out.astype(BF16), final_g)


def kernel(x, mem, norm_mix_g, w_in, conv_w, conv_b, dt_bias_fwd, dt_bias_bwd, a_log_fwd, a_log_bwd, d_skip,
           ssd_norm_g, attn_sink, w_branch_ssd, w_branch_attn, w_mix_out, norm_xattn_g, norm_mem_g, w_xattn_q,
           w_xattn_kv, w_xattn_out, norm_ffn_g, w_ffn_in, w_ffn_out, norm_final_g):
    bsz, seq, _ = x.shape
    mem_len = mem.shape[1]
    assert w_in.shape[0] == 1, "single-layer stack expected"
    layer = 0
    x2 = x.reshape(bsz * seq, D_MODEL)
    mem2 = mem.reshape(bsz * mem_len, D_MODEL)
    out = _layer(x2, mem2, bsz, seq, mem_len, norm_mix_g[layer], w_in[layer], conv_w[layer], conv_b[layer],
                 dt_bias_fwd[layer], dt_bias_bwd[layer], a_log_fwd[layer], a_log_bwd[layer], d_skip[layer],
                 ssd_norm_g[layer], attn_sink[layer], w_branch_ssd[layer], w_branch_attn[layer],
                 w_mix_out[layer], norm_xattn_g[layer], norm_mem_g[layer], w_xattn_q[layer],
                 w_xattn_kv[layer], w_xattn_out[layer], norm_ffn_g[layer], w_ffn_in[layer],
                 w_ffn_out[layer], norm_final_g[None])
    return out.reshape(bsz, seq, D_MODEL)
```

```python
import functools

import jax
import jax.numpy as jnp
from jax import lax
from jax.experimental import pallas as pl
from jax.experimental.pallas import tpu as pltpu

F32 = jnp.float32
BF16 = jnp.bfloat16

D_MODEL = 1024
EPS = 1e-6
SSD_INNER = 2048
SSD_HEAD_DIM = 64
SSD_HEADS = 32
SSD_GROUPS = 4
SSD_HPG = 8
SSD_STATE = 128
SSD_CONV = 5
CHUNK = 128
BWD_CHUNKS = 4
MAIN_CHUNKS = 2
SSD_BC = SSD_GROUPS * SSD_STATE
SSD_XBC = SSD_INNER + 2 * SSD_BC
GROUP_W = SSD_HPG * SSD_HEAD_DIM
ATTN_HEAD_DIM = 64
ATTN_KV_HEADS = 4
ATTN_GQA = 4
ATTN_WIDTH = 1024
ATTN_KV_WIDTH = 256
ATTN_BLOCK = 128
SWA_BLOCKS = 2
ROPE_THETA = 10000.0
XATTN_HEADS = 4
XATTN_HEAD_DIM = 256
FFN_HIDDEN = 2816

LANES = 128
NEG_BIG = -1e30
LOG2_E = 1.4426950408889634

OFF_Z = 0
OFF_XBC = OFF_Z + SSD_INNER
OFF_Q = OFF_XBC + SSD_XBC
OFF_K = OFF_Q + ATTN_WIDTH
OFF_V = OFF_K + ATTN_KV_WIDTH
OFF_G = OFF_V + ATTN_KV_WIDTH
OFF_DT = OFF_G + 2 * D_MODEL
IN_PACKED = OFF_DT + LANES

VMEM_LIMIT = 56 * 1024 * 1024


def _params(n_axes, flags=None):
    return pltpu.CompilerParams(dimension_semantics=("arbitrary",) * n_axes,
                                vmem_limit_bytes=VMEM_LIMIT, flags=flags)


def _resident(shape):
    nd = len(shape)
    return pl.BlockSpec(shape, lambda *_: (0,) * nd, pipeline_mode=pl.Buffered(1))


def _rms(x, g):
    return x * lax.rsqrt(jnp.mean(x * x, axis=-1, keepdims=True) + EPS) * g


def _dot(a, b):
    return jnp.dot(a, b, preferred_element_type=F32)


def _dot_nt(a, b):
    return lax.dot_general(a, b, (((1,), (1,)), ((), ())), preferred_element_type=F32)


def _rotary(u, cos, sin_signed):
    n = u.shape[1]
    lane = lax.broadcasted_iota(jnp.int32, u.shape, 1)
    first_half = (lane % ATTN_HEAD_DIM) < (ATTN_HEAD_DIM // 2)
    partner = jnp.where(first_half, pltpu.roll(u, n - ATTN_HEAD_DIM // 2, axis=1),
                        pltpu.roll(u, ATTN_HEAD_DIM // 2, axis=1))
    reps = n // LANES
    cos_t = jnp.concatenate([cos] * reps, axis=1) if reps > 1 else cos
    sin_t = jnp.concatenate([sin_signed] * reps, axis=1) if reps > 1 else sin_signed
    return u * cos_t + partner * sin_t


def _in_proj_kernel(x_ref, xp_ref, xn_ref, g_ref, w_ref, cos_ref, sin_ref, cw_ref, cb_ref,
                    z_ref, act_ref, q_ref, kv_ref, gate_ref, dt_ref, ext_ref, *, per_seq):
    i = pl.program_id(0)
    tm = x_ref.shape[0]
    g = g_ref[...]
    h_ext = _rms(jnp.concatenate([x_ref[...], xp_ref[...], xn_ref[...]], axis=0), g).astype(BF16)
    step = 256
    rblk = 128
    row_blocks = [slice(r, r + rblk) for r in range(0, tm, rblk)]
    has_prev = (i % per_seq) > 0
    has_next = (i % per_seq) < per_seq - 1

    def proj(rs, off, width):
        return _dot(h_ext[rs], w_ref[:, off:off + width])

    def xbc_proj(c):
        cs = slice(c, c + step)
        for rs in row_blocks[:-1]:
            ext_ref[8 + rs.start:8 + rs.stop, cs] = proj(rs, OFF_XBC + c, step)
        last = row_blocks[-1]
        u = proj(slice(last.start, tm + 16), OFF_XBC + c, step)
        ext_ref[8 + last.start:8 + tm, cs] = u[:rblk]
        ext_ref[0:8, cs] = jnp.where(has_prev, u[rblk:rblk + 8], 0.0)
        ext_ref[8 + tm:16 + tm, cs] = jnp.where(has_next, u[rblk + 8:], 0.0)

    assert SSD_CONV == 5

    def conv_silu(c):
        cs = slice(c, c + step)
        e = ext_ref[:, cs]
        rows = e.shape[0]
        tap = lambda k: cw_ref[k:k + 1, cs] * e
        down = lambda a: pltpu.roll(a, 1, axis=0)
        up = lambda a: pltpu.roll(a, rows - 1, axis=0)
        left = down(down(tap(0)) + tap(1)) + tap(2)
        right = up(up(tap(4)) + tap(3))
        acc = (left + right)[8:8 + tm] + cb_ref[:, cs]
        act_ref[:, cs] = (acc * jax.nn.sigmoid(acc)).astype(BF16)

    def z_proj(c):
        for rs in row_blocks:
            z_ref[rs, c:c + step] = proj(rs, OFF_Z + c, step).astype(BF16)

    def q_proj(c):
        for rs in row_blocks:
            q = _rotary(proj(rs, OFF_Q + c, step), cos_ref[rs, :], sin_ref[rs, :])
            q_ref[rs, c:c + step] = (q * (ATTN_HEAD_DIM ** -0.5)).astype(BF16)

    def kv_proj(_):
        for rs in row_blocks:
            kv = proj(rs, OFF_K, 2 * ATTN_KV_WIDTH)
            kv_ref[rs, :ATTN_KV_WIDTH] = _rotary(kv[:, :ATTN_KV_WIDTH], cos_ref[rs, :], sin_ref[rs, :]).astype(BF16)
            kv_ref[rs, ATTN_KV_WIDTH:] = kv[:, ATTN_KV_WIDTH:].astype(BF16)
            dt_ref[rs, :] = proj(rs, OFF_DT, LANES)

    def gate_proj(c):
        for rs in row_blocks:
            gate_ref[rs, c:c + step] = proj(rs, OFF_G + c, step).astype(BF16)

    mxu_tasks = ([(z_proj, c) for c in range(0, SSD_INNER, step)]
                 + [(q_proj, c) for c in range(0, ATTN_WIDTH, step)] + [(kv_proj, 0)]
                 + [(gate_proj, c) for c in range(0, 2 * D_MODEL, step)])
    conv_chunks = list(range(0, SSD_XBC, step))
    xbc_proj(conv_chunks[0])
    for j, c in enumerate(conv_chunks):
        if j + 1 < len(conv_chunks):
            xbc_proj(conv_chunks[j + 1])
        for fn, arg in mxu_tasks[2 * j:2 * j + 2]:
            fn(arg)
        conv_silu(c)
    for fn, arg in mxu_tasks[2 * len(conv_chunks):]:
        fn(arg)


def _in_proj(x2, g, w_packed, cos, sin_signed, conv_w, conv_b, seq):
    t = x2.shape[0]
    tm = 512
    per_seq = seq // tm
    halo_per_tile = tm // 8
    last_halo = t // 8 - 1
    row = lambda i: (i, 0)
    pos = lambda i: (i % per_seq, 0)
    outs = [(SSD_INNER, BF16), (SSD_XBC, BF16), (ATTN_WIDTH, BF16), (2 * ATTN_KV_WIDTH, BF16),
            (2 * D_MODEL, BF16), (LANES, F32)]
    return pl.pallas_call(
        functools.partial(_in_proj_kernel, per_seq=per_seq),
        grid=(t // tm,),
        in_specs=[pl.BlockSpec((tm, D_MODEL), row),
                  pl.BlockSpec((8, D_MODEL), lambda i: (jnp.maximum(i * halo_per_tile - 1, 0), 0)),
                  pl.BlockSpec((8, D_MODEL), lambda i: (jnp.minimum((i + 1) * halo_per_tile, last_halo), 0)),
                  _resident((1, D_MODEL)),
                  _resident((D_MODEL, IN_PACKED)),
                  pl.BlockSpec((tm, LANES), pos),
                  pl.BlockSpec((tm, LANES), pos),
                  _resident((8, SSD_XBC)),
                  _resident((1, SSD_XBC))],
        out_specs=[pl.BlockSpec((tm, w), row) for w, _ in outs],
        out_shape=[jax.ShapeDtypeStruct((t, w), d) for w, d in outs],
        scratch_shapes=[pltpu.VMEM((tm + 16, SSD_XBC), F32)],
        compiler_params=_params(1),
        name="in_proj",
    )(x2, x2, x2, g, w_packed, cos, sin_signed, conv_w, conv_b)


def _cumsum_rows(a):
    row = lax.broadcasted_iota(jnp.int32, a.shape, 0)
    v = a
    k = 1
    while k < a.shape[0]:
        v = v + jnp.where(row >= k, pltpu.roll(v, k, axis=0), 0.0)
        k *= 2
    return v


def _expand(v, e2_ref):
    hi = v.astype(BF16)
    lo = (v - hi.astype(F32)).astype(BF16)
    return _dot(jnp.concatenate([hi, lo], axis=1), e2_ref[...])


def _softplus(x):
    return jnp.maximum(x, 0.0) + jnp.log1p(jnp.exp(-jnp.abs(x)))


def _head_scalars(dt_raw, hp_ref):
    dtv = _softplus(dt_raw + hp_ref[0:1, :])
    a = dtv * (-LOG2_E * jnp.exp(hp_ref[1:2, :]))
    return dtv, a


def _ssd_bwd_kernel(act_ref, dt_ref, hp_ref, e2b_ref, hb_ref, state_ref):
    c = pl.program_id(1)

    @pl.when(c == 0)
    def _():
        state_ref[...] = jnp.zeros_like(state_ref)

    for j in reversed(range(BWD_CHUNKS)):
        rows = slice(j * CHUNK, (j + 1) * CHUNK)
        hb_ref[j] = state_ref[...].astype(BF16)
        xs = act_ref[rows, :SSD_INNER].astype(F32)
        bm = act_ref[rows, SSD_INNER:].astype(F32)
        dtv, a = _head_scalars(dt_ref[rows, :], hp_ref)
        incl = _cumsum_rows(a)
        excl = incl - a
        xw = (xs * _expand(jnp.exp2(excl) * dtv, e2b_ref)).astype(BF16)
        decay = _expand(jnp.exp2(incl[CHUNK - 8:, :]), e2b_ref)[7:8, :]
        for g in range(SSD_GROUPS):
            gs = slice(g * GROUP_W, (g + 1) * GROUP_W)
            bt = bm[:, g * SSD_STATE:(g + 1) * SSD_STATE].T.astype(BF16)
            state_ref[:, gs] = state_ref[:, gs] * decay[:, gs] + _dot(bt, xw[:, gs])


def _ssd_bwd(act, dt, hp, e2b, bsz, nc):
    width = SSD_INNER + SSD_BC
    steps = nc // BWD_CHUNKS
    block_of = lambda b, c: b * steps + (steps - 1 - c)
    return pl.pallas_call(
        _ssd_bwd_kernel,
        grid=(bsz, steps),
        in_specs=[pl.BlockSpec((BWD_CHUNKS * CHUNK, width), lambda b, c: (block_of(b, c), 0)),
                  pl.BlockSpec((BWD_CHUNKS * CHUNK, LANES), lambda b, c: (block_of(b, c), 0)),
                  _resident((8, LANES)),
                  _resident((2 * LANES, SSD_INNER))],
        out_specs=pl.BlockSpec((BWD_CHUNKS, SSD_STATE, SSD_INNER), lambda b, c: (block_of(b, c), 0, 0)),
        out_shape=jax.ShapeDtypeStruct((bsz * nc, SSD_STATE, SSD_INNER), BF16),
        scratch_shapes=[pltpu.VMEM((SSD_STATE, SSD_INNER), F32)],
        compiler_params=_params(2),
        name="ssd_bwd",
    )(act, dt, hp, e2b)


def _ssd_main_kernel(act_ref, dt_ref, z_ref, hb_ref, hp_ref, dskip_ref, ng_ref, e2f_ref, e2b_ref, sel_ref,
                     y_ref, state_ref, ybuf_ref):
    c = pl.program_id(1)

    @pl.when(c == 0)
    def _():
        state_ref[...] = jnp.zeros_like(state_ref)

    for j in range(MAIN_CHUNKS):
        _ssd_main_chunk(j, act_ref, dt_ref, z_ref, hb_ref, hp_ref, dskip_ref, ng_ref, e2f_ref, e2b_ref, sel_ref,
                        y_ref, state_ref, ybuf_ref)


def _ssd_main_chunk(j, act_ref, dt_ref, z_ref, hb_ref, hp_ref, dskip_ref, ng_ref, e2f_ref, e2b_ref, sel_ref,
                    y_ref, state_ref, ybuf_ref):
    rows = slice(j * CHUNK, (j + 1) * CHUNK)
    ybuf_ref = ybuf_ref.at[j]
    xs16 = act_ref[rows, :SSD_INNER]
    xs = xs16.astype(F32)
    bm16 = act_ref[rows, SSD_INNER:SSD_INNER + SSD_BC]
    cm16 = act_ref[rows, SSD_INNER + SSD_BC:]

    dtv, a = _head_scalars(dt_ref[rows, :], hp_ref)
    incl = _cumsum_rows(a)
    excl = incl - a
    tot = incl[CHUNK - 1:CHUNK, :]
    incl_t = incl.T
    excl_t = excl.T
    dtv_t = dtv.T

    row = lax.broadcasted_iota(jnp.int32, (CHUNK, CHUNK), 0)
    col = lax.broadcasted_iota(jnp.int32, (CHUNK, CHUNK), 1)
    lower = row >= col
    strict_lower = row > col
    strict_upper = row < col

    lane = lax.broadcasted_iota(jnp.int32, incl.shape, 1)
    vmix = jnp.where(lane < SSD_HEADS, incl, excl)
    vhi = vmix.astype(BF16)
    pieces = jnp.concatenate([vhi, (vmix - vhi.astype(F32)).astype(BF16)], axis=1)
    first_head = lax.broadcasted_iota(jnp.int32, (CHUNK, 2 * SSD_HEAD_DIM), 1) < SSD_HEAD_DIM
    gw = SSD_HPG * CHUNK

    groups = range(SSD_GROUPS)
    ns = [slice(g * SSD_STATE, (g + 1) * SSD_STATE) for g in groups]
    gs = [slice(g * GROUP_W, (g + 1) * GROUP_W) for g in groups]
    cbs = [_dot_nt(cm16[:, ns[g]], bm16[:, ns[g]]) for g in groups]
    col_f = [_dot(pieces, sel_ref[:, g * gw:(g + 1) * gw]) for g in groups]
    col_b = [_dot(pieces, sel_ref[:, (SSD_GROUPS + g) * gw:(SSD_GROUPS + g + 1) * gw]) for g in groups]
    scale_f = _expand(jnp.exp2(incl), e2f_ref)
    scale_b = _expand(jnp.exp2(tot - excl), e2b_ref)
    xw = (xs * _expand(jnp.exp2(tot - incl) * dtv, e2f_ref)).astype(BF16)
    decay = _expand(jnp.exp2(incl[CHUNK - 8:, :]), e2f_ref)[7:8, :]
    st = [state_ref[:, gs[g]] for g in groups]
    off_f = [_dot(cm16[:, ns[g]], st[g].astype(BF16)) for g in groups]
    off_b = [_dot(cm16[:, ns[g]], hb_ref[j, :, gs[g]]) for g in groups]
    for g in groups:
        bt = bm16[:, ns[g]].astype(F32).T.astype(BF16)
        state_ref[:, gs[g]] = st[g] * decay[:, gs[g]] + _dot(bt, xw[:, gs[g]])

    for g in groups:
        for pair in range(SSD_HPG // 2):
            ms = []
            for r in (2 * pair, 2 * pair + 1):
                h = g * SSD_HPG + r
                hb_col = SSD_HEADS + h
                rs = slice(r * CHUNK, (r + 1) * CHUNK)
                seg_f = col_f[g][:, rs] - incl_t[h:h + 1, :]
                seg_b = excl_t[hb_col:hb_col + 1, :] - col_b[g][:, rs]
                w = jnp.exp2(jnp.where(lower, seg_f, seg_b))
                dt_f = dtv_t[h:h + 1, :]
                dt_b = dtv_t[hb_col:hb_col + 1, :]
                d = jnp.where(strict_lower, dt_f, jnp.where(strict_upper, dt_b, dt_f + dt_b))
                ms.append((cbs[g] * w * d).astype(BF16))
            ps = slice((g * SSD_HPG + 2 * pair) * SSD_HEAD_DIM, (g * SSD_HPG + 2 * pair + 2) * SSD_HEAD_DIM)
            x2 = xs16[:, ps]
            zero = jnp.zeros_like(x2)
            rhs = jnp.concatenate([jnp.where(first_head, x2, zero), jnp.where(first_head, zero, x2)], axis=0)
            ybuf_ref[:, ps] = _dot(jnp.concatenate(ms, axis=1), rhs)

    zf = z_ref[rows, :].astype(F32)
    zgate = zf * jax.nn.sigmoid(zf)
    for g in groups:
        y = (ybuf_ref[:, gs[g]] + scale_f[:, gs[g]] * off_f[g] + scale_b[:, gs[g]] * off_b[g]
             + dskip_ref[:, gs[g]] * xs[:, gs[g]])
        v = y * zgate[:, gs[g]]
        v = v * lax.rsqrt(jnp.mean(v * v, axis=-1, keepdims=True) + EPS)
        y_ref[rows, gs[g]] = (v * ng_ref[:, gs[g]]).astype(BF16)


def _ssd_main(act, dt, z, hb, hp, dskip, norm_g, e2f, e2b, sel, bsz, nc):
    steps = nc // MAIN_CHUNKS
    rows = MAIN_CHUNKS * CHUNK
    block_of = lambda b, c: b * steps + c
    return pl.pallas_call(
        _ssd_main_kernel,
        grid=(bsz, steps),
        in_specs=[pl.BlockSpec((rows, SSD_XBC), lambda b, c: (block_of(b, c), 0)),
                  pl.BlockSpec((rows, LANES), lambda b, c: (block_of(b, c), 0)),
                  pl.BlockSpec((rows, SSD_INNER), lambda b, c: (block_of(b, c), 0)),
                  pl.BlockSpec((MAIN_CHUNKS, SSD_STATE, SSD_INNER), lambda b, c: (block_of(b, c), 0, 0)),
                  _resident((8, LANES)),
                  _resident((1, SSD_INNER)),
                  _resident((1, SSD_INNER)),
                  _resident((2 * LANES, SSD_INNER)),
                  _resident((2 * LANES, SSD_INNER)),
                  _resident((2 * LANES, 2 * SSD_HEADS * CHUNK))],
        out_specs=pl.BlockSpec((rows, SSD_INNER), lambda b, c: (block_of(b, c), 0)),
        out_shape=jax.ShapeDtypeStruct((bsz * nc * CHUNK, SSD_INNER), BF16),
        scratch_shapes=[pltpu.VMEM((SSD_STATE, SSD_INNER), F32),
                        pltpu.VMEM((MAIN_CHUNKS, CHUNK, SSD_INNER), F32)],
        compiler_params=_params(2),
        name="ssd_main",
    )(act, dt, z, hb, hp, dskip, norm_g, e2f, e2b, sel)


def _swa_kernel(q_ref, kvp_ref, kvc_ref, kvn_ref, sink_ref, o_ref):
    n = pl.program_id(1)
    last = pl.num_programs(1) - 1
    kj = lax.broadcasted_iota(jnp.int32, (3 * ATTN_BLOCK, ATTN_BLOCK), 0)
    qi = lax.broadcasted_iota(jnp.int32, (3 * ATTN_BLOCK, ATTN_BLOCK), 1)
    rel = kj - qi
    band = (rel >= 0) & (rel <= 2 * ATTN_BLOCK)
    kvcat = jnp.concatenate([kvp_ref[...], kvc_ref[...], kvn_ref[...]], axis=0)
    kcat = kvcat[:, :ATTN_KV_WIDTH]
    vcat_t = kvcat[:, ATTN_KV_WIDTH:].astype(F32).T.astype(BF16)
    scores = []
    for j in range(SWA_BLOCKS):
        valid = band
        if j == 0:
            valid = valid & ((kj >= ATTN_BLOCK) | (n > 0))
        if j == SWA_BLOCKS - 1:
            valid = valid & ((kj < 2 * ATTN_BLOCK) | (n < last))
        bias = jnp.concatenate([jnp.where(valid, 0.0, NEG_BIG)] * ATTN_GQA, axis=1)
        keys = slice(j * ATTN_BLOCK, (j + 3) * ATTN_BLOCK)
        qrows = slice(j * ATTN_BLOCK, (j + 1) * ATTN_BLOCK)
        for kv in range(ATTN_KV_HEADS):
            ks = slice(kv * ATTN_HEAD_DIM, (kv + 1) * ATTN_HEAD_DIM)
            q_stack = jnp.concatenate(
                [q_ref[qrows, (kv * ATTN_GQA + r) * ATTN_HEAD_DIM:(kv * ATTN_GQA + r + 1) * ATTN_HEAD_DIM]
                 for r in range(ATTN_GQA)], axis=0)
            scores.append(_dot_nt(kcat[keys, ks], q_stack) + bias)
    probs = []
    for idx, s in enumerate(scores):
        kv = idx % ATTN_KV_HEADS
        sink = sink_ref[kv:kv + 1, :]
        m = jnp.maximum(jnp.max(s, axis=0, keepdims=True), sink)
        p = jnp.exp(s - m)
        denom = jnp.sum(p, axis=0, keepdims=True) + jnp.exp(sink - m)
        probs.append((p.astype(BF16), 1.0 / denom))
    for j in range(SWA_BLOCKS):
        keys = slice(j * ATTN_BLOCK, (j + 3) * ATTN_BLOCK)
        outs = []
        for kv in range(ATTN_KV_HEADS):
            ks = slice(kv * ATTN_HEAD_DIM, (kv + 1) * ATTN_HEAD_DIM)
            p16, inv = probs[j * ATTN_KV_HEADS + kv]
            o_t = _dot(vcat_t[ks, keys], p16) * inv
            outs.extend(o_t[:, r * ATTN_BLOCK:(r + 1) * ATTN_BLOCK] for r in range(ATTN_GQA))
        o_ref[j * ATTN_BLOCK:(j + 1) * ATTN_BLOCK, :] = jnp.concatenate(outs, axis=0).T.astype(BF16)


def _swa(q, kv, sink_rows, bsz, nb):
    steps = nb // SWA_BLOCKS
    rows = SWA_BLOCKS * ATTN_BLOCK
    cur = lambda b, n: (b * steps + n, 0)
    prev = lambda b, n: (b * nb + jnp.maximum(SWA_BLOCKS * n - 1, 0), 0)
    nxt = lambda b, n: (b * nb + jnp.minimum(SWA_BLOCKS * (n + 1), nb - 1), 0)
    return pl.pallas_call(
        _swa_kernel,
        grid=(bsz, steps),
        in_specs=[pl.BlockSpec((rows, ATTN_WIDTH), cur),
                  pl.BlockSpec((ATTN_BLOCK, 2 * ATTN_KV_WIDTH), prev),
                  pl.BlockSpec((rows, 2 * ATTN_KV_WIDTH), cur),
                  pl.BlockSpec((ATTN_BLOCK, 2 * ATTN_KV_WIDTH), nxt),
                  _resident((8, ATTN_GQA * ATTN_BLOCK))],
        out_specs=pl.BlockSpec((rows, ATTN_WIDTH), cur),
        out_shape=jax.ShapeDtypeStruct(q.shape, BF16),
        compiler_params=_params(2),
        name="swa",
    )(q, kv, kv, kv, sink_rows)


def _merge_kernel(x_ref, y_ref, a_ref, gate_ref, ws_ref, wa_ref, wo_ref, o_ref):
    bs = _dot(y_ref[...], ws_ref[...])
    ba = _dot(a_ref[...], wa_ref[...])
    g = jax.nn.sigmoid(gate_ref[...].astype(F32))
    mix = (g[:, :D_MODEL] * bs + g[:, D_MODEL:] * ba).astype(BF16)
    o_ref[...] = x_ref[...] + _dot(mix, wo_ref[...])


def _merge(x2, y, attn, gates, ws, wa, wo):
    t = x2.shape[0]
    tm = 512
    row = lambda i: (i, 0)
    return pl.pallas_call(
        _merge_kernel,
        grid=(t // tm,),
        in_specs=[pl.BlockSpec((tm, D_MODEL), row),
                  pl.BlockSpec((tm, SSD_INNER), row),
                  pl.BlockSpec((tm, ATTN_WIDTH), row),
                  pl.BlockSpec((tm, 2 * D_MODEL), row),
                  _resident((SSD_INNER, D_MODEL)),
                  _resident((ATTN_WIDTH, D_MODEL)),
                  _resident((D_MODEL, D_MODEL))],
        out_specs=pl.BlockSpec((tm, D_MODEL), row),
        out_shape=jax.ShapeDtypeStruct((t, D_MODEL), F32),
        compiler_params=_params(1),
        name="merge",
    )(x2, y, attn, gates, ws, wa, wo)


def _mem_kv_kernel(m_ref, g_ref, w_ref, o_ref):
    h = _rms(m_ref[...], g_ref[...]).astype(BF16)
    o_ref[...] = _dot(h, w_ref[...]).astype(BF16)


def _mem_kv(mem2, g, w_kv):
    t = mem2.shape[0]
    tm = 256
    row = lambda i: (i, 0)
    return pl.pallas_call(
        _mem_kv_kernel,
        grid=(t // tm,),
        in_specs=[pl.BlockSpec((tm, D_MODEL), row),
                  _resident((1, D_MODEL)),
                  _resident((D_MODEL, 2 * D_MODEL))],
        out_specs=pl.BlockSpec((tm, 2 * D_MODEL), row),
        out_shape=jax.ShapeDtypeStruct((t, 2 * D_MODEL), BF16),
        compiler_params=_params(1),
        name="mem_kv",
    )(mem2, g, w_kv)


def _xattn_kernel(x_ref, g_ref, kv_ref, wq_ref, wo_ref, o_ref, ctx_ref):
    x = x_ref[...]
    h = _rms(x, g_ref[...]).astype(BF16)
    q = (_dot(h, wq_ref[...]) * (XATTN_HEAD_DIM ** -0.5)).astype(BF16)
    for hd in range(XATTN_HEADS):
        ks = slice(hd * XATTN_HEAD_DIM, (hd + 1) * XATTN_HEAD_DIM)
        vs = slice(D_MODEL + hd * XATTN_HEAD_DIM, D_MODEL + (hd + 1) * XATTN_HEAD_DIM)
        s = _dot_nt(q[:, ks], kv_ref[:, ks])
        m = jnp.max(s, axis=-1, keepdims=True)
        p = jnp.exp(s - m)
        denom = jnp.sum(p, axis=-1, keepdims=True)
        ctx = _dot(p.astype(BF16), kv_ref[:, vs]) * (1.0 / denom)
        ctx_ref[:, ks] = ctx.astype(BF16)
    o_ref[...] = x + _dot(ctx_ref[...], wo_ref[...])


def _xattn(x2, g, kv, wq, wo, seq, mem_len):
    t = x2.shape[0]
    tm = 512
    per_seq = seq // tm
    row = lambda i: (i, 0)
    return pl.pallas_call(
        _xattn_kernel,
        grid=(t // tm,),
        in_specs=[pl.BlockSpec((tm, D_MODEL), row),
                  _resident((1, D_MODEL)),
                  pl.BlockSpec((mem_len, 2 * D_MODEL), lambda i: (i // per_seq, 0)),
                  _resident((D_MODEL, D_MODEL)),
                  _resident((D_MODEL, D_MODEL))],
        out_specs=pl.BlockSpec((tm, D_MODEL), row),
        out_shape=jax.ShapeDtypeStruct((t, D_MODEL), F32),
        scratch_shapes=[pltpu.VMEM((tm, D_MODEL), BF16)],
        compiler_params=_params(1),
        name="xattn",
    )(x2, g, kv, wq, wo)


MXU_TILE = 256
FFN_SPLIT = (FFN_HIDDEN // MXU_TILE + 1) // 2 * MXU_TILE


def _ffn_kernel(x_ref, g_ref, wi_ref, wd_ref, gf_ref, o_ref):
    x = x_ref[...]
    h = _rms(x, g_ref[...]).astype(BF16)
    acc = x
    for lo, hi in ((0, FFN_SPLIT), (FFN_SPLIT, FFN_HIDDEN)):
        gate = _dot(h, wi_ref[:, lo:hi])
        up = _dot(h, wi_ref[:, FFN_HIDDEN + lo:FFN_HIDDEN + hi])
        act = (gate * jax.nn.sigmoid(gate) * up).astype(BF16)
        acc = acc + _dot(act, wd_ref[lo:hi, :])
    o_ref[...] = _rms(acc, gf_ref[...])


def _ffn(x2, g, wi, wd, gf):
    t = x2.shape[0]
    tm = 512
    row = lambda i: (i, 0)
    return pl.pallas_call(
        _ffn_kernel,
        grid=(t // tm,),
        in_specs=[pl.BlockSpec((tm, D_MODEL), row),
                  _resident((1, D_MODEL)),
                  _resident((D_MODEL, 2 * FFN_HIDDEN)),
                  _resident((FFN_HIDDEN, D_MODEL)),
                  _resident((1, D_MODEL))],
        out_specs=pl.BlockSpec((tm, D_MODEL), row),
        out_shape=jax.ShapeDtypeStruct((t, D_MODEL), F32),
        compiler_params=_params(1),
        name="ffn",
    )(x2, g, wi, wd, gf)


def _pad_rows(a, rows):
    return jnp.pad(a, ((0, rows - a.shape[0]), (0, 0)))


def _pad_cols(a, cols):
    return jnp.pad(a, ((0, 0), (0, cols - a.shape[1])))


def _column_select_matrix():
    j = jnp.arange(LANES)[:, None]
    blk = jnp.arange(2 * SSD_HEADS * CHUNK)[None, :] // CHUNK
    e = (j == blk).astype(BF16)
    return jnp.concatenate([e, e], axis=0)


def _select_matrix(slot):
    j = jnp.arange(LANES)[:, None]
    ch = jnp.arange(SSD_INNER)[None, :]
    e = (j == slot * SSD_HEADS + ch // SSD_HEAD_DIM).astype(BF16)
    return jnp.concatenate([e, e], axis=0)


def _layer(x2, mem2, bsz, seq, mem_len, norm_mix_g, w_in, conv_w, conv_b, dt_bias_fwd, dt_bias_bwd,
           a_log_fwd, a_log_bwd, d_skip, ssd_norm_g, attn_sink, w_branch_ssd, w_branch_attn, w_mix_out,
           norm_xattn_g, norm_mem_g, w_xattn_q, w_xattn_kv, w_xattn_out, norm_ffn_g, w_ffn_in, w_ffn_out,
           final_g):
    nc = seq // CHUNK
    s0 = SSD_INNER
    s1 = s0 + SSD_XBC
    s2 = s1 + 2 * SSD_HEADS
    s3 = s2 + ATTN_WIDTH
    s4 = s3 + ATTN_KV_WIDTH
    s5 = s4 + ATTN_KV_WIDTH
    w_packed = jnp.concatenate(
        [w_in[:, :s1], w_in[:, s2:s3], w_in[:, s3:s4], w_in[:, s4:s5], w_in[:, s5:],
         _pad_cols(w_in[:, s1:s2], LANES)], axis=1).astype(BF16)

    half = ATTN_HEAD_DIM // 2
    inv_freq = ROPE_THETA ** (-jnp.arange(half, dtype=F32) / half)
    ang = jnp.arange(seq, dtype=F32)[:, None] * inv_freq[None]
    cos = jnp.tile(jnp.cos(ang), (1, LANES // half))
    sin_signed = jnp.tile(jnp.concatenate([-jnp.sin(ang), jnp.sin(ang)], axis=1), (1, LANES // ATTN_HEAD_DIM))

    cw = _pad_rows(conv_w, 8)
    z, act, q, kv_attn, gates, dt = _in_proj(x2, norm_mix_g[None], w_packed, cos, sin_signed, cw, conv_b[None], seq)

    hp = _pad_rows(jnp.stack([_pad_cols(jnp.concatenate([dt_bias_fwd, dt_bias_bwd])[None], LANES)[0],
                              _pad_cols(jnp.concatenate([a_log_fwd, a_log_bwd])[None], LANES)[0]]), 8)
    e2f = _select_matrix(0)
    e2b = _select_matrix(1)
    hb = _ssd_bwd(act, dt, hp, e2b, bsz, nc)
    dskip = jnp.repeat(d_skip, SSD_HEAD_DIM)[None]
    y = _ssd_main(act, dt, z, hb, hp, dskip, ssd_norm_g[None], e2f, e2b, _column_select_matrix(), bsz, nc)

    sink_rows = _pad_rows(jnp.repeat(attn_sink, ATTN_BLOCK).reshape(ATTN_KV_HEADS, ATTN_GQA * ATTN_BLOCK), 8)
    attn = _swa(q, kv_attn, sink_rows, bsz, seq // ATTN_BLOCK)

    x2 = _merge(x2, y, attn, gates, w_branch_ssd.astype(BF16), w_branch_attn.astype(BF16),
                w_mix_out.astype(BF16))

    kv = _mem_kv(mem2, norm_mem_g[None], w_xattn_kv.astype(BF16))
    x2 = _xattn(x2, norm_xattn_g[None], kv, w_xattn_q.astype(BF16), w_xattn_out.astype(BF16), seq, mem_len)

    return _ffn(x2, norm_ffn_g[None], w_ffn_in.astype(BF16), w_ffn_out.astype(BF16), final_g)


def kernel(x, mem, norm_mix_g, w_in, conv_w, conv_b, dt_bias_fwd, dt_bias_bwd, a_log_fwd, a_log_bwd, d_skip,
           ssd_norm_g, attn_sink, w_branch_ssd, w_branch_attn, w_mix_out, norm_xattn_g, norm_mem_g, w_xattn_q,
           w_xattn_kv, w_xattn_out, norm_ffn_g, w_ffn_in, w_ffn_out, norm_final_g):
    bsz, seq, _ = x.shape
    mem_len = mem.shape[1]
    assert w_in.shape[0] == 1, "single-layer stack expected"
    layer = 0
    x2 = x.reshape(bsz * seq, D_MODEL)
    mem2 = mem.reshape(bsz * mem_len, D_MODEL)
    out = _layer(x2, mem2, bsz, seq, mem_len, norm_mix_g[layer], w_in[layer], conv_w[layer], conv_b[layer],
                 dt_bias_fwd[layer], dt_bias_bwd[layer], a_log_fwd[layer], a_log_bwd[layer], d_skip[layer],
                 ssd_norm_g[layer], attn_sink[layer], w_branch_ssd[layer], w_branch_attn[layer],
                 w_mix_out[layer], norm_xattn_g[layer], norm_mem_g[layer], w_xattn_q[layer],
                 w_xattn_kv[layer], w_xattn_out[layer], norm_ffn_g[layer], w_ffn_in[layer],
                 w_ffn_out[layer], norm_final_g[None])
    return out.reshape(bsz, seq, D_MODEL)
```

```python
import functools

import jax
import jax.numpy as jnp
from jax import lax
from jax.experimental import pallas as pl
from jax.experimental.pallas import tpu as pltpu

F32 = jnp.float32
BF16 = jnp.bfloat16

D_MODEL = 1024
EPS = 1e-6
SSD_INNER = 2048
SSD_HEAD_DIM = 64
SSD_HEADS = 32
SSD_GROUPS = 4
SSD_HPG = 8
SSD_STATE = 128
SSD_CONV = 5
CHUNK = 128
BWD_CHUNKS = 4
MAIN_CHUNKS = 2
DECAY_ROWS = 16
WIDE_ROWS = 4 * CHUNK + DECAY_ROWS
SSD_BC = SSD_GROUPS * SSD_STATE
SSD_XBC = SSD_INNER + 2 * SSD_BC
GROUP_W = SSD_HPG * SSD_HEAD_DIM
ATTN_HEAD_DIM = 64
ATTN_KV_HEADS = 4
ATTN_GQA = 4
ATTN_WIDTH = 1024
ATTN_KV_WIDTH = 256
ATTN_BLOCK = 128
SWA_BLOCKS = 2
ROPE_THETA = 10000.0
XATTN_HEADS = 4
XATTN_HEAD_DIM = 256
FFN_HIDDEN = 2816

LANES = 128
NEG_BIG = -1e30
LOG2_E = 1.4426950408889634

OFF_Z = 0
OFF_XBC = OFF_Z + SSD_INNER
OFF_Q = OFF_XBC + SSD_XBC
OFF_K = OFF_Q + ATTN_WIDTH
OFF_V = OFF_K + ATTN_KV_WIDTH
OFF_G = OFF_V + ATTN_KV_WIDTH
OFF_DT = OFF_G + 2 * D_MODEL
IN_PACKED = OFF_DT + LANES

VMEM_LIMIT = 56 * 1024 * 1024


def _params(n_axes, flags=None):
    return pltpu.CompilerParams(dimension_semantics=("arbitrary",) * n_axes,
                                vmem_limit_bytes=VMEM_LIMIT, flags=flags)


def _resident(shape):
    nd = len(shape)
    return pl.BlockSpec(shape, lambda *_: (0,) * nd, pipeline_mode=pl.Buffered(1))


def _rms(x, g):
    return x * lax.rsqrt(jnp.mean(x * x, axis=-1, keepdims=True) + EPS) * g


def _dot(a, b):
    return jnp.dot(a, b, preferred_element_type=F32)


def _dot_nt(a, b):
    return lax.dot_general(a, b, (((1,), (1,)), ((), ())), preferred_element_type=F32)


def _rotary(u, cos, sin_signed):
    n = u.shape[1]
    lane = lax.broadcasted_iota(jnp.int32, u.shape, 1)
    first_half = (lane % ATTN_HEAD_DIM) < (ATTN_HEAD_DIM // 2)
    partner = jnp.where(first_half, pltpu.roll(u, n - ATTN_HEAD_DIM // 2, axis=1),
                        pltpu.roll(u, ATTN_HEAD_DIM // 2, axis=1))
    reps = n // LANES
    cos_t = jnp.concatenate([cos] * reps, axis=1) if reps > 1 else cos
    sin_t = jnp.concatenate([sin_signed] * reps, axis=1) if reps > 1 else sin_signed
    return u * cos_t + partner * sin_t


def _in_proj_kernel(x_ref, xp_ref, xn_ref, g_ref, w_ref, cos_ref, sin_ref, cw_ref, cb_ref,
                    z_ref, act_ref, q_ref, kv_ref, gate_ref, dt_ref, ext_ref, *, per_seq):
    i = pl.program_id(0)
    tm = x_ref.shape[0]
    g = g_ref[...]
    h_ext = _rms(jnp.concatenate([x_ref[...], xp_ref[...], xn_ref[...]], axis=0), g).astype(BF16)
    step = 256
    rblk = 128
    row_blocks = [slice(r, r + rblk) for r in range(0, tm, rblk)]
    has_prev = (i % per_seq) > 0
    has_next = (i % per_seq) < per_seq - 1

    def proj(rs, off, width):
        return _dot(h_ext[rs], w_ref[:, off:off + width])

    def xbc_proj(c):
        cs = slice(c, c + step)
        for rs in row_blocks[:-1]:
            ext_ref[8 + rs.start:8 + rs.stop, cs] = proj(rs, OFF_XBC + c, step)
        last = row_blocks[-1]
        u = proj(slice(last.start, tm + 16), OFF_XBC + c, step)
        ext_ref[8 + last.start:8 + tm, cs] = u[:rblk]
        ext_ref[0:8, cs] = jnp.where(has_prev, u[rblk:rblk + 8], 0.0)
        ext_ref[8 + tm:16 + tm, cs] = jnp.where(has_next, u[rblk + 8:], 0.0)

    assert SSD_CONV == 5

    def conv_silu(c):
        cs = slice(c, c + step)
        e = ext_ref[:, cs]
        rows = e.shape[0]
        tap = lambda k: cw_ref[k:k + 1, cs] * e
        down = lambda a: pltpu.roll(a, 1, axis=0)
        up = lambda a: pltpu.roll(a, rows - 1, axis=0)
        left = down(down(tap(0)) + tap(1)) + tap(2)
        right = up(up(tap(4)) + tap(3))
        acc = (left + right)[8:8 + tm] + cb_ref[:, cs]
        act_ref[:, cs] = (acc * jax.nn.sigmoid(acc)).astype(BF16)

    def z_proj(c):
        for rs in row_blocks:
            z_ref[rs, c:c + step] = proj(rs, OFF_Z + c, step).astype(BF16)

    def q_proj(c):
        for rs in row_blocks:
            q = _rotary(proj(rs, OFF_Q + c, step), cos_ref[rs, :], sin_ref[rs, :])
            q_ref[rs, c:c + step] = (q * (ATTN_HEAD_DIM ** -0.5 * LOG2_E)).astype(BF16)

    def kv_proj(_):
        for rs in row_blocks:
            kv = proj(rs, OFF_K, 2 * ATTN_KV_WIDTH)
            kv_ref[rs, :ATTN_KV_WIDTH] = _rotary(kv[:, :ATTN_KV_WIDTH], cos_ref[rs, :], sin_ref[rs, :]).astype(BF16)
            kv_ref[rs, ATTN_KV_WIDTH:] = kv[:, ATTN_KV_WIDTH:].astype(BF16)
            dt_ref[rs, :] = proj(rs, OFF_DT, LANES)

    def gate_proj(c):
        for rs in row_blocks:
            gate_ref[rs, c:c + step] = proj(rs, OFF_G + c, step).astype(BF16)

    mxu_tasks = ([(z_proj, c) for c in range(0, SSD_INNER, step)]
                 + [(q_proj, c) for c in range(0, ATTN_WIDTH, step)] + [(kv_proj, 0)]
                 + [(gate_proj, c) for c in range(0, 2 * D_MODEL, step)])
    conv_chunks = list(range(0, SSD_XBC, step))
    xbc_proj(conv_chunks[0])
    for j, c in enumerate(conv_chunks):
        if j + 1 < len(conv_chunks):
            xbc_proj(conv_chunks[j + 1])
        for fn, arg in mxu_tasks[2 * j:2 * j + 2]:
            fn(arg)
        conv_silu(c)
    for fn, arg in mxu_tasks[2 * len(conv_chunks):]:
        fn(arg)


def _in_proj(x2, g, w_packed, cos, sin_signed, conv_w, conv_b, seq):
    t = x2.shape[0]
    tm = 512
    per_seq = seq // tm
    halo_per_tile = tm // 8
    last_halo = t // 8 - 1
    row = lambda i: (i, 0)
    pos = lambda i: (i % per_seq, 0)
    outs = [(SSD_INNER, BF16), (SSD_XBC, BF16), (ATTN_WIDTH, BF16), (2 * ATTN_KV_WIDTH, BF16),
            (2 * D_MODEL, BF16), (LANES, F32)]
    return pl.pallas_call(
        functools.partial(_in_proj_kernel, per_seq=per_seq),
        grid=(t // tm,),
        in_specs=[pl.BlockSpec((tm, D_MODEL), row),
                  pl.BlockSpec((8, D_MODEL), lambda i: (jnp.maximum(i * halo_per_tile - 1, 0), 0)),
                  pl.BlockSpec((8, D_MODEL), lambda i: (jnp.minimum((i + 1) * halo_per_tile, last_halo), 0)),
                  _resident((1, D_MODEL)),
                  _resident((D_MODEL, IN_PACKED)),
                  pl.BlockSpec((tm, LANES), pos),
                  pl.BlockSpec((tm, LANES), pos),
                  _resident((8, SSD_XBC)),
                  _resident((1, SSD_XBC))],
        out_specs=[pl.BlockSpec((tm, w), row) for w, _ in outs],
        out_shape=[jax.ShapeDtypeStruct((t, w), d) for w, d in outs],
        scratch_shapes=[pltpu.VMEM((tm + 16, SSD_XBC), F32)],
        compiler_params=_params(1),
        name="in_proj",
    )(x2, x2, x2, g, w_packed, cos, sin_signed, conv_w, conv_b)


def _cumsum_rows(a):
    row = lax.broadcasted_iota(jnp.int32, a.shape, 0)
    v = a
    k = 1
    while k < a.shape[0]:
        v = v + jnp.where(row >= k, pltpu.roll(v, k, axis=0), 0.0)
        k *= 2
    return v


def _expand(v, e2_ref):
    hi = v.astype(BF16)
    lo = (v - hi.astype(F32)).astype(BF16)
    return _dot(jnp.concatenate([hi, lo], axis=1), e2_ref[...])


def _softplus(x):
    return jnp.maximum(x, 0.0) + jnp.log1p(jnp.exp(-jnp.abs(x)))


def _head_scalars(dt_raw, hp_ref):
    dtv = _softplus(dt_raw + hp_ref[0:1, :])
    a = dtv * (-LOG2_E * jnp.exp(hp_ref[1:2, :]))
    return dtv, a


def _chunk_scalars(dt_raw, hp_ref):
    dtv, a = _head_scalars(dt_raw, hp_ref)
    incl = _cumsum_rows(a)
    return dtv, incl, incl - a


def _ssd_bwd_kernel(act_ref, dt_ref, hp_ref, e2b_ref, hb_ref, state_ref):
    c = pl.program_id(1)

    @pl.when(c == 0)
    def _():
        state_ref[...] = jnp.zeros_like(state_ref)

    scales, decays = [], []
    for j in range(BWD_CHUNKS):
        dtv, incl, excl = _chunk_scalars(dt_ref[j * CHUNK:(j + 1) * CHUNK, :], hp_ref)
        scales.append(jnp.exp2(excl) * dtv)
        decays.append(jnp.exp2(incl[CHUNK - DECAY_ROWS:, :]))
    wide = _expand(jnp.concatenate(scales + decays, axis=0), e2b_ref)

    for j in reversed(range(BWD_CHUNKS)):
        rows = slice(j * CHUNK, (j + 1) * CHUNK)
        hb_ref[j] = state_ref[...].astype(BF16)
        xs = act_ref[rows, :SSD_INNER].astype(F32)
        bm = act_ref[rows, SSD_INNER:].astype(F32)
        xw = (xs * wide[rows]).astype(BF16)
        last = BWD_CHUNKS * CHUNK + (j + 1) * DECAY_ROWS - 1
        decay = wide[last:last + 1, :]
        for g in range(SSD_GROUPS):
            gs = slice(g * GROUP_W, (g + 1) * GROUP_W)
            bt = bm[:, g * SSD_STATE:(g + 1) * SSD_STATE].T.astype(BF16)
            state_ref[:, gs] = state_ref[:, gs] * decay[:, gs] + _dot(bt, xw[:, gs])


def _ssd_bwd(act, dt, hp, e2b, bsz, nc):
    width = SSD_INNER + SSD_BC
    steps = nc // BWD_CHUNKS
    block_of = lambda b, c: b * steps + (steps - 1 - c)
    return pl.pallas_call(
        _ssd_bwd_kernel,
        grid=(bsz, steps),
        in_specs=[pl.BlockSpec((BWD_CHUNKS * CHUNK, width), lambda b, c: (block_of(b, c), 0)),
                  pl.BlockSpec((BWD_CHUNKS * CHUNK, LANES), lambda b, c: (block_of(b, c), 0)),
                  _resident((8, LANES)),
                  _resident((2 * LANES, SSD_INNER))],
        out_specs=pl.BlockSpec((BWD_CHUNKS, SSD_STATE, SSD_INNER), lambda b, c: (block_of(b, c), 0, 0)),
        out_shape=jax.ShapeDtypeStruct((bsz * nc, SSD_STATE, SSD_INNER), BF16),
        scratch_shapes=[pltpu.VMEM((SSD_STATE, SSD_INNER), F32)],
        compiler_params=_params(2),
        name="ssd_bwd",
    )(act, dt, hp, e2b)


def _ssd_main_kernel(act_ref, dt_ref, z_ref, hb_ref, hp_ref, dskip_ref, ng_ref, e2f_ref, sel_ref,
                     y_ref, state_ref, ybuf_ref):
    c = pl.program_id(1)

    @pl.when(c == 0)
    def _():
        state_ref[...] = jnp.zeros_like(state_ref)

    lane = lax.broadcasted_iota(jnp.int32, (CHUNK, LANES), 1)
    row = lax.broadcasted_iota(jnp.int32, (CHUNK, CHUNK), 0)
    col = lax.broadcasted_iota(jnp.int32, (CHUNK, CHUNK), 1)
    groups = range(SSD_GROUPS)
    to_fwd_slot = lambda v: pltpu.roll(v, LANES - SSD_HEADS, axis=1)
    prep, wide_in, col_in = [], [], []
    for j in range(MAIN_CHUNKS):
        rows = slice(j * CHUNK, (j + 1) * CHUNK)
        dtv, incl, excl = _chunk_scalars(dt_ref[rows, :], hp_ref)
        tot = incl[CHUNK - 1:CHUNK, :]
        cbs = [_dot_nt(act_ref[rows, SSD_INNER + SSD_BC + g * SSD_STATE:SSD_INNER + SSD_BC + (g + 1) * SSD_STATE],
                       act_ref[rows, SSD_INNER + g * SSD_STATE:SSD_INNER + (g + 1) * SSD_STATE]) for g in groups]
        lane_group = (lane % SSD_HEADS) // SSD_HPG
        cb_diag = jnp.zeros_like(dtv)
        for g in groups:
            dg = jnp.sum(jnp.where(row == col, cbs[g], 0.0), axis=1, keepdims=True)
            cb_diag = jnp.where(lane_group == g, dg, cb_diag)
        wide_in += [jnp.exp2(incl),
                    jnp.exp2(tot - incl) * dtv,
                    to_fwd_slot(jnp.exp2(tot - excl)),
                    to_fwd_slot(cb_diag * dtv),
                    jnp.exp2(incl[CHUNK - DECAY_ROWS:, :])]
        log_dt = jnp.log2(dtv)
        key_f = (incl - log_dt).T
        key_b = (excl + log_dt).T
        vmix = jnp.where(lane < SSD_HEADS, incl, excl)
        hi = vmix.astype(BF16).astype(F32)
        lo = vmix - hi
        for shift in [d * SSD_HEADS + g * SSD_HPG for d in range(2) for g in groups]:
            sh = (LANES - shift) % LANES
            col_in.append(jnp.concatenate([pltpu.roll(hi, sh, axis=1) if sh else hi,
                                           pltpu.roll(lo, sh, axis=1) if sh else lo], axis=1).astype(BF16))
        prep.append((cbs, key_f, key_b))
    wide = _expand(jnp.concatenate(wide_in, axis=0), e2f_ref)
    cols = _dot(jnp.concatenate(col_in, axis=0), sel_ref[...])

    for j in range(MAIN_CHUNKS):
        w0 = j * WIDE_ROWS
        c0 = j * 2 * SSD_GROUPS * CHUNK
        _ssd_main_chunk(j, prep[j], wide[w0:w0 + WIDE_ROWS], cols[c0:c0 + 2 * SSD_GROUPS * CHUNK],
                        act_ref, z_ref, hb_ref, dskip_ref, ng_ref, y_ref, state_ref, ybuf_ref.at[j])


def _ssd_main_chunk(j, prep, wide, cols, act_ref, z_ref, hb_ref, dskip_ref, ng_ref, y_ref, state_ref, ybuf_ref):
    rows = slice(j * CHUNK, (j + 1) * CHUNK)
    cbs, key_f, key_b = prep
    scale_f = wide[0:CHUNK]
    state_scale = wide[CHUNK:2 * CHUNK]
    scale_b = wide[2 * CHUNK:3 * CHUNK]
    self_b = wide[3 * CHUNK:4 * CHUNK]
    decay = wide[WIDE_ROWS - 1:WIDE_ROWS, :]
    xs16 = act_ref[rows, :SSD_INNER]
    xs = xs16.astype(F32)
    bm16 = act_ref[rows, SSD_INNER:SSD_INNER + SSD_BC]
    cm16 = act_ref[rows, SSD_INNER + SSD_BC:]

    row = lax.broadcasted_iota(jnp.int32, (CHUNK, CHUNK), 0)
    col = lax.broadcasted_iota(jnp.int32, (CHUNK, CHUNK), 1)
    lower = row >= col
    first_head = lax.broadcasted_iota(jnp.int32, (CHUNK, 2 * SSD_HEAD_DIM), 1) < SSD_HEAD_DIM

    groups = range(SSD_GROUPS)
    ns = [slice(g * SSD_STATE, (g + 1) * SSD_STATE) for g in groups]
    gs = [slice(g * GROUP_W, (g + 1) * GROUP_W) for g in groups]
    xw = (xs * state_scale).astype(BF16)
    st = [state_ref[:, gs[g]] for g in groups]
    off_f = [_dot(cm16[:, ns[g]], st[g].astype(BF16)) for g in groups]
    off_b = [_dot(cm16[:, ns[g]], hb_ref[j, :, gs[g]]) for g in groups]
    for g in groups:
        bt = bm16[:, ns[g]].astype(F32).T.astype(BF16)
        state_ref[:, gs[g]] = st[g] * decay[:, gs[g]] + _dot(bt, xw[:, gs[g]])

    for g in groups:
        col_f = cols[g * CHUNK:(g + 1) * CHUNK]
        col_b = cols[(SSD_GROUPS + g) * CHUNK:(SSD_GROUPS + g + 1) * CHUNK]
        for pair in range(SSD_HPG // 2):
            ms = []
            for r in (2 * pair, 2 * pair + 1):
                h = g * SSD_HPG + r
                hb_col = SSD_HEADS + h
                rs = slice(r * CHUNK, (r + 1) * CHUNK)
                seg_f = col_f[:, rs] - key_f[h:h + 1, :]
                seg_b = key_b[hb_col:hb_col + 1, :] - col_b[:, rs]
                ms.append((cbs[g] * jnp.exp2(jnp.where(lower, seg_f, seg_b))).astype(BF16))
            ps = slice((g * SSD_HPG + 2 * pair) * SSD_HEAD_DIM, (g * SSD_HPG + 2 * pair + 2) * SSD_HEAD_DIM)
            x2 = xs16[:, ps]
            zero = jnp.zeros_like(x2)
            rhs = jnp.concatenate([jnp.where(first_head, x2, zero), jnp.where(first_head, zero, x2)], axis=0)
            ybuf_ref[:, ps] = _dot(jnp.concatenate(ms, axis=1), rhs)

    zf = z_ref[rows, :].astype(F32)
    zgate = zf * jax.nn.sigmoid(zf)
    for g in groups:
        y = (ybuf_ref[:, gs[g]] + scale_f[:, gs[g]] * off_f[g] + scale_b[:, gs[g]] * off_b[g]
             + (dskip_ref[:, gs[g]] + self_b[:, gs[g]]) * xs[:, gs[g]])
        v = y * zgate[:, gs[g]]
        v = v * lax.rsqrt(jnp.mean(v * v, axis=-1, keepdims=True) + EPS)
        y_ref[rows, gs[g]] = (v * ng_ref[:, gs[g]]).astype(BF16)


def _ssd_main(act, dt, z, hb, hp, dskip, norm_g, e2f, sel, bsz, nc):
    steps = nc // MAIN_CHUNKS
    rows = MAIN_CHUNKS * CHUNK
    block_of = lambda b, c: b * steps + c
    return pl.pallas_call(
        _ssd_main_kernel,
        grid=(bsz, steps),
        in_specs=[pl.BlockSpec((rows, SSD_XBC), lambda b, c: (block_of(b, c), 0)),
                  pl.BlockSpec((rows, LANES), lambda b, c: (block_of(b, c), 0)),
                  pl.BlockSpec((rows, SSD_INNER), lambda b, c: (block_of(b, c), 0)),
                  pl.BlockSpec((MAIN_CHUNKS, SSD_STATE, SSD_INNER), lambda b, c: (block_of(b, c), 0, 0)),
                  _resident((8, LANES)),
                  _resident((1, SSD_INNER)),
                  _resident((1, SSD_INNER)),
                  _resident((2 * LANES, SSD_INNER)),
                  _resident((2 * LANES, SSD_HPG * CHUNK))],
        out_specs=pl.BlockSpec((rows, SSD_INNER), lambda b, c: (block_of(b, c), 0)),
        out_shape=jax.ShapeDtypeStruct((bsz * nc * CHUNK, SSD_INNER), BF16),
        scratch_shapes=[pltpu.VMEM((SSD_STATE, SSD_INNER), F32),
                        pltpu.VMEM((MAIN_CHUNKS, CHUNK, SSD_INNER), F32)],
        compiler_params=_params(2),
        name="ssd_main",
    )(act, dt, z, hb, hp, dskip, norm_g, e2f, sel)


def _swa_kernel(q_ref, kvp_ref, kvc_ref, kvn_ref, sink_ref, o_ref):
    n = pl.program_id(1)
    last = pl.num_programs(1) - 1
    kj = lax.broadcasted_iota(jnp.int32, (3 * ATTN_BLOCK, ATTN_BLOCK), 0)
    qi = lax.broadcasted_iota(jnp.int32, (3 * ATTN_BLOCK, ATTN_BLOCK), 1)
    rel = kj - qi
    band = (rel >= 0) & (rel <= 2 * ATTN_BLOCK)
    kvcat = jnp.concatenate([kvp_ref[...], kvc_ref[...], kvn_ref[...]], axis=0)
    kcat = kvcat[:, :ATTN_KV_WIDTH]
    vcat_t = kvcat[:, ATTN_KV_WIDTH:].astype(F32).T.astype(BF16)
    scores = []
    for j in range(SWA_BLOCKS):
        valid = band
        if j == 0:
            valid = valid & ((kj >= ATTN_BLOCK) | (n > 0))
        if j == SWA_BLOCKS - 1:
            valid = valid & ((kj < 2 * ATTN_BLOCK) | (n < last))
        bias = jnp.concatenate([jnp.where(valid, 0.0, NEG_BIG)] * ATTN_GQA, axis=1)
        edge_bias = (bias[:ATTN_BLOCK], bias[2 * ATTN_BLOCK:])
        keys = slice(j * ATTN_BLOCK, (j + 3) * ATTN_BLOCK)
        qrows = slice(j * ATTN_BLOCK, (j + 1) * ATTN_BLOCK)
        for kv in range(ATTN_KV_HEADS):
            ks = slice(kv * ATTN_HEAD_DIM, (kv + 1) * ATTN_HEAD_DIM)
            q_stack = jnp.concatenate(
                [q_ref[qrows, (kv * ATTN_GQA + r) * ATTN_HEAD_DIM:(kv * ATTN_GQA + r + 1) * ATTN_HEAD_DIM]
                 for r in range(ATTN_GQA)], axis=0)
            s = _dot_nt(kcat[keys, ks], q_stack)
            scores.append(jnp.concatenate([s[:ATTN_BLOCK] + edge_bias[0], s[ATTN_BLOCK:2 * ATTN_BLOCK],
                                           s[2 * ATTN_BLOCK:] + edge_bias[1]], axis=0))
    probs = []
    for idx, s in enumerate(scores):
        kv = idx % ATTN_KV_HEADS
        sink = sink_ref[kv:kv + 1, :]
        m = jnp.maximum(jnp.max(s, axis=0, keepdims=True), sink)
        p = jnp.exp2(s - m)
        denom = jnp.sum(p, axis=0, keepdims=True) + jnp.exp2(sink - m)
        probs.append((p.astype(BF16), 1.0 / denom))
    for j in range(SWA_BLOCKS):
        keys = slice(j * ATTN_BLOCK, (j + 3) * ATTN_BLOCK)
        outs = []
        for kv in range(ATTN_KV_HEADS):
            ks = slice(kv * ATTN_HEAD_DIM, (kv + 1) * ATTN_HEAD_DIM)
            p16, inv = probs[j * ATTN_KV_HEADS + kv]
            o_t = _dot(vcat_t[ks, keys], p16) * inv
            outs.extend(o_t[:, r * ATTN_BLOCK:(r + 1) * ATTN_BLOCK] for r in range(ATTN_GQA))
        o_ref[j * ATTN_BLOCK:(j + 1) * ATTN_BLOCK, :] = jnp.concatenate(outs, axis=0).T.astype(BF16)


def _swa(q, kv, sink_rows, bsz, nb):
    steps = nb // SWA_BLOCKS
    rows = SWA_BLOCKS * ATTN_BLOCK
    cur = lambda b, n: (b * steps + n, 0)
    prev = lambda b, n: (b * nb + jnp.maximum(SWA_BLOCKS * n - 1, 0), 0)
    nxt = lambda b, n: (b * nb + jnp.minimum(SWA_BLOCKS * (n + 1), nb - 1), 0)
    return pl.pallas_call(
        _swa_kernel,
        grid=(bsz, steps),
        in_specs=[pl.BlockSpec((rows, ATTN_WIDTH), cur),
                  pl.BlockSpec((ATTN_BLOCK, 2 * ATTN_KV_WIDTH), prev),
                  pl.BlockSpec((rows, 2 * ATTN_KV_WIDTH), cur),
                  pl.BlockSpec((ATTN_BLOCK, 2 * ATTN_KV_WIDTH), nxt),
                  _resident((8, ATTN_GQA * ATTN_BLOCK))],
        out_specs=pl.BlockSpec((rows, ATTN_WIDTH), cur),
        out_shape=jax.ShapeDtypeStruct(q.shape, BF16),
        compiler_params=_params(2),
        name="swa",
    )(q, kv, kv, kv, sink_rows)


def _merge_kernel(x_ref, y_ref, a_ref, gate_ref, ws_ref, wa_ref, wo_ref, o_ref):
    bs = _dot(y_ref[...], ws_ref[...])
    ba = _dot(a_ref[...], wa_ref[...])
    g = jax.nn.sigmoid(gate_ref[...].astype(F32))
    mix = (g[:, :D_MODEL] * bs + g[:, D_MODEL:] * ba).astype(BF16)
    o_ref[...] = x_ref[...] + _dot(mix, wo_ref[...])


def _merge(x2, y, attn, gates, ws, wa, wo):
    t = x2.shape[0]
    tm = 512
    row = lambda i: (i, 0)
    return pl.pallas_call(
        _merge_kernel,
        grid=(t // tm,),
        in_specs=[pl.BlockSpec((tm, D_MODEL), row),
                  pl.BlockSpec((tm, SSD_INNER), row),
                  pl.BlockSpec((tm, ATTN_WIDTH), row),
                  pl.BlockSpec((tm, 2 * D_MODEL), row),
                  _resident((SSD_INNER, D_MODEL)),
                  _resident((ATTN_WIDTH, D_MODEL)),
                  _resident((D_MODEL, D_MODEL))],
        out_specs=pl.BlockSpec((tm, D_MODEL), row),
        out_shape=jax.ShapeDtypeStruct((t, D_MODEL), F32),
        compiler_params=_params(1),
        name="merge",
    )(x2, y, attn, gates, ws, wa, wo)


def _mem_kv_kernel(m_ref, g_ref, w_ref, o_ref):
    h = _rms(m_ref[...], g_ref[...]).astype(BF16)
    o_ref[...] = _dot(h, w_ref[...]).astype(BF16)


def _mem_kv(mem2, g, w_kv):
    t = mem2.shape[0]
    tm = 256
    row = lambda i: (i, 0)
    return pl.pallas_call(
        _mem_kv_kernel,
        grid=(t // tm,),
        in_specs=[pl.BlockSpec((tm, D_MODEL), row),
                  _resident((1, D_MODEL)),
                  _resident((D_MODEL, 2 * D_MODEL))],
        out_specs=pl.BlockSpec((tm, 2 * D_MODEL), row),
        out_shape=jax.ShapeDtypeStruct((t, 2 * D_MODEL), BF16),
        compiler_params=_params(1),
        name="mem_kv",
    )(mem2, g, w_kv)


def _xattn_kernel(x_ref, g_ref, kv_ref, wq_ref, wo_ref, o_ref, ctx_ref):
    x = x_ref[...]
    h = _rms(x, g_ref[...]).astype(BF16)
    q = (_dot(h, wq_ref[...]) * (XATTN_HEAD_DIM ** -0.5 * LOG2_E)).astype(BF16)
    heads = [slice(hd * XATTN_HEAD_DIM, (hd + 1) * XATTN_HEAD_DIM) for hd in range(XATTN_HEADS)]
    scores = [_dot_nt(q[:, ks], kv_ref[:, ks]) for ks in heads]
    probs = []
    for s in scores:
        p = jnp.exp2(s - jnp.max(s, axis=-1, keepdims=True))
        probs.append((p.astype(BF16), 1.0 / jnp.sum(p, axis=-1, keepdims=True)))
    for ks, (p16, inv) in zip(heads, probs):
        vs = slice(D_MODEL + ks.start, D_MODEL + ks.stop)
        ctx_ref[:, ks] = (_dot(p16, kv_ref[:, vs]) * inv).astype(BF16)
    o_ref[...] = x + _dot(ctx_ref[...], wo_ref[...])


def _xattn(x2, g, kv, wq, wo, seq, mem_len):
    t = x2.shape[0]
    tm = 512
    per_seq = seq // tm
    row = lambda i: (i, 0)
    return pl.pallas_call(
        _xattn_kernel,
        grid=(t // tm,),
        in_specs=[pl.BlockSpec((tm, D_MODEL), row),
                  _resident((1, D_MODEL)),
                  pl.BlockSpec((mem_len, 2 * D_MODEL), lambda i: (i // per_seq, 0)),
                  _resident((D_MODEL, D_MODEL)),
                  _resident((D_MODEL, D_MODEL))],
        out_specs=pl.BlockSpec((tm, D_MODEL), row),
        out_shape=jax.ShapeDtypeStruct((t, D_MODEL), F32),
        scratch_shapes=[pltpu.VMEM((tm, D_MODEL), BF16)],
        compiler_params=_params(1),
        name="xattn",
    )(x2, g, kv, wq, wo)


MXU_TILE = 256
FFN_SPLIT = (FFN_HIDDEN // MXU_TILE + 1) // 2 * MXU_TILE


def _ffn_kernel(x_ref, g_ref, wi_ref, wd_ref, gf_ref, o_ref):
    x = x_ref[...]
    h = _rms(x, g_ref[...]).astype(BF16)
    acc = x
    for lo, hi in ((0, FFN_SPLIT), (FFN_SPLIT, FFN_HIDDEN)):
        gate = _dot(h, wi_ref[:, lo:hi])
        up = _dot(h, wi_ref[:, FFN_HIDDEN + lo:FFN_HIDDEN + hi])
        act = (gate * jax.nn.sigmoid(gate) * up).astype(BF16)
        acc = acc + _dot(act, wd_ref[lo:hi, :])
    o_ref[...] = _rms(acc, gf_ref[...])


def _ffn(x2, g, wi, wd, gf):
    t = x2.shape[0]
    tm = 512
    row = lambda i: (i, 0)
    return pl.pallas_call(
        _ffn_kernel,
        grid=(t // tm,),
        in_specs=[pl.BlockSpec((tm, D_MODEL), row),
                  _resident((1, D_MODEL)),
                  _resident((D_MODEL, 2 * FFN_HIDDEN)),
                  _resident((FFN_HIDDEN, D_MODEL)),
                  _resident((1, D_MODEL))],
        out_specs=pl.BlockSpec((tm, D_MODEL), row),
        out_shape=jax.ShapeDtypeStruct((t, D_MODEL), F32),
        compiler_params=_params(1),
        name="ffn",
    )(x2, g, wi, wd, gf)


def _pad_rows(a, rows):
    return jnp.pad(a, ((0, rows - a.shape[0]), (0, 0)))


def _pad_cols(a, cols):
    return jnp.pad(a, ((0, 0), (0, cols - a.shape[1])))


def _column_select_matrix():
    j = jnp.arange(LANES)[:, None]
    blk = jnp.arange(SSD_HPG * CHUNK)[None, :] // CHUNK
    e = (j == blk).astype(BF16)
    return jnp.concatenate([e, e], axis=0)


def _select_matrix(slot):
    j = jnp.arange(LANES)[:, None]
    ch = jnp.arange(SSD_INNER)[None, :]
    e = (j == slot * SSD_HEADS + ch // SSD_HEAD_DIM).astype(BF16)
    return jnp.concatenate([e, e], axis=0)


def _layer(x2, mem2, bsz, seq, mem_len, norm_mix_g, w_in, conv_w, conv_b, dt_bias_fwd, dt_bias_bwd,
           a_log_fwd, a_log_bwd, d_skip, ssd_norm_g, attn_sink, w_branch_ssd, w_branch_attn, w_mix_out,
           norm_xattn_g, norm_mem_g, w_xattn_q, w_xattn_kv, w_xattn_out, norm_ffn_g, w_ffn_in, w_ffn_out,
           final_g):
    nc = seq // CHUNK
    s0 = SSD_INNER
    s1 = s0 + SSD_XBC
    s2 = s1 + 2 * SSD_HEADS
    s3 = s2 + ATTN_WIDTH
    s4 = s3 + ATTN_KV_WIDTH
    s5 = s4 + ATTN_KV_WIDTH
    w_packed = jnp.concatenate(
        [w_in[:, :s1], w_in[:, s2:s3], w_in[:, s3:s4], w_in[:, s4:s5], w_in[:, s5:],
         _pad_cols(w_in[:, s1:s2], LANES)], axis=1).astype(BF16)

    half = ATTN_HEAD_DIM // 2
    inv_freq = ROPE_THETA ** (-jnp.arange(half, dtype=F32) / half)
    ang = jnp.arange(seq, dtype=F32)[:, None] * inv_freq[None]
    cos = jnp.tile(jnp.cos(ang), (1, LANES // half))
    sin_signed = jnp.tile(jnp.concatenate([-jnp.sin(ang), jnp.sin(ang)], axis=1), (1, LANES // ATTN_HEAD_DIM))

    cw = _pad_rows(conv_w, 8)
    z, act, q, kv_attn, gates, dt = _in_proj(x2, norm_mix_g[None], w_packed, cos, sin_signed, cw, conv_b[None], seq)

    hp = _pad_rows(jnp.stack([_pad_cols(jnp.concatenate([dt_bias_fwd, dt_bias_bwd])[None], LANES)[0],
                              _pad_cols(jnp.concatenate([a_log_fwd, a_log_bwd])[None], LANES)[0]]), 8)
    e2f = _select_matrix(0)
    e2b = _select_matrix(1)
    hb = _ssd_bwd(act, dt, hp, e2b, bsz, nc)
    dskip = jnp.repeat(d_skip, SSD_HEAD_DIM)[None]
    y = _ssd_main(act, dt, z, hb, hp, dskip, ssd_norm_g[None], e2f, _column_select_matrix(), bsz, nc)

    sink_rows = _pad_rows(jnp.repeat(attn_sink * LOG2_E, ATTN_BLOCK).reshape(ATTN_KV_HEADS, ATTN_GQA * ATTN_BLOCK), 8)
    attn = _swa(q, kv_attn, sink_rows, bsz, seq // ATTN_BLOCK)

    x2 = _merge(x2, y, attn, gates, w_branch_ssd.astype(BF16), w_branch_attn.astype(BF16),
                w_mix_out.astype(BF16))

    kv = _mem_kv(mem2, norm_mem_g[None], w_xattn_kv.astype(BF16))
    x2 = _xattn(x2, norm_xattn_g[None], kv, w_xattn_q.astype(BF16), w_xattn_out.astype(BF16), seq, mem_len)

    return _ffn(x2, norm_ffn_g[None], w_ffn_in.astype(BF16), w_ffn_out.astype(BF16), final_g)


def kernel(x, mem, norm_mix_g, w_in, conv_w, conv_b, dt_bias_fwd, dt_bias_bwd, a_log_fwd, a_log_bwd, d_skip,
           ssd_norm_g, attn_sink, w_branch_ssd, w_branch_attn, w_mix_out, norm_xattn_g, norm_mem_g, w_xattn_q,
           w_xattn_kv, w_xattn_out, norm_ffn_g, w_ffn_in, w_ffn_out, norm_final_g):
    bsz, seq, _ = x.shape
    mem_len = mem.shape[1]
    assert w_in.shape[0] == 1, "single-layer stack expected"
    layer = 0
    x2 = x.reshape(bsz * seq, D_MODEL)
    mem2 = mem.reshape(bsz * mem_len, D_MODEL)
    out = _layer(x2, mem2, bsz, seq, mem_len, norm_mix_g[layer], w_in[layer], conv_w[layer], conv_b[layer],
                 dt_bias_fwd[layer], dt_bias_bwd[layer], a_log_fwd[layer], a_log_bwd[layer], d_skip[layer],
                 ssd_norm_g[layer], attn_sink[layer], w_branch_ssd[layer], w_branch_attn[layer],
                 w_mix_out[layer], norm_xattn_g[layer], norm_mem_g[layer], w_xattn_q[layer],
                 w_xattn_kv[layer], w_xattn_out[layer], norm_ffn_g[layer], w_ffn_in[layer],
                 w_ffn_out[layer], norm_final_g[None])
    return out.reshape(bsz, seq, D_MODEL)
```

```python
import functools

import jax
import jax.numpy as jnp
import numpy as np
from jax import lax
from jax.experimental import pallas as pl
from jax.experimental.pallas import tpu as pltpu

F32 = jnp.float32
BF16 = jnp.bfloat16

D_MODEL = 1024
EPS = 1e-6
SSD_INNER = 2048
SSD_HEAD_DIM = 64
SSD_HEADS = 32
SSD_GROUPS = 4
SSD_HPG = 8
SSD_STATE = 128
SSD_CONV = 5
CHUNK = 128
BWD_CHUNKS = 4
MAIN_CHUNKS = 2
DECAY_ROWS = 16
WIDE_ROWS = 4 * CHUNK + DECAY_ROWS
SSD_BC = SSD_GROUPS * SSD_STATE
SSD_XBC = SSD_INNER + 2 * SSD_BC
GROUP_W = SSD_HPG * SSD_HEAD_DIM
ATTN_HEAD_DIM = 64
ATTN_KV_HEADS = 4
ATTN_GQA = 4
ATTN_WIDTH = 1024
ATTN_KV_WIDTH = 256
ATTN_BLOCK = 128
SWA_BLOCKS = 2
ROPE_THETA = 10000.0
XATTN_HEADS = 4
XATTN_HEAD_DIM = 256
FFN_HIDDEN = 2816

LANES = 128
NEG_BIG = -1e30
LOG2_E = 1.4426950408889634

OFF_Z = 0
OFF_XBC = OFF_Z + SSD_INNER
OFF_Q = OFF_XBC + SSD_XBC
OFF_K = OFF_Q + ATTN_WIDTH
OFF_V = OFF_K + ATTN_KV_WIDTH
OFF_G = OFF_V + ATTN_KV_WIDTH
OFF_DT = OFF_G + 2 * D_MODEL
IN_PACKED = OFF_DT + LANES

VMEM_LIMIT = 56 * 1024 * 1024


def _params(n_axes, flags=None):
    return pltpu.CompilerParams(dimension_semantics=("arbitrary",) * n_axes,
                                vmem_limit_bytes=VMEM_LIMIT, flags=flags)


def _resident(shape):
    nd = len(shape)
    return pl.BlockSpec(shape, lambda *_: (0,) * nd, pipeline_mode=pl.Buffered(1))


def _rms(x, g):
    return x * lax.rsqrt(jnp.mean(x * x, axis=-1, keepdims=True) + EPS) * g


def _silu(x):
    h = 0.5 * x
    return h + h * jnp.tanh(h)


def _dot(a, b):
    return jnp.dot(a, b, preferred_element_type=F32)


def _dot_nt(a, b):
    return lax.dot_general(a, b, (((1,), (1,)), ((), ())), preferred_element_type=F32)


def _rotary(u, cos, sin_signed):
    n = u.shape[1]
    lane = lax.broadcasted_iota(jnp.int32, u.shape, 1)
    first_half = (lane % ATTN_HEAD_DIM) < (ATTN_HEAD_DIM // 2)
    partner = jnp.where(first_half, pltpu.roll(u, n - ATTN_HEAD_DIM // 2, axis=1),
                        pltpu.roll(u, ATTN_HEAD_DIM // 2, axis=1))
    reps = n // LANES
    cos_t = jnp.concatenate([cos] * reps, axis=1) if reps > 1 else cos
    sin_t = jnp.concatenate([sin_signed] * reps, axis=1) if reps > 1 else sin_signed
    return u * cos_t + partner * sin_t


def _in_proj_kernel(x_ref, xp_ref, xn_ref, g_ref, w_ref, cos_ref, sin_ref, cw_ref, cb_ref,
                    z_ref, act_ref, q_ref, kv_ref, gate_ref, dt_ref, ext_ref, *, per_seq):
    i = pl.program_id(0)
    tm = x_ref.shape[0]
    g = g_ref[...]
    h_ext = _rms(jnp.concatenate([x_ref[...], xp_ref[...], xn_ref[...]], axis=0), g).astype(BF16)
    step = 256
    rblk = 128
    row_blocks = [slice(r, r + rblk) for r in range(0, tm, rblk)]
    has_prev = (i % per_seq) > 0
    has_next = (i % per_seq) < per_seq - 1

    def proj(rs, off, width):
        return _dot(h_ext[rs], w_ref[:, off:off + width])

    def xbc_proj(c):
        cs = slice(c, c + step)
        for rs in row_blocks[:-1]:
            ext_ref[8 + rs.start:8 + rs.stop, cs] = proj(rs, OFF_XBC + c, step)
        last = row_blocks[-1]
        u = proj(slice(last.start, tm + 16), OFF_XBC + c, step)
        ext_ref[8 + last.start:8 + tm, cs] = u[:rblk]
        ext_ref[0:8, cs] = jnp.where(has_prev, u[rblk:rblk + 8], 0.0)
        ext_ref[8 + tm:16 + tm, cs] = jnp.where(has_next, u[rblk + 8:], 0.0)

    assert SSD_CONV == 5

    def conv_silu(c):
        cs = slice(c, c + step)
        e = ext_ref[:, cs]
        rows = e.shape[0]
        tap = lambda k: cw_ref[k:k + 1, cs] * e
        down = lambda a: pltpu.roll(a, 1, axis=0)
        up = lambda a: pltpu.roll(a, rows - 1, axis=0)
        left = down(down(tap(0)) + tap(1)) + tap(2)
        right = up(up(tap(4)) + tap(3))
        acc = (left + right)[8:8 + tm] + cb_ref[:, cs]
        act_ref[:, cs] = _silu(acc).astype(BF16)

    def z_proj(c):
        for rs in row_blocks:
            z_ref[rs, c:c + step] = proj(rs, OFF_Z + c, step).astype(BF16)

    def q_proj(c):
        for rs in row_blocks:
            q = _rotary(proj(rs, OFF_Q + c, step), cos_ref[rs, :], sin_ref[rs, :])
            q_ref[rs, c:c + step] = (q * (ATTN_HEAD_DIM ** -0.5 * LOG2_E)).astype(BF16)

    def kv_proj(_):
        for rs in row_blocks:
            kv = proj(rs, OFF_K, 2 * ATTN_KV_WIDTH)
            kv_ref[rs, :ATTN_KV_WIDTH] = _rotary(kv[:, :ATTN_KV_WIDTH], cos_ref[rs, :], sin_ref[rs, :]).astype(BF16)
            kv_ref[rs, ATTN_KV_WIDTH:] = kv[:, ATTN_KV_WIDTH:].astype(BF16)
            dt_ref[rs, :] = proj(rs, OFF_DT, LANES)

    def gate_proj(c):
        for rs in row_blocks:
            gate_ref[rs, c:c + step] = proj(rs, OFF_G + c, step).astype(BF16)

    mxu_tasks = ([(z_proj, c) for c in range(0, SSD_INNER, step)]
                 + [(q_proj, c) for c in range(0, ATTN_WIDTH, step)] + [(kv_proj, 0)]
                 + [(gate_proj, c) for c in range(0, 2 * D_MODEL, step)])
    conv_chunks = list(range(0, SSD_XBC, step))
    xbc_proj(conv_chunks[0])
    for j, c in enumerate(conv_chunks):
        if j + 1 < len(conv_chunks):
            xbc_proj(conv_chunks[j + 1])
        for fn, arg in mxu_tasks[2 * j:2 * j + 2]:
            fn(arg)
        conv_silu(c)
    for fn, arg in mxu_tasks[2 * len(conv_chunks):]:
        fn(arg)


def _in_proj(x2, g, w_packed, cos, sin_signed, conv_w, conv_b, seq):
    t = x2.shape[0]
    tm = 512
    per_seq = seq // tm
    halo_per_tile = tm // 8
    last_halo = t // 8 - 1
    row = lambda i: (i, 0)
    pos = lambda i: (i % per_seq, 0)
    outs = [(SSD_INNER, BF16), (SSD_XBC, BF16), (ATTN_WIDTH, BF16), (2 * ATTN_KV_WIDTH, BF16),
            (2 * D_MODEL, BF16), (LANES, F32)]
    return pl.pallas_call(
        functools.partial(_in_proj_kernel, per_seq=per_seq),
        grid=(t // tm,),
        in_specs=[pl.BlockSpec((tm, D_MODEL), row),
                  pl.BlockSpec((8, D_MODEL), lambda i: (jnp.maximum(i * halo_per_tile - 1, 0), 0)),
                  pl.BlockSpec((8, D_MODEL), lambda i: (jnp.minimum((i + 1) * halo_per_tile, last_halo), 0)),
                  _resident((1, D_MODEL)),
                  _resident((D_MODEL, IN_PACKED)),
                  pl.BlockSpec((tm, LANES), pos),
                  pl.BlockSpec((tm, LANES), pos),
                  _resident((8, SSD_XBC)),
                  _resident((1, SSD_XBC))],
        out_specs=[pl.BlockSpec((tm, w), row) for w, _ in outs],
        out_shape=[jax.ShapeDtypeStruct((t, w), d) for w, d in outs],
        scratch_shapes=[pltpu.VMEM((tm + 16, SSD_XBC), F32)],
        compiler_params=_params(1),
        name="in_proj",
    )(x2, x2, x2, g, w_packed, cos, sin_signed, conv_w, conv_b)


def _cumsum_rows(a):
    row = lax.broadcasted_iota(jnp.int32, a.shape, 0)
    v = a
    k = 1
    while k < a.shape[0]:
        v = v + jnp.where(row >= k, pltpu.roll(v, k, axis=0), 0.0)
        k *= 2
    return v


def _expand(v, e2_ref):
    hi = v.astype(BF16)
    lo = (v - hi.astype(F32)).astype(BF16)
    return _dot(jnp.concatenate([hi, lo], axis=1), e2_ref[...])


def _softplus(x):
    return jnp.maximum(x, 0.0) + jnp.log1p(jnp.exp(-jnp.abs(x)))


def _head_scalars(dt_raw, hp_ref):
    dtv = _softplus(dt_raw + hp_ref[0:1, :])
    a = dtv * (-LOG2_E * jnp.exp(hp_ref[1:2, :]))
    return dtv, a


def _chunk_scalars(dt_raw, hp_ref):
    dtv, a = _head_scalars(dt_raw, hp_ref)
    incl = _cumsum_rows(a)
    return dtv, incl, incl - a


def _ssd_bwd_kernel(act_ref, dt_ref, hp_ref, e2b_ref, hb_ref, state_ref):
    c = pl.program_id(1)

    @pl.when(c == 0)
    def _():
        state_ref[...] = jnp.zeros_like(state_ref)

    scales, decays = [], []
    for j in range(BWD_CHUNKS):
        dtv, incl, excl = _chunk_scalars(dt_ref[j * CHUNK:(j + 1) * CHUNK, :], hp_ref)
        scales.append(jnp.exp2(excl) * dtv)
        decays.append(jnp.exp2(incl[CHUNK - DECAY_ROWS:, :]))
    wide = _expand(jnp.concatenate(scales + decays, axis=0), e2b_ref)

    for j in reversed(range(BWD_CHUNKS)):
        rows = slice(j * CHUNK, (j + 1) * CHUNK)
        hb_ref[j] = state_ref[...].astype(BF16)
        xs = act_ref[rows, :SSD_INNER].astype(F32)
        bm = act_ref[rows, SSD_INNER:].astype(F32)
        xw = (xs * wide[rows]).astype(BF16)
        last = BWD_CHUNKS * CHUNK + (j + 1) * DECAY_ROWS - 1
        decay = wide[last:last + 1, :]
        for g in range(SSD_GROUPS):
            gs = slice(g * GROUP_W, (g + 1) * GROUP_W)
            bt = bm[:, g * SSD_STATE:(g + 1) * SSD_STATE].T.astype(BF16)
            state_ref[:, gs] = state_ref[:, gs] * decay[:, gs] + _dot(bt, xw[:, gs])


def _ssd_bwd(act, dt, hp, e2b, bsz, nc):
    width = SSD_INNER + SSD_BC
    steps = nc // BWD_CHUNKS
    block_of = lambda b, c: b * steps + (steps - 1 - c)
    return pl.pallas_call(
        _ssd_bwd_kernel,
        grid=(bsz, steps),
        in_specs=[pl.BlockSpec((BWD_CHUNKS * CHUNK, width), lambda b, c: (block_of(b, c), 0)),
                  pl.BlockSpec((BWD_CHUNKS * CHUNK, LANES), lambda b, c: (block_of(b, c), 0)),
                  _resident((8, LANES)),
                  _resident((2 * LANES, SSD_INNER))],
        out_specs=pl.BlockSpec((BWD_CHUNKS, SSD_STATE, SSD_INNER), lambda b, c: (block_of(b, c), 0, 0)),
        out_shape=jax.ShapeDtypeStruct((bsz * nc, SSD_STATE, SSD_INNER), BF16),
        scratch_shapes=[pltpu.VMEM((SSD_STATE, SSD_INNER), F32)],
        compiler_params=_params(2),
        name="ssd_bwd",
    )(act, dt, hp, e2b)


def _ssd_main_kernel(act_ref, dt_ref, z_ref, hb_ref, hp_ref, dskip_ref, ng_ref, e2f_ref, sel_ref,
                     y_ref, state_ref, ybuf_ref):
    c = pl.program_id(1)

    @pl.when(c == 0)
    def _():
        state_ref[...] = jnp.zeros_like(state_ref)

    lane = lax.broadcasted_iota(jnp.int32, (CHUNK, LANES), 1)
    row = lax.broadcasted_iota(jnp.int32, (CHUNK, CHUNK), 0)
    col = lax.broadcasted_iota(jnp.int32, (CHUNK, CHUNK), 1)
    groups = range(SSD_GROUPS)
    to_fwd_slot = lambda v: pltpu.roll(v, LANES - SSD_HEADS, axis=1)
    prep, wide_in, col_in = [], [], []
    for j in range(MAIN_CHUNKS):
        rows = slice(j * CHUNK, (j + 1) * CHUNK)
        dtv, incl, excl = _chunk_scalars(dt_ref[rows, :], hp_ref)
        tot = incl[CHUNK - 1:CHUNK, :]
        cbs = [_dot_nt(act_ref[rows, SSD_INNER + SSD_BC + g * SSD_STATE:SSD_INNER + SSD_BC + (g + 1) * SSD_STATE],
                       act_ref[rows, SSD_INNER + g * SSD_STATE:SSD_INNER + (g + 1) * SSD_STATE]) for g in groups]
        lane_group = (lane % SSD_HEADS) // SSD_HPG
        cb_diag = jnp.zeros_like(dtv)
        for g in groups:
            dg = jnp.sum(jnp.where(row == col, cbs[g], 0.0), axis=1, keepdims=True)
            cb_diag = jnp.where(lane_group == g, dg, cb_diag)
        wide_in += [jnp.exp2(incl),
                    jnp.exp2(tot - incl) * dtv,
                    to_fwd_slot(jnp.exp2(tot - excl)),
                    to_fwd_slot(cb_diag * dtv),
                    jnp.exp2(incl[CHUNK - DECAY_ROWS:, :])]
        log_dt = jnp.log2(dtv)
        key_f = (incl - log_dt).T
        key_b = (excl + log_dt).T
        vmix = jnp.where(lane < SSD_HEADS, incl, excl)
        hi = vmix.astype(BF16).astype(F32)
        lo = vmix - hi
        for shift in [d * SSD_HEADS + g * SSD_HPG for d in range(2) for g in groups]:
            sh = (LANES - shift) % LANES
            col_in.append(jnp.concatenate([pltpu.roll(hi, sh, axis=1) if sh else hi,
                                           pltpu.roll(lo, sh, axis=1) if sh else lo], axis=1).astype(BF16))
        prep.append((cbs, key_f, key_b))
    wide = _expand(jnp.concatenate(wide_in, axis=0), e2f_ref)
    cols = _dot(jnp.concatenate(col_in, axis=0), sel_ref[...])

    for j in range(MAIN_CHUNKS):
        w0 = j * WIDE_ROWS
        c0 = j * 2 * SSD_GROUPS * CHUNK
        _ssd_main_chunk(j, prep[j], wide[w0:w0 + WIDE_ROWS], cols[c0:c0 + 2 * SSD_GROUPS * CHUNK],
                        act_ref, z_ref, hb_ref, dskip_ref, ng_ref, y_ref, state_ref, ybuf_ref.at[j])


def _ssd_main_chunk(j, prep, wide, cols, act_ref, z_ref, hb_ref, dskip_ref, ng_ref, y_ref, state_ref, ybuf_ref):
    rows = slice(j * CHUNK, (j + 1) * CHUNK)
    cbs, key_f, key_b = prep
    scale_f = wide[0:CHUNK]
    state_scale = wide[CHUNK:2 * CHUNK]
    scale_b = wide[2 * CHUNK:3 * CHUNK]
    self_b = wide[3 * CHUNK:4 * CHUNK]
    decay = wide[WIDE_ROWS - 1:WIDE_ROWS, :]
    xs16 = act_ref[rows, :SSD_INNER]
    xs = xs16.astype(F32)
    bm16 = act_ref[rows, SSD_INNER:SSD_INNER + SSD_BC]
    cm16 = act_ref[rows, SSD_INNER + SSD_BC:]

    row = lax.broadcasted_iota(jnp.int32, (CHUNK, CHUNK), 0)
    col = lax.broadcasted_iota(jnp.int32, (CHUNK, CHUNK), 1)
    lower = row >= col
    first_head = lax.broadcasted_iota(jnp.int32, (CHUNK, 2 * SSD_HEAD_DIM), 1) < SSD_HEAD_DIM

    groups = range(SSD_GROUPS)
    ns = [slice(g * SSD_STATE, (g + 1) * SSD_STATE) for g in groups]
    gs = [slice(g * GROUP_W, (g + 1) * GROUP_W) for g in groups]
    xw = (xs * state_scale).astype(BF16)
    st = [state_ref[:, gs[g]] for g in groups]
    off_f = [_dot(cm16[:, ns[g]], st[g].astype(BF16)) for g in groups]
    off_b = [_dot(cm16[:, ns[g]], hb_ref[j, :, gs[g]]) for g in groups]
    for g in groups:
        bt = bm16[:, ns[g]].astype(F32).T.astype(BF16)
        state_ref[:, gs[g]] = st[g] * decay[:, gs[g]] + _dot(bt, xw[:, gs[g]])

    for g in groups:
        col_f = cols[g * CHUNK:(g + 1) * CHUNK]
        col_b = cols[(SSD_GROUPS + g) * CHUNK:(SSD_GROUPS + g + 1) * CHUNK]
        for pair in range(SSD_HPG // 2):
            ms = []
            for r in (2 * pair, 2 * pair + 1):
                h = g * SSD_HPG + r
                hb_col = SSD_HEADS + h
                rs = slice(r * CHUNK, (r + 1) * CHUNK)
                seg_f = col_f[:, rs] - key_f[h:h + 1, :]
                seg_b = key_b[hb_col:hb_col + 1, :] - col_b[:, rs]
                ms.append((cbs[g] * jnp.exp2(jnp.where(lower, seg_f, seg_b))).astype(BF16))
            ps = slice((g * SSD_HPG + 2 * pair) * SSD_HEAD_DIM, (g * SSD_HPG + 2 * pair + 2) * SSD_HEAD_DIM)
            x2 = xs16[:, ps]
            zero = jnp.zeros_like(x2)
            rhs = jnp.concatenate([jnp.where(first_head, x2, zero), jnp.where(first_head, zero, x2)], axis=0)
            ybuf_ref[:, ps] = _dot(jnp.concatenate(ms, axis=1), rhs)

    zf = z_ref[rows, :].astype(F32)
    zgate = _silu(zf)
    for g in groups:
        y = (ybuf_ref[:, gs[g]] + scale_f[:, gs[g]] * off_f[g] + scale_b[:, gs[g]] * off_b[g]
             + (dskip_ref[:, gs[g]] + self_b[:, gs[g]]) * xs[:, gs[g]])
        v = y * zgate[:, gs[g]]
        v = v * lax.rsqrt(jnp.mean(v * v, axis=-1, keepdims=True) + EPS)
        y_ref[rows, gs[g]] = (v * ng_ref[:, gs[g]]).astype(BF16)


def _ssd_main(act, dt, z, hb, hp, dskip, norm_g, e2f, sel, bsz, nc):
    steps = nc // MAIN_CHUNKS
    rows = MAIN_CHUNKS * CHUNK
    block_of = lambda b, c: b * steps + c
    return pl.pallas_call(
        _ssd_main_kernel,
        grid=(bsz, steps),
        in_specs=[pl.BlockSpec((rows, SSD_XBC), lambda b, c: (block_of(b, c), 0)),
                  pl.BlockSpec((rows, LANES), lambda b, c: (block_of(b, c), 0)),
                  pl.BlockSpec((rows, SSD_INNER), lambda b, c: (block_of(b, c), 0)),
                  pl.BlockSpec((MAIN_CHUNKS, SSD_STATE, SSD_INNER), lambda b, c: (block_of(b, c), 0, 0)),
                  _resident((8, LANES)),
                  _resident((1, SSD_INNER)),
                  _resident((1, SSD_INNER)),
                  _resident((2 * LANES, SSD_INNER)),
                  _resident((2 * LANES, SSD_HPG * CHUNK))],
        out_specs=pl.BlockSpec((rows, SSD_INNER), lambda b, c: (block_of(b, c), 0)),
        out_shape=jax.ShapeDtypeStruct((bsz * nc * CHUNK, SSD_INNER), BF16),
        scratch_shapes=[pltpu.VMEM((SSD_STATE, SSD_INNER), F32),
                        pltpu.VMEM((MAIN_CHUNKS, CHUNK, SSD_INNER), F32)],
        compiler_params=_params(2),
        name="ssd_main",
    )(act, dt, z, hb, hp, dskip, norm_g, e2f, sel)


def _swa_kernel(q_ref, kvp_ref, kvc_ref, kvn_ref, sink_ref, o_ref):
    n = pl.program_id(1)
    last = pl.num_programs(1) - 1
    kj = lax.broadcasted_iota(jnp.int32, (3 * ATTN_BLOCK, ATTN_BLOCK), 0)
    qi = lax.broadcasted_iota(jnp.int32, (3 * ATTN_BLOCK, ATTN_BLOCK), 1)
    rel = kj - qi
    band = (rel >= 0) & (rel <= 2 * ATTN_BLOCK)
    kvcat = jnp.concatenate([kvp_ref[...], kvc_ref[...], kvn_ref[...]], axis=0)
    kcat = kvcat[:, :ATTN_KV_WIDTH]
    vcat_t = kvcat[:, ATTN_KV_WIDTH:].astype(F32).T.astype(BF16)
    scores = []
    for j in range(SWA_BLOCKS):
        valid = band
        if j == 0:
            valid = valid & ((kj >= ATTN_BLOCK) | (n > 0))
        if j == SWA_BLOCKS - 1:
            valid = valid & ((kj < 2 * ATTN_BLOCK) | (n < last))
        bias = jnp.concatenate([jnp.where(valid, 0.0, NEG_BIG)] * ATTN_GQA, axis=1)
        edge_bias = (bias[:ATTN_BLOCK], bias[2 * ATTN_BLOCK:])
        keys = slice(j * ATTN_BLOCK, (j + 3) * ATTN_BLOCK)
        qrows = slice(j * ATTN_BLOCK, (j + 1) * ATTN_BLOCK)
        for kv in range(ATTN_KV_HEADS):
            ks = slice(kv * ATTN_HEAD_DIM, (kv + 1) * ATTN_HEAD_DIM)
            q_stack = jnp.concatenate(
                [q_ref[qrows, (kv * ATTN_GQA + r) * ATTN_HEAD_DIM:(kv * ATTN_GQA + r + 1) * ATTN_HEAD_DIM]
                 for r in range(ATTN_GQA)], axis=0)
            s = _dot_nt(kcat[keys, ks], q_stack)
            scores.append(jnp.concatenate([s[:ATTN_BLOCK] + edge_bias[0], s[ATTN_BLOCK:2 * ATTN_BLOCK],
                                           s[2 * ATTN_BLOCK:] + edge_bias[1]], axis=0))
    probs = []
    for idx, s in enumerate(scores):
        kv = idx % ATTN_KV_HEADS
        sink = sink_ref[kv:kv + 1, :]
        m = jnp.maximum(jnp.max(s, axis=0, keepdims=True), sink)
        p = jnp.exp2(s - m)
        denom = jnp.sum(p, axis=0, keepdims=True) + jnp.exp2(sink - m)
        probs.append((p.astype(BF16), 1.0 / denom))
    for j in range(SWA_BLOCKS):
        keys = slice(j * ATTN_BLOCK, (j + 3) * ATTN_BLOCK)
        outs = []
        for kv in range(ATTN_KV_HEADS):
            ks = slice(kv * ATTN_HEAD_DIM, (kv + 1) * ATTN_HEAD_DIM)
            p16, inv = probs[j * ATTN_KV_HEADS + kv]
            o_t = _dot(vcat_t[ks, keys], p16) * inv
            outs.extend(o_t[:, r * ATTN_BLOCK:(r + 1) * ATTN_BLOCK] for r in range(ATTN_GQA))
        o_ref[j * ATTN_BLOCK:(j + 1) * ATTN_BLOCK, :] = jnp.concatenate(outs, axis=0).T.astype(BF16)


def _swa(q, kv, sink_rows, bsz, nb):
    steps = nb // SWA_BLOCKS
    rows = SWA_BLOCKS * ATTN_BLOCK
    cur = lambda b, n: (b * steps + n, 0)
    prev = lambda b, n: (b * nb + jnp.maximum(SWA_BLOCKS * n - 1, 0), 0)
    nxt = lambda b, n: (b * nb + jnp.minimum(SWA_BLOCKS * (n + 1), nb - 1), 0)
    return pl.pallas_call(
        _swa_kernel,
        grid=(bsz, steps),
        in_specs=[pl.BlockSpec((rows, ATTN_WIDTH), cur),
                  pl.BlockSpec((ATTN_BLOCK, 2 * ATTN_KV_WIDTH), prev),
                  pl.BlockSpec((rows, 2 * ATTN_KV_WIDTH), cur),
                  pl.BlockSpec((ATTN_BLOCK, 2 * ATTN_KV_WIDTH), nxt),
                  _resident((8, ATTN_GQA * ATTN_BLOCK))],
        out_specs=pl.BlockSpec((rows, ATTN_WIDTH), cur),
        out_shape=jax.ShapeDtypeStruct(q.shape, BF16),
        compiler_params=_params(2),
        name="swa",
    )(q, kv, kv, kv, sink_rows)


def _mem_kv_kernel(m_ref, g_ref, w_ref, o_ref):
    h = _rms(m_ref[...], g_ref[...]).astype(BF16)
    o_ref[...] = _dot(h, w_ref[...]).astype(BF16)


def _mem_kv(mem2, g, w_kv):
    t = mem2.shape[0]
    tm = 256
    row = lambda i: (i, 0)
    return pl.pallas_call(
        _mem_kv_kernel,
        grid=(t // tm,),
        in_specs=[pl.BlockSpec((tm, D_MODEL), row),
                  _resident((1, D_MODEL)),
                  _resident((D_MODEL, 2 * D_MODEL))],
        out_specs=pl.BlockSpec((tm, 2 * D_MODEL), row),
        out_shape=jax.ShapeDtypeStruct((t, 2 * D_MODEL), BF16),
        compiler_params=_params(1),
        name="mem_kv",
    )(mem2, g, w_kv)


def _merge_xattn_kernel(x_ref, y_ref, a_ref, gate_ref, kv_ref, ws_ref, wa_ref, wo_ref, g_ref, wq_ref, wxo_ref,
                        o_ref, ctx_ref):
    bs = _dot(y_ref[...], ws_ref[...])
    ba = _dot(a_ref[...], wa_ref[...])
    gate = jax.nn.sigmoid(gate_ref[...].astype(F32))
    mix = (gate[:, :D_MODEL] * bs + gate[:, D_MODEL:] * ba).astype(BF16)
    x = x_ref[...] + _dot(mix, wo_ref[...])

    h = _rms(x, g_ref[...]).astype(BF16)
    q = (_dot(h, wq_ref[...]) * (XATTN_HEAD_DIM ** -0.5 * LOG2_E)).astype(BF16)
    heads = [slice(hd * XATTN_HEAD_DIM, (hd + 1) * XATTN_HEAD_DIM) for hd in range(XATTN_HEADS)]
    scores = [_dot_nt(q[:, ks], kv_ref[:, ks]) for ks in heads]
    probs = []
    for s in scores:
        p = jnp.exp2(s - jnp.max(s, axis=-1, keepdims=True))
        probs.append((p.astype(BF16), 1.0 / jnp.sum(p, axis=-1, keepdims=True)))
    for ks, (p16, inv) in zip(heads, probs):
        vs = slice(D_MODEL + ks.start, D_MODEL + ks.stop)
        ctx_ref[:, ks] = (_dot(p16, kv_ref[:, vs]) * inv).astype(BF16)
    o_ref[...] = x + _dot(ctx_ref[...], wxo_ref[...])


def _merge_xattn(x2, y, attn, gates, kv, ws, wa, wo, g, wq, wxo, seq, mem_len):
    t = x2.shape[0]
    tm = 512
    per_seq = seq // tm
    row = lambda i: (i, 0)
    return pl.pallas_call(
        _merge_xattn_kernel,
        grid=(t // tm,),
        in_specs=[pl.BlockSpec((tm, D_MODEL), row),
                  pl.BlockSpec((tm, SSD_INNER), row),
                  pl.BlockSpec((tm, ATTN_WIDTH), row),
                  pl.BlockSpec((tm, 2 * D_MODEL), row),
                  pl.BlockSpec((mem_len, 2 * D_MODEL), lambda i: (i // per_seq, 0)),
                  _resident((SSD_INNER, D_MODEL)),
                  _resident((ATTN_WIDTH, D_MODEL)),
                  _resident((D_MODEL, D_MODEL)),
                  _resident((1, D_MODEL)),
                  _resident((D_MODEL, D_MODEL)),
                  _resident((D_MODEL, D_MODEL))],
        out_specs=pl.BlockSpec((tm, D_MODEL), row),
        out_shape=jax.ShapeDtypeStruct((t, D_MODEL), F32),
        scratch_shapes=[pltpu.VMEM((tm, D_MODEL), BF16)],
        compiler_params=_params(1),
        name="merge_xattn",
    )(x2, y, attn, gates, kv, ws, wa, wo, g, wq, wxo)


MXU_TILE = 256
FFN_SPLIT = (FFN_HIDDEN // MXU_TILE + 1) // 2 * MXU_TILE


def _ffn_kernel(x_ref, g_ref, wi_ref, wd_ref, gf_ref, o_ref):
    x = x_ref[...]
    h = _rms(x, g_ref[...]).astype(BF16)
    acc = x
    for lo, hi in ((0, FFN_SPLIT), (FFN_SPLIT, FFN_HIDDEN)):
        gate = _dot(h, wi_ref[:, lo:hi])
        up = _dot(h, wi_ref[:, FFN_HIDDEN + lo:FFN_HIDDEN + hi])
        act = (_silu(gate) * up).astype(BF16)
        acc = acc + _dot(act, wd_ref[lo:hi, :])
    o_ref[...] = _rms(acc, gf_ref[...])


def _ffn(x2, g, wi, wd, gf):
    t = x2.shape[0]
    tm = 512
    row = lambda i: (i, 0)
    return pl.pallas_call(
        _ffn_kernel,
        grid=(t // tm,),
        in_specs=[pl.BlockSpec((tm, D_MODEL), row),
                  _resident((1, D_MODEL)),
                  _resident((D_MODEL, 2 * FFN_HIDDEN)),
                  _resident((FFN_HIDDEN, D_MODEL)),
                  _resident((1, D_MODEL))],
        out_specs=pl.BlockSpec((tm, D_MODEL), row),
        out_shape=jax.ShapeDtypeStruct((t, D_MODEL), F32),
        compiler_params=_params(1),
        name="ffn",
    )(x2, g, wi, wd, gf)


def _pad_rows(a, rows):
    return jnp.pad(a, ((0, rows - a.shape[0]), (0, 0)))


def _pad_cols(a, cols):
    return jnp.pad(a, ((0, 0), (0, cols - a.shape[1])))


def _column_select_matrix():
    j = np.arange(LANES)[:, None]
    blk = np.arange(SSD_HPG * CHUNK)[None, :] // CHUNK
    e = (j == blk).astype(np.float32)
    return jnp.asarray(np.concatenate([e, e], axis=0), dtype=BF16)


def _select_matrix(slot):
    j = np.arange(LANES)[:, None]
    ch = np.arange(SSD_INNER)[None, :]
    e = (j == slot * SSD_HEADS + ch // SSD_HEAD_DIM).astype(np.float32)
    return jnp.asarray(np.concatenate([e, e], axis=0), dtype=BF16)


def _layer(x2, mem2, bsz, seq, mem_len, norm_mix_g, w_in, conv_w, conv_b, dt_bias_fwd, dt_bias_bwd,
           a_log_fwd, a_log_bwd, d_skip, ssd_norm_g, attn_sink, w_branch_ssd, w_branch_attn, w_mix_out,
           norm_xattn_g, norm_mem_g, w_xattn_q, w_xattn_kv, w_xattn_out, norm_ffn_g, w_ffn_in, w_ffn_out,
           final_g):
    nc = seq // CHUNK
    s0 = SSD_INNER
    s1 = s0 + SSD_XBC
    s2 = s1 + 2 * SSD_HEADS
    s3 = s2 + ATTN_WIDTH
    s4 = s3 + ATTN_KV_WIDTH
    s5 = s4 + ATTN_KV_WIDTH
    w_packed = jnp.concatenate(
        [w_in[:, :s1], w_in[:, s2:s3], w_in[:, s3:s4], w_in[:, s4:s5], w_in[:, s5:],
         _pad_cols(w_in[:, s1:s2], LANES)], axis=1).astype(BF16)

    half = ATTN_HEAD_DIM // 2
    inv_freq = ROPE_THETA ** (-jnp.arange(half, dtype=F32) / half)
    ang = jnp.arange(seq, dtype=F32)[:, None] * inv_freq[None]
    cos = jnp.tile(jnp.cos(ang), (1, LANES // half))
    sin_signed = jnp.tile(jnp.concatenate([-jnp.sin(ang), jnp.sin(ang)], axis=1), (1, LANES // ATTN_HEAD_DIM))

    cw = _pad_rows(conv_w, 8)
    z, act, q, kv_attn, gates, dt = _in_proj(x2, norm_mix_g[None], w_packed, cos, sin_signed, cw, conv_b[None], seq)

    hp = _pad_rows(jnp.stack([_pad_cols(jnp.concatenate([dt_bias_fwd, dt_bias_bwd])[None], LANES)[0],
                              _pad_cols(jnp.concatenate([a_log_fwd, a_log_bwd])[None], LANES)[0]]), 8)
    e2f = _select_matrix(0)
    e2b = _select_matrix(1)
    hb = _ssd_bwd(act, dt, hp, e2b, bsz, nc)
    dskip = jnp.repeat(d_skip, SSD_HEAD_DIM)[None]
    y = _ssd_main(act, dt, z, hb, hp, dskip, ssd_norm_g[None], e2f, _column_select_matrix(), bsz, nc)

    sink_rows = _pad_rows(jnp.repeat(attn_sink * LOG2_E, ATTN_BLOCK).reshape(ATTN_KV_HEADS, ATTN_GQA * ATTN_BLOCK), 8)
    attn = _swa(q, kv_attn, sink_rows, bsz, seq // ATTN_BLOCK)

    kv = _mem_kv(mem2, norm_mem_g[None], w_xattn_kv.astype(BF16))
    x2 = _merge_xattn(x2, y, attn, gates, kv, w_branch_ssd.astype(BF16), w_branch_attn.astype(BF16),
                      w_mix_out.astype(BF16), norm_xattn_g[None], w_xattn_q.astype(BF16),
                      w_xattn_out.astype(BF16), seq, mem_len)

    return _ffn(x2, norm_ffn_g[None], w_ffn_in.astype(BF16), w_ffn_out.astype(BF16), final_g)


def kernel(x, mem, norm_mix_g, w_in, conv_w, conv_b, dt_bias_fwd, dt_bias_bwd, a_log_fwd, a_log_bwd, d_skip,
           ssd_norm_g, attn_sink, w_branch_ssd, w_branch_attn, w_mix_out, norm_xattn_g, norm_mem_g, w_xattn_q,
           w_xattn_kv, w_xattn_out, norm_ffn_g, w_ffn_in, w_ffn_out, norm_final_g):
    bsz, seq, _ = x.shape
    mem_len = mem.shape[1]
    assert w_in.shape[0] == 1, "single-layer stack expected"
    layer = 0
    x2 = x.reshape(bsz * seq, D_MODEL)
    mem2 = mem.reshape(bsz * mem_len, D_MODEL)
    out = _layer(x2, mem2, bsz, seq, mem_len, norm_mix_g[layer], w_in[layer], conv_w[layer], conv_b[layer],
                 dt_bias_fwd[layer], dt_bias_bwd[layer], a_log_fwd[layer], a_log_bwd[layer], d_skip[layer],
                 ssd_norm_g[layer], attn_sink[layer], w_branch_ssd[layer], w_branch_attn[layer],
                 w_mix_out[layer], norm_xattn_g[layer], norm_mem_g[layer], w_xattn_q[layer],
                 w_xattn_kv[layer], w_xattn_out[layer], norm_ffn_g[layer], w_ffn_in[layer],
                 w_ffn_out[layer], norm_final_g[None])
    return out.reshape(bsz, seq, D_MODEL)
```

```python
import functools

import jax
import jax.numpy as jnp
import numpy as np
from jax import lax
from jax.experimental import pallas as pl
from jax.experimental.pallas import tpu as pltpu

F32 = jnp.float32
BF16 = jnp.bfloat16

D_MODEL = 1024
EPS = 1e-6
SSD_INNER = 2048
SSD_HEAD_DIM = 64
SSD_HEADS = 32
SSD_GROUPS = 4
SSD_HPG = 8
SSD_STATE = 128
SSD_CONV = 5
CHUNK = 128
BWD_CHUNKS = 8
MAIN_CHUNKS = 2
DECAY_ROWS = 16
WIDE_ROWS = 4 * CHUNK + DECAY_ROWS
SSD_BC = SSD_GROUPS * SSD_STATE
SSD_XBC = SSD_INNER + 2 * SSD_BC
GROUP_W = SSD_HPG * SSD_HEAD_DIM
ATTN_HEAD_DIM = 64
ATTN_KV_HEADS = 4
ATTN_GQA = 4
ATTN_WIDTH = 1024
ATTN_KV_WIDTH = 256
ATTN_BLOCK = 128
SWA_BLOCKS = 4
ROPE_THETA = 10000.0
XATTN_HEADS = 4
XATTN_HEAD_DIM = 256
FFN_HIDDEN = 2816

LANES = 128
NEG_BIG = -1e30
LOG2_E = 1.4426950408889634

OFF_Z = 0
OFF_XBC = OFF_Z + SSD_INNER
OFF_Q = OFF_XBC + SSD_XBC
OFF_K = OFF_Q + ATTN_WIDTH
OFF_V = OFF_K + ATTN_KV_WIDTH
OFF_G = OFF_V + ATTN_KV_WIDTH
OFF_DT = OFF_G + 2 * D_MODEL
IN_PACKED = OFF_DT + LANES

VMEM_LIMIT = 56 * 1024 * 1024


def _params(n_axes, flags=None):
    return pltpu.CompilerParams(dimension_semantics=("arbitrary",) * n_axes,
                                vmem_limit_bytes=VMEM_LIMIT, flags=flags)


def _resident(shape):
    nd = len(shape)
    return pl.BlockSpec(shape, lambda *_: (0,) * nd, pipeline_mode=pl.Buffered(1))


def _rms(x, g):
    return x * lax.rsqrt(jnp.mean(x * x, axis=-1, keepdims=True) + EPS) * g


def _silu(x):
    h = 0.5 * x
    return h + h * jnp.tanh(h)


def _dot(a, b):
    return jnp.dot(a, b, preferred_element_type=F32)


def _dot_nt(a, b):
    return lax.dot_general(a, b, (((1,), (1,)), ((), ())), preferred_element_type=F32)


def _rotary(u, cos, sin_signed):
    n = u.shape[1]
    lane = lax.broadcasted_iota(jnp.int32, u.shape, 1)
    first_half = (lane % ATTN_HEAD_DIM) < (ATTN_HEAD_DIM // 2)
    partner = jnp.where(first_half, pltpu.roll(u, n - ATTN_HEAD_DIM // 2, axis=1),
                        pltpu.roll(u, ATTN_HEAD_DIM // 2, axis=1))
    reps = n // LANES
    cos_t = jnp.concatenate([cos] * reps, axis=1) if reps > 1 else cos
    sin_t = jnp.concatenate([sin_signed] * reps, axis=1) if reps > 1 else sin_signed
    return u * cos_t + partner * sin_t


def _in_proj_kernel(x_ref, xp_ref, xn_ref, g_ref, w_ref, cos_ref, sin_ref, cw_ref, cb_ref,
                    z_ref, act_ref, q_ref, kv_ref, gate_ref, dt_ref, ext_ref, *, per_seq):
    i = pl.program_id(0)
    tm = x_ref.shape[0]
    g = g_ref[...]
    h_ext = _rms(jnp.concatenate([x_ref[...], xp_ref[...], xn_ref[...]], axis=0), g).astype(BF16)
    step = 256
    rblk = 128
    row_blocks = [slice(r, r + rblk) for r in range(0, tm, rblk)]
    has_prev = (i % per_seq) > 0
    has_next = (i % per_seq) < per_seq - 1

    def proj(rs, off, width):
        return _dot(h_ext[rs], w_ref[:, off:off + width])

    def xbc_proj(c):
        cs = slice(c, c + step)
        for rs in row_blocks[:-1]:
            ext_ref[8 + rs.start:8 + rs.stop, cs] = proj(rs, OFF_XBC + c, step)
        last = row_blocks[-1]
        u = proj(slice(last.start, tm + 16), OFF_XBC + c, step)
        ext_ref[8 + last.start:8 + tm, cs] = u[:rblk]
        ext_ref[0:8, cs] = jnp.where(has_prev, u[rblk:rblk + 8], 0.0)
        ext_ref[8 + tm:16 + tm, cs] = jnp.where(has_next, u[rblk + 8:], 0.0)

    assert SSD_CONV == 5

    def conv_silu(c):
        cs = slice(c, c + step)
        e = ext_ref[:, cs]
        rows = e.shape[0]
        tap = lambda k: cw_ref[k:k + 1, cs] * e
        down = lambda a: pltpu.roll(a, 1, axis=0)
        up = lambda a: pltpu.roll(a, rows - 1, axis=0)
        left = down(down(tap(0)) + tap(1)) + tap(2)
        right = up(up(tap(4)) + tap(3))
        acc = (left + right)[8:8 + tm] + cb_ref[:, cs]
        act_ref[:, cs] = _silu(acc).astype(BF16)

    def z_proj(c):
        for rs in row_blocks:
            z_ref[rs, c:c + step] = proj(rs, OFF_Z + c, step).astype(BF16)

    def q_proj(c):
        for rs in row_blocks:
            q = _rotary(proj(rs, OFF_Q + c, step), cos_ref[rs, :], sin_ref[rs, :])
            q_ref[rs, c:c + step] = (q * (ATTN_HEAD_DIM ** -0.5 * LOG2_E)).astype(BF16)

    def kv_proj(_):
        for rs in row_blocks:
            kv = proj(rs, OFF_K, 2 * ATTN_KV_WIDTH)
            kv_ref[rs, :ATTN_KV_WIDTH] = _rotary(kv[:, :ATTN_KV_WIDTH], cos_ref[rs, :], sin_ref[rs, :]).astype(BF16)
            kv_ref[rs, ATTN_KV_WIDTH:] = kv[:, ATTN_KV_WIDTH:].astype(BF16)
            dt_ref[rs, :] = proj(rs, OFF_DT, LANES)

    def gate_proj(c):
        for rs in row_blocks:
            gate_ref[rs, c:c + step] = proj(rs, OFF_G + c, step).astype(BF16)

    mxu_tasks = ([(z_proj, c) for c in range(0, SSD_INNER, step)]
                 + [(q_proj, c) for c in range(0, ATTN_WIDTH, step)] + [(kv_proj, 0)]
                 + [(gate_proj, c) for c in range(0, 2 * D_MODEL, step)])
    conv_chunks = list(range(0, SSD_XBC, step))
    xbc_proj(conv_chunks[0])
    for j, c in enumerate(conv_chunks):
        if j + 1 < len(conv_chunks):
            xbc_proj(conv_chunks[j + 1])
        for fn, arg in mxu_tasks[2 * j:2 * j + 2]:
            fn(arg)
        conv_silu(c)
    for fn, arg in mxu_tasks[2 * len(conv_chunks):]:
        fn(arg)


def _in_proj(x2, g, w_packed, cos, sin_signed, conv_w, conv_b, seq):
    t = x2.shape[0]
    tm = 512
    per_seq = seq // tm
    halo_per_tile = tm // 8
    last_halo = t // 8 - 1
    row = lambda i: (i, 0)
    pos = lambda i: (i % per_seq, 0)
    outs = [(SSD_INNER, BF16), (SSD_XBC, BF16), (ATTN_WIDTH, BF16), (2 * ATTN_KV_WIDTH, BF16),
            (2 * D_MODEL, BF16), (LANES, F32)]
    return pl.pallas_call(
        functools.partial(_in_proj_kernel, per_seq=per_seq),
        grid=(t // tm,),
        in_specs=[pl.BlockSpec((tm, D_MODEL), row),
                  pl.BlockSpec((8, D_MODEL), lambda i: (jnp.maximum(i * halo_per_tile - 1, 0), 0)),
                  pl.BlockSpec((8, D_MODEL), lambda i: (jnp.minimum((i + 1) * halo_per_tile, last_halo), 0)),
                  _resident((1, D_MODEL)),
                  _resident((D_MODEL, IN_PACKED)),
                  pl.BlockSpec((tm, LANES), pos),
                  pl.BlockSpec((tm, LANES), pos),
                  _resident((8, SSD_XBC)),
                  _resident((1, SSD_XBC))],
        out_specs=[pl.BlockSpec((tm, w), row) for w, _ in outs],
        out_shape=[jax.ShapeDtypeStruct((t, w), d) for w, d in outs],
        scratch_shapes=[pltpu.VMEM((tm + 16, SSD_XBC), F32)],
        compiler_params=_params(1),
        name="in_proj",
    )(x2, x2, x2, g, w_packed, cos, sin_signed, conv_w, conv_b)


def _cumsum_rows(a):
    row = lax.broadcasted_iota(jnp.int32, a.shape, 0)
    v = a
    k = 1
    while k < a.shape[0]:
        v = v + jnp.where(row >= k, pltpu.roll(v, k, axis=0), 0.0)
        k *= 2
    return v


def _expand(v, e2_ref):
    hi = v.astype(BF16)
    lo = (v - hi.astype(F32)).astype(BF16)
    return _dot(jnp.concatenate([hi, lo], axis=1), e2_ref[...])


def _softplus(x):
    return jnp.maximum(x, 0.0) + jnp.log1p(jnp.exp(-jnp.abs(x)))


def _head_scalars(dt_raw, hp_ref):
    dtv = _softplus(dt_raw + hp_ref[0:1, :])
    a = dtv * (-LOG2_E * jnp.exp(hp_ref[1:2, :]))
    return dtv, a


def _chunk_scalars(dt_raw, hp_ref):
    dtv, a = _head_scalars(dt_raw, hp_ref)
    incl = _cumsum_rows(a)
    return dtv, incl, incl - a


def _ssd_bwd_kernel(act_ref, dt_ref, hp_ref, e2b_ref, hb_ref, state_ref):
    c = pl.program_id(1)

    @pl.when(c == 0)
    def _():
        state_ref[...] = jnp.zeros_like(state_ref)

    scales, decays = [], []
    for j in range(BWD_CHUNKS):
        dtv, incl, excl = _chunk_scalars(dt_ref[j * CHUNK:(j + 1) * CHUNK, :], hp_ref)
        scales.append(jnp.exp2(excl) * dtv)
        decays.append(jnp.exp2(incl[CHUNK - DECAY_ROWS:, :]))
    wide = _expand(jnp.concatenate(scales + decays, axis=0), e2b_ref)

    for j in reversed(range(BWD_CHUNKS)):
        rows = slice(j * CHUNK, (j + 1) * CHUNK)
        hb_ref[j] = state_ref[...].astype(BF16)
        xs = act_ref[rows, :SSD_INNER].astype(F32)
        bm = act_ref[rows, SSD_INNER:].astype(F32)
        xw = (xs * wide[rows]).astype(BF16)
        last = BWD_CHUNKS * CHUNK + (j + 1) * DECAY_ROWS - 1
        decay = wide[last:last + 1, :]
        for g in range(SSD_GROUPS):
            gs = slice(g * GROUP_W, (g + 1) * GROUP_W)
            bt = bm[:, g * SSD_STATE:(g + 1) * SSD_STATE].T.astype(BF16)
            state_ref[:, gs] = state_ref[:, gs] * decay[:, gs] + _dot(bt, xw[:, gs])


def _ssd_bwd(act, dt, hp, e2b, bsz, nc):
    width = SSD_INNER + SSD_BC
    steps = nc // BWD_CHUNKS
    block_of = lambda b, c: b * steps + (steps - 1 - c)
    return pl.pallas_call(
        _ssd_bwd_kernel,
        grid=(bsz, steps),
        in_specs=[pl.BlockSpec((BWD_CHUNKS * CHUNK, width), lambda b, c: (block_of(b, c), 0)),
                  pl.BlockSpec((BWD_CHUNKS * CHUNK, LANES), lambda b, c: (block_of(b, c), 0)),
                  _resident((8, LANES)),
                  _resident((2 * LANES, SSD_INNER))],
        out_specs=pl.BlockSpec((BWD_CHUNKS, SSD_STATE, SSD_INNER), lambda b, c: (block_of(b, c), 0, 0)),
        out_shape=jax.ShapeDtypeStruct((bsz * nc, SSD_STATE, SSD_INNER), BF16),
        scratch_shapes=[pltpu.VMEM((SSD_STATE, SSD_INNER), F32)],
        compiler_params=_params(2),
        name="ssd_bwd",
    )(act, dt, hp, e2b)


def _ssd_main_kernel(act_ref, dt_ref, z_ref, hb_ref, hp_ref, dskip_ref, ng_ref, e2f_ref, sel_ref,
                     y_ref, state_ref, ybuf_ref):
    c = pl.program_id(1)

    @pl.when(c == 0)
    def _():
        state_ref[...] = jnp.zeros_like(state_ref)

    lane = lax.broadcasted_iota(jnp.int32, (CHUNK, LANES), 1)
    row = lax.broadcasted_iota(jnp.int32, (CHUNK, CHUNK), 0)
    col = lax.broadcasted_iota(jnp.int32, (CHUNK, CHUNK), 1)
    groups = range(SSD_GROUPS)
    to_fwd_slot = lambda v: pltpu.roll(v, LANES - SSD_HEADS, axis=1)
    prep, wide_in, col_in = [], [], []
    for j in range(MAIN_CHUNKS):
        rows = slice(j * CHUNK, (j + 1) * CHUNK)
        dtv, incl, excl = _chunk_scalars(dt_ref[rows, :], hp_ref)
        tot = incl[CHUNK - 1:CHUNK, :]
        cbs = [_dot_nt(act_ref[rows, SSD_INNER + SSD_BC + g * SSD_STATE:SSD_INNER + SSD_BC + (g + 1) * SSD_STATE],
                       act_ref[rows, SSD_INNER + g * SSD_STATE:SSD_INNER + (g + 1) * SSD_STATE]) for g in groups]
        lane_group = (lane % SSD_HEADS) // SSD_HPG
        cb_diag = jnp.zeros_like(dtv)
        for g in groups:
            dg = jnp.sum(jnp.where(row == col, cbs[g], 0.0), axis=1, keepdims=True)
            cb_diag = jnp.where(lane_group == g, dg, cb_diag)
        wide_in += [jnp.exp2(incl),
                    jnp.exp2(tot - incl) * dtv,
                    to_fwd_slot(jnp.exp2(tot - excl)),
                    to_fwd_slot(cb_diag * dtv),
                    jnp.exp2(incl[CHUNK - DECAY_ROWS:, :])]
        log_dt = jnp.log2(dtv)
        key_f = (incl - log_dt).T
        key_b = (excl + log_dt).T
        vmix = jnp.where(lane < SSD_HEADS, incl, excl)
        hi = vmix.astype(BF16).astype(F32)
        lo = vmix - hi
        for shift in [d * SSD_HEADS + g * SSD_HPG for d in range(2) for g in groups]:
            sh = (LANES - shift) % LANES
            col_in.append(jnp.concatenate([pltpu.roll(hi, sh, axis=1) if sh else hi,
                                           pltpu.roll(lo, sh, axis=1) if sh else lo], axis=1).astype(BF16))
        prep.append((cbs, key_f, key_b))
    wide = _expand(jnp.concatenate(wide_in, axis=0), e2f_ref)
    cols = _dot(jnp.concatenate(col_in, axis=0), sel_ref[...])

    for j in range(MAIN_CHUNKS):
        w0 = j * WIDE_ROWS
        c0 = j * 2 * SSD_GROUPS * CHUNK
        _ssd_main_chunk(j, prep[j], wide[w0:w0 + WIDE_ROWS], cols[c0:c0 + 2 * SSD_GROUPS * CHUNK],
                        act_ref, z_ref, hb_ref, dskip_ref, ng_ref, y_ref, state_ref, ybuf_ref.at[j])


def _ssd_main_chunk(j, prep, wide, cols, act_ref, z_ref, hb_ref, dskip_ref, ng_ref, y_ref, state_ref, ybuf_ref):
    rows = slice(j * CHUNK, (j + 1) * CHUNK)
    cbs, key_f, key_b = prep
    scale_f = wide[0:CHUNK]
    state_scale = wide[CHUNK:2 * CHUNK]
    scale_b = wide[2 * CHUNK:3 * CHUNK]
    self_b = wide[3 * CHUNK:4 * CHUNK]
    decay = wide[WIDE_ROWS - 1:WIDE_ROWS, :]
    xs16 = act_ref[rows, :SSD_INNER]
    xs = xs16.astype(F32)
    bm16 = act_ref[rows, SSD_INNER:SSD_INNER + SSD_BC]
    cm16 = act_ref[rows, SSD_INNER + SSD_BC:]

    row = lax.broadcasted_iota(jnp.int32, (CHUNK, CHUNK), 0)
    col = lax.broadcasted_iota(jnp.int32, (CHUNK, CHUNK), 1)
    lower = row >= col
    first_head = lax.broadcasted_iota(jnp.int32, (CHUNK, 2 * SSD_HEAD_DIM), 1) < SSD_HEAD_DIM

    groups = range(SSD_GROUPS)
    ns = [slice(g * SSD_STATE, (g + 1) * SSD_STATE) for g in groups]
    gs = [slice(g * GROUP_W, (g + 1) * GROUP_W) for g in groups]
    xw = (xs * state_scale).astype(BF16)
    st = [state_ref[:, gs[g]] for g in groups]
    off_f = [_dot(cm16[:, ns[g]], st[g].astype(BF16)) for g in groups]
    off_b = [_dot(cm16[:, ns[g]], hb_ref[j, :, gs[g]]) for g in groups]
    for g in groups:
        bt = bm16[:, ns[g]].astype(F32).T.astype(BF16)
        state_ref[:, gs[g]] = st[g] * decay[:, gs[g]] + _dot(bt, xw[:, gs[g]])

    for g in groups:
        col_f = cols[g * CHUNK:(g + 1) * CHUNK]
        col_b = cols[(SSD_GROUPS + g) * CHUNK:(SSD_GROUPS + g + 1) * CHUNK]
        for pair in range(SSD_HPG // 2):
            ms = []
            for r in (2 * pair, 2 * pair + 1):
                h = g * SSD_HPG + r
                hb_col = SSD_HEADS + h
                rs = slice(r * CHUNK, (r + 1) * CHUNK)
                seg_f = col_f[:, rs] - key_f[h:h + 1, :]
                seg_b = key_b[hb_col:hb_col + 1, :] - col_b[:, rs]
                ms.append((cbs[g] * jnp.exp2(jnp.where(lower, seg_f, seg_b))).astype(BF16))
            ps = slice((g * SSD_HPG + 2 * pair) * SSD_HEAD_DIM, (g * SSD_HPG + 2 * pair + 2) * SSD_HEAD_DIM)
            x2 = xs16[:, ps]
            zero = jnp.zeros_like(x2)
            rhs = jnp.concatenate([jnp.where(first_head, x2, zero), jnp.where(first_head, zero, x2)], axis=0)
            ybuf_ref[:, ps] = _dot(jnp.concatenate(ms, axis=1), rhs)

    zf = z_ref[rows, :].astype(F32)
    zgate = _silu(zf)
    for g in groups:
        y = (ybuf_ref[:, gs[g]] + scale_f[:, gs[g]] * off_f[g] + scale_b[:, gs[g]] * off_b[g]
             + (dskip_ref[:, gs[g]] + self_b[:, gs[g]]) * xs[:, gs[g]])
        v = y * zgate[:, gs[g]]
        v = v * lax.rsqrt(jnp.mean(v * v, axis=-1, keepdims=True) + EPS)
        y_ref[rows, gs[g]] = (v * ng_ref[:, gs[g]]).astype(BF16)


def _ssd_main(act, dt, z, hb, hp, dskip, norm_g, e2f, sel, bsz, nc):
    steps = nc // MAIN_CHUNKS
    rows = MAIN_CHUNKS * CHUNK
    block_of = lambda b, c: b * steps + c
    return pl.pallas_call(
        _ssd_main_kernel,
        grid=(bsz, steps),
        in_specs=[pl.BlockSpec((rows, SSD_XBC), lambda b, c: (block_of(b, c), 0)),
                  pl.BlockSpec((rows, LANES), lambda b, c: (block_of(b, c), 0)),
                  pl.BlockSpec((rows, SSD_INNER), lambda b, c: (block_of(b, c), 0)),
                  pl.BlockSpec((MAIN_CHUNKS, SSD_STATE, SSD_INNER), lambda b, c: (block_of(b, c), 0, 0)),
                  _resident((8, LANES)),
                  _resident((1, SSD_INNER)),
                  _resident((1, SSD_INNER)),
                  _resident((2 * LANES, SSD_INNER)),
                  _resident((2 * LANES, SSD_HPG * CHUNK))],
        out_specs=pl.BlockSpec((rows, SSD_INNER), lambda b, c: (block_of(b, c), 0)),
        out_shape=jax.ShapeDtypeStruct((bsz * nc * CHUNK, SSD_INNER), BF16),
        scratch_shapes=[pltpu.VMEM((SSD_STATE, SSD_INNER), F32),
                        pltpu.VMEM((MAIN_CHUNKS, CHUNK, SSD_INNER), F32)],
        compiler_params=_params(2),
        name="ssd_main",
    )(act, dt, z, hb, hp, dskip, norm_g, e2f, sel)


def _swa_kernel(q_ref, kvp_ref, kvc_ref, kvn_ref, sink_ref, o_ref):
    n = pl.program_id(1)
    last = pl.num_programs(1) - 1
    kj = lax.broadcasted_iota(jnp.int32, (3 * ATTN_BLOCK, ATTN_BLOCK), 0)
    qi = lax.broadcasted_iota(jnp.int32, (3 * ATTN_BLOCK, ATTN_BLOCK), 1)
    rel = kj - qi
    band = (rel >= 0) & (rel <= 2 * ATTN_BLOCK)
    kvcat = jnp.concatenate([kvp_ref[...], kvc_ref[...], kvn_ref[...]], axis=0)
    kcat = kvcat[:, :ATTN_KV_WIDTH]
    vcat_t = kvcat[:, ATTN_KV_WIDTH:].astype(F32).T.astype(BF16)
    scores = []
    for j in range(SWA_BLOCKS):
        valid = band
        if j == 0:
            valid = valid & ((kj >= ATTN_BLOCK) | (n > 0))
        if j == SWA_BLOCKS - 1:
            valid = valid & ((kj < 2 * ATTN_BLOCK) | (n < last))
        bias = jnp.concatenate([jnp.where(valid, 0.0, NEG_BIG)] * ATTN_GQA, axis=1)
        edge_bias = (bias[:ATTN_BLOCK], bias[2 * ATTN_BLOCK:])
        keys = slice(j * ATTN_BLOCK, (j + 3) * ATTN_BLOCK)
        qrows = slice(j * ATTN_BLOCK, (j + 1) * ATTN_BLOCK)
        for kv in range(ATTN_KV_HEADS):
            ks = slice(kv * ATTN_HEAD_DIM, (kv + 1) * ATTN_HEAD_DIM)
            q_stack = jnp.concatenate(
                [q_ref[qrows, (kv * ATTN_GQA + r) * ATTN_HEAD_DIM:(kv * ATTN_GQA + r + 1) * ATTN_HEAD_DIM]
                 for r in range(ATTN_GQA)], axis=0)
            s = _dot_nt(kcat[keys, ks], q_stack)
            scores.append(jnp.concatenate([s[:ATTN_BLOCK] + edge_bias[0], s[ATTN_BLOCK:2 * ATTN_BLOCK],
                                           s[2 * ATTN_BLOCK:] + edge_bias[1]], axis=0))
    probs = []
    for idx, s in enumerate(scores):
        kv = idx % ATTN_KV_HEADS
        sink = sink_ref[kv:kv + 1, :]
        m = jnp.maximum(jnp.max(s, axis=0, keepdims=True), sink)
        p = jnp.exp2(s - m)
        denom = jnp.sum(p, axis=0, keepdims=True) + jnp.exp2(sink - m)
        probs.append((p.astype(BF16), 1.0 / denom))
    for j in range(SWA_BLOCKS):
        keys = slice(j * ATTN_BLOCK, (j + 3) * ATTN_BLOCK)
        outs = []
        for kv in range(ATTN_KV_HEADS):
            ks = slice(kv * ATTN_HEAD_DIM, (kv + 1) * ATTN_HEAD_DIM)
            p16, inv = probs[j * ATTN_KV_HEADS + kv]
            o_t = _dot(vcat_t[ks, keys], p16) * inv
            outs.extend(o_t[:, r * ATTN_BLOCK:(r + 1) * ATTN_BLOCK] for r in range(ATTN_GQA))
        o_ref[j * ATTN_BLOCK:(j + 1) * ATTN_BLOCK, :] = jnp.concatenate(outs, axis=0).T.astype(BF16)


def _swa(q, kv, sink_rows, bsz, nb):
    steps = nb // SWA_BLOCKS
    rows = SWA_BLOCKS * ATTN_BLOCK
    cur = lambda b, n: (b * steps + n, 0)
    prev = lambda b, n: (b * nb + jnp.maximum(SWA_BLOCKS * n - 1, 0), 0)
    nxt = lambda b, n: (b * nb + jnp.minimum(SWA_BLOCKS * (n + 1), nb - 1), 0)
    return pl.pallas_call(
        _swa_kernel,
        grid=(bsz, steps),
        in_specs=[pl.BlockSpec((rows, ATTN_WIDTH), cur),
                  pl.BlockSpec((ATTN_BLOCK, 2 * ATTN_KV_WIDTH), prev),
                  pl.BlockSpec((rows, 2 * ATTN_KV_WIDTH), cur),
                  pl.BlockSpec((ATTN_BLOCK, 2 * ATTN_KV_WIDTH), nxt),
                  _resident((8, ATTN_GQA * ATTN_BLOCK))],
        out_specs=pl.BlockSpec((rows, ATTN_WIDTH), cur),
        out_shape=jax.ShapeDtypeStruct(q.shape, BF16),
        compiler_params=_params(2),
        name="swa",
    )(q, kv, kv, kv, sink_rows)


def _mem_kv_kernel(m_ref, g_ref, w_ref, o_ref):
    h = _rms(m_ref[...], g_ref[...]).astype(BF16)
    o_ref[...] = _dot(h, w_ref[...]).astype(BF16)


def _mem_kv(mem2, g, w_kv):
    t = mem2.shape[0]
    tm = 256
    row = lambda i: (i, 0)
    return pl.pallas_call(
        _mem_kv_kernel,
        grid=(t // tm,),
        in_specs=[pl.BlockSpec((tm, D_MODEL), row),
                  _resident((1, D_MODEL)),
                  _resident((D_MODEL, 2 * D_MODEL))],
        out_specs=pl.BlockSpec((tm, 2 * D_MODEL), row),
        out_shape=jax.ShapeDtypeStruct((t, 2 * D_MODEL), BF16),
        compiler_params=_params(1),
        name="mem_kv",
    )(mem2, g, w_kv)


def _merge_xattn_kernel(x_ref, y_ref, a_ref, gate_ref, kv_ref, ws_ref, wa_ref, wo_ref, g_ref, wq_ref, wxo_ref,
                        o_ref, ctx_ref):
    bs = _dot(y_ref[...], ws_ref[...])
    ba = _dot(a_ref[...], wa_ref[...])
    gate = jax.nn.sigmoid(gate_ref[...].astype(F32))
    mix = (gate[:, :D_MODEL] * bs + gate[:, D_MODEL:] * ba).astype(BF16)
    x = x_ref[...] + _dot(mix, wo_ref[...])

    h = _rms(x, g_ref[...]).astype(BF16)
    q = (_dot(h, wq_ref[...]) * (XATTN_HEAD_DIM ** -0.5 * LOG2_E)).astype(BF16)
    heads = [slice(hd * XATTN_HEAD_DIM, (hd + 1) * XATTN_HEAD_DIM) for hd in range(XATTN_HEADS)]
    scores = [_dot_nt(q[:, ks], kv_ref[:, ks]) for ks in heads]
    probs = []
    for s in scores:
        p = jnp.exp2(s - jnp.max(s, axis=-1, keepdims=True))
        probs.append((p.astype(BF16), 1.0 / jnp.sum(p, axis=-1, keepdims=True)))
    for ks, (p16, inv) in zip(heads, probs):
        vs = slice(D_MODEL + ks.start, D_MODEL + ks.stop)
        ctx_ref[:, ks] = (_dot(p16, kv_ref[:, vs]) * inv).astype(BF16)
    o_ref[...] = x + _dot(ctx_ref[...], wxo_ref[...])


def _merge_xattn(x2, y, attn, gates, kv, ws, wa, wo, g, wq, wxo, seq, mem_len):
    t = x2.shape[0]
    tm = 512
    per_seq = seq // tm
    row = lambda i: (i, 0)
    return pl.pallas_call(
        _merge_xattn_kernel,
        grid=(t // tm,),
        in_specs=[pl.BlockSpec((tm, D_MODEL), row),
                  pl.BlockSpec((tm, SSD_INNER), row),
                  pl.BlockSpec((tm, ATTN_WIDTH), row),
                  pl.BlockSpec((tm, 2 * D_MODEL), row),
                  pl.BlockSpec((mem_len, 2 * D_MODEL), lambda i: (i // per_seq, 0)),
                  _resident((SSD_INNER, D_MODEL)),
                  _resident((ATTN_WIDTH, D_MODEL)),
                  _resident((D_MODEL, D_MODEL)),
                  _resident((1, D_MODEL)),
                  _resident((D_MODEL, D_MODEL)),
                  _resident((D_MODEL, D_MODEL))],
        out_specs=pl.BlockSpec((tm, D_MODEL), row),
        out_shape=jax.ShapeDtypeStruct((t, D_MODEL), F32),
        scratch_shapes=[pltpu.VMEM((tm, D_MODEL), BF16)],
        compiler_params=_params(1),
        name="merge_xattn",
    )(x2, y, attn, gates, kv, ws, wa, wo, g, wq, wxo)


MXU_TILE = 256
FFN_SPLIT = (FFN_HIDDEN // MXU_TILE + 1) // 2 * MXU_TILE


def _ffn_kernel(x_ref, g_ref, wi_ref, wd_ref, gf_ref, o_ref):
    x = x_ref[...]
    h = _rms(x, g_ref[...]).astype(BF16)
    acc = x
    for lo, hi in ((0, FFN_SPLIT), (FFN_SPLIT, FFN_HIDDEN)):
        gate = _dot(h, wi_ref[:, lo:hi])
        up = _dot(h, wi_ref[:, FFN_HIDDEN + lo:FFN_HIDDEN + hi])
        act = (_silu(gate) * up).astype(BF16)
        acc = acc + _dot(act, wd_ref[lo:hi, :])
    o_ref[...] = _rms(acc, gf_ref[...])


def _ffn(x2, g, wi, wd, gf):
    t = x2.shape[0]
    tm = 512
    row = lambda i: (i, 0)
    return pl.pallas_call(
        _ffn_kernel,
        grid=(t // tm,),
        in_specs=[pl.BlockSpec((tm, D_MODEL), row),
                  _resident((1, D_MODEL)),
                  _resident((D_MODEL, 2 * FFN_HIDDEN)),
                  _resident((FFN_HIDDEN, D_MODEL)),
                  _resident((1, D_MODEL))],
        out_specs=pl.BlockSpec((tm, D_MODEL), row),
        out_shape=jax.ShapeDtypeStruct((t, D_MODEL), F32),
        compiler_params=_params(1),
        name="ffn",
    )(x2, g, wi, wd, gf)


def _pad_rows(a, rows):
    return jnp.pad(a, ((0, rows - a.shape[0]), (0, 0)))


def _pad_cols(a, cols):
    return jnp.pad(a, ((0, 0), (0, cols - a.shape[1])))


def _column_select_matrix():
    j = np.arange(LANES)[:, None]
    blk = np.arange(SSD_HPG * CHUNK)[None, :] // CHUNK
    e = (j == blk).astype(np.float32)
    return jnp.asarray(np.concatenate([e, e], axis=0), dtype=BF16)


def _select_matrix(slot):
    j = np.arange(LANES)[:, None]
    ch = np.arange(SSD_INNER)[None, :]
    e = (j == slot * SSD_HEADS + ch // SSD_HEAD_DIM).astype(np.float32)
    return jnp.asarray(np.concatenate([e, e], axis=0), dtype=BF16)


def _layer(x2, mem2, bsz, seq, mem_len, norm_mix_g, w_in, conv_w, conv_b, dt_bias_fwd, dt_bias_bwd,
           a_log_fwd, a_log_bwd, d_skip, ssd_norm_g, attn_sink, w_branch_ssd, w_branch_attn, w_mix_out,
           norm_xattn_g, norm_mem_g, w_xattn_q, w_xattn_kv, w_xattn_out, norm_ffn_g, w_ffn_in, w_ffn_out,
           final_g):
    nc = seq // CHUNK
    s0 = SSD_INNER
    s1 = s0 + SSD_XBC
    s2 = s1 + 2 * SSD_HEADS
    s3 = s2 + ATTN_WIDTH
    s4 = s3 + ATTN_KV_WIDTH
    s5 = s4 + ATTN_KV_WIDTH
    w_packed = jnp.concatenate(
        [w_in[:, :s1], w_in[:, s2:s3], w_in[:, s3:s4], w_in[:, s4:s5], w_in[:, s5:],
         _pad_cols(w_in[:, s1:s2], LANES)], axis=1).astype(BF16)

    half = ATTN_HEAD_DIM // 2
    inv_freq = ROPE_THETA ** (-jnp.arange(half, dtype=F32) / half)
    ang = jnp.arange(seq, dtype=F32)[:, None] * inv_freq[None]
    cos = jnp.tile(jnp.cos(ang), (1, LANES // half))
    sin_signed = jnp.tile(jnp.concatenate([-jnp.sin(ang), jnp.sin(ang)], axis=1), (1, LANES // ATTN_HEAD_DIM))

    cw = _pad_rows(conv_w, 8)
    z, act, q, kv_attn, gates, dt = _in_proj(x2, norm_mix_g[None], w_packed, cos, sin_signed, cw, conv_b[None], seq)

    hp = _pad_rows(jnp.stack([_pad_cols(jnp.concatenate([dt_bias_fwd, dt_bias_bwd])[None], LANES)[0],
                              _pad_cols(jnp.concatenate([a_log_fwd, a_log_bwd])[None], LANES)[0]]), 8)
    e2f = _select_matrix(0)
    e2b = _select_matrix(1)
    hb = _ssd_bwd(act, dt, hp, e2b, bsz, nc)
    dskip = jnp.repeat(d_skip, SSD_HEAD_DIM)[None]
    y = _ssd_main(act, dt, z, hb, hp, dskip, ssd_norm_g[None], e2f, _column_select_matrix(), bsz, nc)

    sink_rows = _pad_rows(jnp.repeat(attn_sink * LOG2_E, ATTN_BLOCK).reshape(ATTN_KV_HEADS, ATTN_GQA * ATTN_BLOCK), 8)
    attn = _swa(q, kv_attn, sink_rows, bsz, seq // ATTN_BLOCK)

    kv = _mem_kv(mem2, norm_mem_g[None], w_xattn_kv.astype(BF16))
    x2 = _merge_xattn(x2, y, attn, gates, kv, w_branch_ssd.astype(BF16), w_branch_attn.astype(BF16),
                      w_mix_out.astype(BF16), norm_xattn_g[None], w_xattn_q.astype(BF16),
                      w_xattn_out.astype(BF16), seq, mem_len)

    return _ffn(x2, norm_ffn_g[None], w_ffn_in.astype(BF16), w_ffn_out.astype(BF16), final_g)


def kernel(x, mem, norm_mix_g, w_in, conv_w, conv_b, dt_bias_fwd, dt_bias_bwd, a_log_fwd, a_log_bwd, d_skip,
           ssd_norm_g, attn_sink, w_branch_ssd, w_branch_attn, w_mix_out, norm_xattn_g, norm_mem_g, w_xattn_q,
           w_xattn_kv, w_xattn_out, norm_ffn_g, w_ffn_in, w_ffn_out, norm_final_g):
    bsz, seq, _ = x.shape
    mem_len = mem.shape[1]
    assert w_in.shape[0] == 1, "single-layer stack expected"
    layer = 0
    x2 = x.reshape(bsz * seq, D_MODEL)
    mem2 = mem.reshape(bsz * mem_len, D_MODEL)
    out = _layer(x2, mem2, bsz, seq, mem_len, norm_mix_g[layer], w_in[layer], conv_w[layer], conv_b[layer],
                 dt_bias_fwd[layer], dt_bias_bwd[layer], a_log_fwd[layer], a_log_bwd[layer], d_skip[layer],
                 ssd_norm_g[layer], attn_sink[layer], w_branch_ssd[layer], w_branch_attn[layer],
                 w_mix_out[layer], norm_xattn_g[layer], norm_mem_g[layer], w_xattn_q[layer],
                 w_xattn_kv[layer], w_xattn_out[layer], norm_ffn_g[layer], w_ffn_in[layer],
                 w_ffn_out[layer], norm_final_g[None])
    return out.reshape(bsz, seq, D_MODEL)
```

```python
import functools

import jax
import jax.numpy as jnp
import numpy as np
from jax import lax
from jax.experimental import pallas as pl
from jax.experimental.pallas import tpu as pltpu

F32 = jnp.float32
BF16 = jnp.bfloat16

D_MODEL = 1024
EPS = 1e-6
SSD_INNER = 2048
SSD_HEAD_DIM = 64
SSD_HEADS = 32
SSD_GROUPS = 4
SSD_HPG = 8
SSD_STATE = 128
SSD_CONV = 5
CHUNK = 128
BWD_CHUNKS = 8
MAIN_CHUNKS = 2
DECAY_ROWS = 16
WIDE_ROWS = 4 * CHUNK + DECAY_ROWS
SSD_BC = SSD_GROUPS * SSD_STATE
SSD_XBC = SSD_INNER + 2 * SSD_BC
GROUP_W = SSD_HPG * SSD_HEAD_DIM
ATTN_HEAD_DIM = 64
ATTN_KV_HEADS = 4
ATTN_GQA = 4
ATTN_WIDTH = 1024
ATTN_KV_WIDTH = 256
ATTN_BLOCK = 128
SWA_BLOCKS = 4
ROPE_THETA = 10000.0
XATTN_HEADS = 4
XATTN_HEAD_DIM = 256
FFN_HIDDEN = 2816

LANES = 128
NEG_BIG = -1e30
LOG2_E = 1.4426950408889634

OFF_Z = 0
OFF_XBC = OFF_Z + SSD_INNER
OFF_DT = OFF_XBC + SSD_XBC
WIDTH_A = OFF_DT + LANES
START_B = OFF_DT + 2 * SSD_HEADS
OFF_Q = 0
OFF_K = OFF_Q + ATTN_WIDTH
OFF_G = OFF_K + 2 * ATTN_KV_WIDTH
WIDTH_B = OFF_G + 2 * D_MODEL

VMEM_LIMIT = 56 * 1024 * 1024


def _params(n_axes, flags=None):
    return pltpu.CompilerParams(dimension_semantics=("arbitrary",) * n_axes,
                                vmem_limit_bytes=VMEM_LIMIT, flags=flags)


def _resident(shape):
    nd = len(shape)
    return pl.BlockSpec(shape, lambda *_: (0,) * nd, pipeline_mode=pl.Buffered(1))


def _rms(x, g):
    return x * lax.rsqrt(jnp.mean(x * x, axis=-1, keepdims=True) + EPS) * g


def _silu(x):
    h = 0.5 * x
    return h + h * jnp.tanh(h)


def _dot(a, b):
    return jnp.dot(a, b, preferred_element_type=F32)


def _dot_nt(a, b):
    return lax.dot_general(a, b, (((1,), (1,)), ((), ())), preferred_element_type=F32)


def _rotary(u, cos, sin_signed):
    n = u.shape[1]
    lane = lax.broadcasted_iota(jnp.int32, u.shape, 1)
    first_half = (lane % ATTN_HEAD_DIM) < (ATTN_HEAD_DIM // 2)
    partner = jnp.where(first_half, pltpu.roll(u, n - ATTN_HEAD_DIM // 2, axis=1),
                        pltpu.roll(u, ATTN_HEAD_DIM // 2, axis=1))
    reps = n // LANES
    cos_t = jnp.concatenate([cos] * reps, axis=1) if reps > 1 else cos
    sin_t = jnp.concatenate([sin_signed] * reps, axis=1) if reps > 1 else sin_signed
    return u * cos_t + partner * sin_t


def _in_proj_kernel(x_ref, xp_ref, xn_ref, g_ref, wa_ref, wb_ref, cos_ref, sin_ref, cw_ref, cb_ref,
                    z_ref, act_ref, q_ref, kv_ref, gate_ref, dt_ref, ext_ref, *, per_seq):
    i = pl.program_id(0)
    tm = x_ref.shape[0]
    g = g_ref[...]
    h_ext = _rms(jnp.concatenate([x_ref[...], xp_ref[...], xn_ref[...]], axis=0), g).astype(BF16)
    step = 256
    rblk = 128
    row_blocks = [slice(r, r + rblk) for r in range(0, tm, rblk)]
    has_prev = (i % per_seq) > 0
    has_next = (i % per_seq) < per_seq - 1

    def proj(rs, w_ref, off, width):
        return _dot(h_ext[rs], w_ref[:, off:off + width])

    def xbc_proj(c):
        cs = slice(c, c + step)
        for rs in row_blocks[:-1]:
            ext_ref[8 + rs.start:8 + rs.stop, cs] = proj(rs, wa_ref, OFF_XBC + c, step)
        last = row_blocks[-1]
        u = proj(slice(last.start, tm + 16), wa_ref, OFF_XBC + c, step)
        ext_ref[8 + last.start:8 + tm, cs] = u[:rblk]
        ext_ref[0:8, cs] = jnp.where(has_prev, u[rblk:rblk + 8], 0.0)
        ext_ref[8 + tm:16 + tm, cs] = jnp.where(has_next, u[rblk + 8:], 0.0)

    assert SSD_CONV == 5

    def conv_silu(c):
        cs = slice(c, c + step)
        e = ext_ref[:, cs]
        rows = e.shape[0]
        tap = lambda k: cw_ref[k:k + 1, cs] * e
        down = lambda a: pltpu.roll(a, 1, axis=0)
        up = lambda a: pltpu.roll(a, rows - 1, axis=0)
        left = down(down(tap(0)) + tap(1)) + tap(2)
        right = up(up(tap(4)) + tap(3))
        acc = (left + right)[8:8 + tm] + cb_ref[:, cs]
        act_ref[:, cs] = _silu(acc).astype(BF16)

    def z_proj(c):
        for rs in row_blocks:
            z_ref[rs, c:c + step] = proj(rs, wa_ref, OFF_Z + c, step).astype(BF16)

    def q_proj(c):
        for rs in row_blocks:
            q = _rotary(proj(rs, wb_ref, OFF_Q + c, step), cos_ref[rs, :], sin_ref[rs, :])
            q_ref[rs, c:c + step] = (q * (ATTN_HEAD_DIM ** -0.5 * LOG2_E)).astype(BF16)

    def kv_proj(_):
        for rs in row_blocks:
            kv = proj(rs, wb_ref, OFF_K, 2 * ATTN_KV_WIDTH)
            kv_ref[rs, :ATTN_KV_WIDTH] = _rotary(kv[:, :ATTN_KV_WIDTH], cos_ref[rs, :], sin_ref[rs, :]).astype(BF16)
            kv_ref[rs, ATTN_KV_WIDTH:] = kv[:, ATTN_KV_WIDTH:].astype(BF16)
            dt_lane = lax.broadcasted_iota(jnp.int32, (rblk, LANES), 1)
            dt_ref[rs, :] = jnp.where(dt_lane < 2 * SSD_HEADS, proj(rs, wa_ref, OFF_DT, LANES), 0.0)

    def gate_proj(c):
        for rs in row_blocks:
            gate_ref[rs, c:c + step] = proj(rs, wb_ref, OFF_G + c, step).astype(BF16)

    mxu_tasks = ([(z_proj, c) for c in range(0, SSD_INNER, step)]
                 + [(q_proj, c) for c in range(0, ATTN_WIDTH, step)] + [(kv_proj, 0)]
                 + [(gate_proj, c) for c in range(0, 2 * D_MODEL, step)])
    conv_chunks = list(range(0, SSD_XBC, step))
    xbc_proj(conv_chunks[0])
    for j, c in enumerate(conv_chunks):
        if j + 1 < len(conv_chunks):
            xbc_proj(conv_chunks[j + 1])
        for fn, arg in mxu_tasks[2 * j:2 * j + 2]:
            fn(arg)
        conv_silu(c)
    for fn, arg in mxu_tasks[2 * len(conv_chunks):]:
        fn(arg)


def _in_proj(x2, g, w_a, w_b, cos, sin_signed, conv_w, conv_b, seq):
    t = x2.shape[0]
    tm = 512
    per_seq = seq // tm
    halo_per_tile = tm // 8
    last_halo = t // 8 - 1
    row = lambda i: (i, 0)
    pos = lambda i: (i % per_seq, 0)
    outs = [(SSD_INNER, BF16), (SSD_XBC, BF16), (ATTN_WIDTH, BF16), (2 * ATTN_KV_WIDTH, BF16),
            (2 * D_MODEL, BF16), (LANES, F32)]
    return pl.pallas_call(
        functools.partial(_in_proj_kernel, per_seq=per_seq),
        grid=(t // tm,),
        in_specs=[pl.BlockSpec((tm, D_MODEL), row),
                  pl.BlockSpec((8, D_MODEL), lambda i: (jnp.maximum(i * halo_per_tile - 1, 0), 0)),
                  pl.BlockSpec((8, D_MODEL), lambda i: (jnp.minimum((i + 1) * halo_per_tile, last_halo), 0)),
                  _resident((1, D_MODEL)),
                  _resident((D_MODEL, WIDTH_A)),
                  _resident((D_MODEL, WIDTH_B)),
                  pl.BlockSpec((tm, LANES), pos),
                  pl.BlockSpec((tm, LANES), pos),
                  _resident((8, SSD_XBC)),
                  _resident((1, SSD_XBC))],
        out_specs=[pl.BlockSpec((tm, w), row) for w, _ in outs],
        out_shape=[jax.ShapeDtypeStruct((t, w), d) for w, d in outs],
        scratch_shapes=[pltpu.VMEM((tm + 16, SSD_XBC), F32)],
        compiler_params=_params(1),
        name="in_proj",
    )(x2, x2, x2, g, w_a, w_b, cos, sin_signed, conv_w, conv_b)


def _cumsum_rows(a):
    row = lax.broadcasted_iota(jnp.int32, a.shape, 0)
    v = a
    k = 1
    while k < a.shape[0]:
        v = v + jnp.where(row >= k, pltpu.roll(v, k, axis=0), 0.0)
        k *= 2
    return v


def _expand(v, e2_ref):
    hi = v.astype(BF16)
    lo = (v - hi.astype(F32)).astype(BF16)
    return _dot(jnp.concatenate([hi, lo], axis=1), e2_ref[...])


def _softplus(x):
    return jnp.maximum(x, 0.0) + jnp.log1p(jnp.exp(-jnp.abs(x)))


def _head_scalars(dt_raw, hp_ref):
    dtv = _softplus(dt_raw + hp_ref[0:1, :])
    a = dtv * (-LOG2_E * jnp.exp(hp_ref[1:2, :]))
    return dtv, a


def _chunk_scalars(dt_raw, hp_ref):
    dtv, a = _head_scalars(dt_raw, hp_ref)
    incl = _cumsum_rows(a)
    return dtv, incl, incl - a


def _ssd_bwd_kernel(act_ref, dt_ref, hp_ref, e2b_ref, hb_ref, state_ref):
    c = pl.program_id(1)

    @pl.when(c == 0)
    def _():
        state_ref[...] = jnp.zeros_like(state_ref)

    scales, decays = [], []
    for j in range(BWD_CHUNKS):
        dtv, incl, excl = _chunk_scalars(dt_ref[j * CHUNK:(j + 1) * CHUNK, :], hp_ref)
        scales.append(jnp.exp2(excl) * dtv)
        decays.append(jnp.exp2(incl[CHUNK - DECAY_ROWS:, :]))
    wide = _expand(jnp.concatenate(scales + decays, axis=0), e2b_ref)

    for j in reversed(range(BWD_CHUNKS)):
        rows = slice(j * CHUNK, (j + 1) * CHUNK)
        hb_ref[j] = state_ref[...].astype(BF16)
        xs = act_ref[rows, :SSD_INNER].astype(F32)
        bm = act_ref[rows, SSD_INNER:].astype(F32)
        xw = (xs * wide[rows]).astype(BF16)
        last = BWD_CHUNKS * CHUNK + (j + 1) * DECAY_ROWS - 1
        decay = wide[last:last + 1, :]
        for g in range(SSD_GROUPS):
            gs = slice(g * GROUP_W, (g + 1) * GROUP_W)
            bt = bm[:, g * SSD_STATE:(g + 1) * SSD_STATE].T.astype(BF16)
            state_ref[:, gs] = state_ref[:, gs] * decay[:, gs] + _dot(bt, xw[:, gs])


def _ssd_bwd(act, dt, hp, e2b, bsz, nc):
    width = SSD_INNER + SSD_BC
    steps = nc // BWD_CHUNKS
    block_of = lambda b, c: b * steps + (steps - 1 - c)
    return pl.pallas_call(
        _ssd_bwd_kernel,
        grid=(bsz, steps),
        in_specs=[pl.BlockSpec((BWD_CHUNKS * CHUNK, width), lambda b, c: (block_of(b, c), 0)),
                  pl.BlockSpec((BWD_CHUNKS * CHUNK, LANES), lambda b, c: (block_of(b, c), 0)),
                  _resident((8, LANES)),
                  _resident((2 * LANES, SSD_INNER))],
        out_specs=pl.BlockSpec((BWD_CHUNKS, SSD_STATE, SSD_INNER), lambda b, c: (block_of(b, c), 0, 0)),
        out_shape=jax.ShapeDtypeStruct((bsz * nc, SSD_STATE, SSD_INNER), BF16),
        scratch_shapes=[pltpu.VMEM((SSD_STATE, SSD_INNER), F32)],
        compiler_params=_params(2),
        name="ssd_bwd",
    )(act, dt, hp, e2b)


def _ssd_main_kernel(act_ref, dt_ref, z_ref, hb_ref, hp_ref, dskip_ref, ng_ref, e2f_ref, sel_ref,
                     y_ref, state_ref, ybuf_ref):
    c = pl.program_id(1)

    @pl.when(c == 0)
    def _():
        state_ref[...] = jnp.zeros_like(state_ref)

    lane = lax.broadcasted_iota(jnp.int32, (CHUNK, LANES), 1)
    row = lax.broadcasted_iota(jnp.int32, (CHUNK, CHUNK), 0)
    col = lax.broadcasted_iota(jnp.int32, (CHUNK, CHUNK), 1)
    groups = range(SSD_GROUPS)
    to_fwd_slot = lambda v: pltpu.roll(v, LANES - SSD_HEADS, axis=1)
    prep, wide_in, col_in = [], [], []
    for j in range(MAIN_CHUNKS):
        rows = slice(j * CHUNK, (j + 1) * CHUNK)
        dtv, incl, excl = _chunk_scalars(dt_ref[rows, :], hp_ref)
        tot = incl[CHUNK - 1:CHUNK, :]
        cbs = [_dot_nt(act_ref[rows, SSD_INNER + SSD_BC + g * SSD_STATE:SSD_INNER + SSD_BC + (g + 1) * SSD_STATE],
                       act_ref[rows, SSD_INNER + g * SSD_STATE:SSD_INNER + (g + 1) * SSD_STATE]) for g in groups]
        lane_group = (lane % SSD_HEADS) // SSD_HPG
        cb_diag = jnp.zeros_like(dtv)
        for g in groups:
            dg = jnp.sum(jnp.where(row == col, cbs[g], 0.0), axis=1, keepdims=True)
            cb_diag = jnp.where(lane_group == g, dg, cb_diag)
        wide_in += [jnp.exp2(incl),
                    jnp.exp2(tot - incl) * dtv,
                    to_fwd_slot(jnp.exp2(tot - excl)),
                    to_fwd_slot(cb_diag * dtv),
                    jnp.exp2(incl[CHUNK - DECAY_ROWS:, :])]
        log_dt = jnp.log2(dtv)
        key_f = (incl - log_dt).T
        key_b = (excl + log_dt).T
        vmix = jnp.where(lane < SSD_HEADS, incl, excl)
        hi = vmix.astype(BF16).astype(F32)
        lo = vmix - hi
        for shift in [d * SSD_HEADS + g * SSD_HPG for d in range(2) for g in groups]:
            sh = (LANES - shift) % LANES
            col_in.append(jnp.concatenate([pltpu.roll(hi, sh, axis=1) if sh else hi,
                                           pltpu.roll(lo, sh, axis=1) if sh else lo], axis=1).astype(BF16))
        prep.append((cbs, key_f, key_b))
    wide = _expand(jnp.concatenate(wide_in, axis=0), e2f_ref)
    cols = _dot(jnp.concatenate(col_in, axis=0), sel_ref[...])

    for j in range(MAIN_CHUNKS):
        w0 = j * WIDE_ROWS
        c0 = j * 2 * SSD_GROUPS * CHUNK
        _ssd_main_chunk(j, prep[j], wide[w0:w0 + WIDE_ROWS], cols[c0:c0 + 2 * SSD_GROUPS * CHUNK],
                        act_ref, z_ref, hb_ref, dskip_ref, ng_ref, y_ref, state_ref, ybuf_ref.at[j])


def _ssd_main_chunk(j, prep, wide, cols, act_ref, z_ref, hb_ref, dskip_ref, ng_ref, y_ref, state_ref, ybuf_ref):
    rows = slice(j * CHUNK, (j + 1) * CHUNK)
    cbs, key_f, key_b = prep
    scale_f = wide[0:CHUNK]
    state_scale = wide[CHUNK:2 * CHUNK]
    scale_b = wide[2 * CHUNK:3 * CHUNK]
    self_b = wide[3 * CHUNK:4 * CHUNK]
    decay = wide[WIDE_ROWS - 1:WIDE_ROWS, :]
    xs16 = act_ref[rows, :SSD_INNER]
    xs = xs16.astype(F32)
    bm16 = act_ref[rows, SSD_INNER:SSD_INNER + SSD_BC]
    cm16 = act_ref[rows, SSD_INNER + SSD_BC:]

    row = lax.broadcasted_iota(jnp.int32, (CHUNK, CHUNK), 0)
    col = lax.broadcasted_iota(jnp.int32, (CHUNK, CHUNK), 1)
    lower = row >= col
    first_head = lax.broadcasted_iota(jnp.int32, (CHUNK, 2 * SSD_HEAD_DIM), 1) < SSD_HEAD_DIM

    groups = range(SSD_GROUPS)
    ns = [slice(g * SSD_STATE, (g + 1) * SSD_STATE) for g in groups]
    gs = [slice(g * GROUP_W, (g + 1) * GROUP_W) for g in groups]
    xw = (xs * state_scale).astype(BF16)
    st = [state_ref[:, gs[g]] for g in groups]
    off_f = [_dot(cm16[:, ns[g]], st[g].astype(BF16)) for g in groups]
    off_b = [_dot(cm16[:, ns[g]], hb_ref[j, :, gs[g]]) for g in groups]
    for g in groups:
        bt = bm16[:, ns[g]].astype(F32).T.astype(BF16)
        state_ref[:, gs[g]] = st[g] * decay[:, gs[g]] + _dot(bt, xw[:, gs[g]])

    for g in groups:
        col_f = cols[g * CHUNK:(g + 1) * CHUNK]
        col_b = cols[(SSD_GROUPS + g) * CHUNK:(SSD_GROUPS + g + 1) * CHUNK]
        for pair in range(SSD_HPG // 2):
            ms = []
            for r in (2 * pair, 2 * pair + 1):
                h = g * SSD_HPG + r
                hb_col = SSD_HEADS + h
                rs = slice(r * CHUNK, (r + 1) * CHUNK)
                seg_f = col_f[:, rs] - key_f[h:h + 1, :]
                seg_b = key_b[hb_col:hb_col + 1, :] - col_b[:, rs]
                ms.append((cbs[g] * jnp.exp2(jnp.where(lower, seg_f, seg_b))).astype(BF16))
            ps = slice((g * SSD_HPG + 2 * pair) * SSD_HEAD_DIM, (g * SSD_HPG + 2 * pair + 2) * SSD_HEAD_DIM)
            x2 = xs16[:, ps]
            zero = jnp.zeros_like(x2)
            rhs = jnp.concatenate([jnp.where(first_head, x2, zero), jnp.where(first_head, zero, x2)], axis=0)
            ybuf_ref[:, ps] = _dot(jnp.concatenate(ms, axis=1), rhs)

    zf = z_ref[rows, :].astype(F32)
    zgate = _silu(zf)
    for g in groups:
        y = (ybuf_ref[:, gs[g]] + scale_f[:, gs[g]] * off_f[g] + scale_b[:, gs[g]] * off_b[g]
             + (dskip_ref[:, gs[g]] + self_b[:, gs[g]]) * xs[:, gs[g]])
        v = y * zgate[:, gs[g]]
        v = v * lax.rsqrt(jnp.mean(v * v, axis=-1, keepdims=True) + EPS)
        y_ref[rows, gs[g]] = (v * ng_ref[:, gs[g]]).astype(BF16)


def _ssd_main(act, dt, z, hb, hp, dskip, norm_g, e2f, sel, bsz, nc):
    steps = nc // MAIN_CHUNKS
    rows = MAIN_CHUNKS * CHUNK
    block_of = lambda b, c: b * steps + c
    return pl.pallas_call(
        _ssd_main_kernel,
        grid=(bsz, steps),
        in_specs=[pl.BlockSpec((rows, SSD_XBC), lambda b, c: (block_of(b, c), 0)),
                  pl.BlockSpec((rows, LANES), lambda b, c: (block_of(b, c), 0)),
                  pl.BlockSpec((rows, SSD_INNER), lambda b, c: (block_of(b, c), 0)),
                  pl.BlockSpec((MAIN_CHUNKS, SSD_STATE, SSD_INNER), lambda b, c: (block_of(b, c), 0, 0)),
                  _resident((8, LANES)),
                  _resident((1, SSD_INNER)),
                  _resident((1, SSD_INNER)),
                  _resident((2 * LANES, SSD_INNER)),
                  _resident((2 * LANES, SSD_HPG * CHUNK))],
        out_specs=pl.BlockSpec((rows, SSD_INNER), lambda b, c: (block_of(b, c), 0)),
        out_shape=jax.ShapeDtypeStruct((bsz * nc * CHUNK, SSD_INNER), BF16),
        scratch_shapes=[pltpu.VMEM((SSD_STATE, SSD_INNER), F32),
                        pltpu.VMEM((MAIN_CHUNKS, CHUNK, SSD_INNER), F32)],
        compiler_params=_params(2),
        name="ssd_main",
    )(act, dt, z, hb, hp, dskip, norm_g, e2f, sel)


def _swa_kernel(q_ref, kvp_ref, kvc_ref, kvn_ref, sink_ref, o_ref):
    n = pl.program_id(1)
    last = pl.num_programs(1) - 1
    kj = lax.broadcasted_iota(jnp.int32, (3 * ATTN_BLOCK, ATTN_BLOCK), 0)
    qi = lax.broadcasted_iota(jnp.int32, (3 * ATTN_BLOCK, ATTN_BLOCK), 1)
    rel = kj - qi
    band = (rel >= 0) & (rel <= 2 * ATTN_BLOCK)
    kvcat = jnp.concatenate([kvp_ref[...], kvc_ref[...], kvn_ref[...]], axis=0)
    kcat = kvcat[:, :ATTN_KV_WIDTH]
    vcat_t = kvcat[:, ATTN_KV_WIDTH:].astype(F32).T.astype(BF16)
    scores = []
    for j in range(SWA_BLOCKS):
        valid = band
        if j == 0:
            valid = valid & ((kj >= ATTN_BLOCK) | (n > 0))
        if j == SWA_BLOCKS - 1:
            valid = valid & ((kj < 2 * ATTN_BLOCK) | (n < last))
        bias = jnp.concatenate([jnp.where(valid, 0.0, NEG_BIG)] * ATTN_GQA, axis=1)
        edge_bias = (bias[:ATTN_BLOCK], bias[2 * ATTN_BLOCK:])
        keys = slice(j * ATTN_BLOCK, (j + 3) * ATTN_BLOCK)
        qrows = slice(j * ATTN_BLOCK, (j + 1) * ATTN_BLOCK)
        for kv in range(ATTN_KV_HEADS):
            ks = slice(kv * ATTN_HEAD_DIM, (kv + 1) * ATTN_HEAD_DIM)
            q_stack = jnp.concatenate(
                [q_ref[qrows, (kv * ATTN_GQA + r) * ATTN_HEAD_DIM:(kv * ATTN_GQA + r + 1) * ATTN_HEAD_DIM]
                 for r in range(ATTN_GQA)], axis=0)
            s = _dot_nt(kcat[keys, ks], q_stack)
            scores.append(jnp.concatenate([s[:ATTN_BLOCK] + edge_bias[0], s[ATTN_BLOCK:2 * ATTN_BLOCK],
                                           s[2 * ATTN_BLOCK:] + edge_bias[1]], axis=0))
    probs = []
    for idx, s in enumerate(scores):
        kv = idx % ATTN_KV_HEADS
        sink = sink_ref[kv:kv + 1, :]
        m = jnp.maximum(jnp.max(s, axis=0, keepdims=True), sink)
        p = jnp.exp2(s - m)
        denom = jnp.sum(p, axis=0, keepdims=True) + jnp.exp2(sink - m)
        probs.append((p.astype(BF16), 1.0 / denom))
    for j in range(SWA_BLOCKS):
        keys = slice(j * ATTN_BLOCK, (j + 3) * ATTN_BLOCK)
        outs = []
        for kv in range(ATTN_KV_HEADS):
            ks = slice(kv * ATTN_HEAD_DIM, (kv + 1) * ATTN_HEAD_DIM)
            p16, inv = probs[j * ATTN_KV_HEADS + kv]
            o_t = _dot(vcat_t[ks, keys], p16) * inv
            outs.extend(o_t[:, r * ATTN_BLOCK:(r + 1) * ATTN_BLOCK] for r in range(ATTN_GQA))
        o_ref[j * ATTN_BLOCK:(j + 1) * ATTN_BLOCK, :] = jnp.concatenate(outs, axis=0).T.astype(BF16)


def _swa(q, kv, sink_rows, bsz, nb):
    steps = nb // SWA_BLOCKS
    rows = SWA_BLOCKS * ATTN_BLOCK
    cur = lambda b, n: (b * steps + n, 0)
    prev = lambda b, n: (b * nb + jnp.maximum(SWA_BLOCKS * n - 1, 0), 0)
    nxt = lambda b, n: (b * nb + jnp.minimum(SWA_BLOCKS * (n + 1), nb - 1), 0)
    return pl.pallas_call(
        _swa_kernel,
        grid=(bsz, steps),
        in_specs=[pl.BlockSpec((rows, ATTN_WIDTH), cur),
                  pl.BlockSpec((ATTN_BLOCK, 2 * ATTN_KV_WIDTH), prev),
                  pl.BlockSpec((rows, 2 * ATTN_KV_WIDTH), cur),
                  pl.BlockSpec((ATTN_BLOCK, 2 * ATTN_KV_WIDTH), nxt),
                  _resident((8, ATTN_GQA * ATTN_BLOCK))],
        out_specs=pl.BlockSpec((rows, ATTN_WIDTH), cur),
        out_shape=jax.ShapeDtypeStruct(q.shape, BF16),
        compiler_params=_params(2),
        name="swa",
    )(q, kv, kv, kv, sink_rows)


def _mem_kv_kernel(m_ref, g_ref, w_ref, o_ref):
    h = _rms(m_ref[...], g_ref[...]).astype(BF16)
    o_ref[...] = _dot(h, w_ref[...]).astype(BF16)


def _mem_kv(mem2, g, w_kv):
    t = mem2.shape[0]
    tm = 256
    row = lambda i: (i, 0)
    return pl.pallas_call(
        _mem_kv_kernel,
        grid=(t // tm,),
        in_specs=[pl.BlockSpec((tm, D_MODEL), row),
                  _resident((1, D_MODEL)),
                  _resident((D_MODEL, 2 * D_MODEL))],
        out_specs=pl.BlockSpec((tm, 2 * D_MODEL), row),
        out_shape=jax.ShapeDtypeStruct((t, 2 * D_MODEL), BF16),
        compiler_params=_params(1),
        name="mem_kv",
    )(mem2, g, w_kv)


def _merge_xattn_kernel(x_ref, y_ref, a_ref, gate_ref, kv_ref, ws_ref, wa_ref, wo_ref, g_ref, wq_ref, wxo_ref,
                        o_ref, ctx_ref):
    bs = _dot(y_ref[...], ws_ref[...])
    ba = _dot(a_ref[...], wa_ref[...])
    gate = jax.nn.sigmoid(gate_ref[...].astype(F32))
    mix = (gate[:, :D_MODEL] * bs + gate[:, D_MODEL:] * ba).astype(BF16)
    x = x_ref[...] + _dot(mix, wo_ref[...])

    h = _rms(x, g_ref[...]).astype(BF16)
    q = (_dot(h, wq_ref[...]) * (XATTN_HEAD_DIM ** -0.5 * LOG2_E)).astype(BF16)
    heads = [slice(hd * XATTN_HEAD_DIM, (hd + 1) * XATTN_HEAD_DIM) for hd in range(XATTN_HEADS)]
    scores = [_dot_nt(q[:, ks], kv_ref[:, ks]) for ks in heads]
    probs = []
    for s in scores:
        p = jnp.exp2(s - jnp.max(s, axis=-1, keepdims=True))
        probs.append((p.astype(BF16), 1.0 / jnp.sum(p, axis=-1, keepdims=True)))
    for ks, (p16, inv) in zip(heads, probs):
        vs = slice(D_MODEL + ks.start, D_MODEL + ks.stop)
        ctx_ref[:, ks] = (_dot(p16, kv_ref[:, vs]) * inv).astype(BF16)
    o_ref[...] = x + _dot(ctx_ref[...], wxo_ref[...])


def _merge_xattn(x2, y, attn, gates, kv, ws, wa, wo, g, wq, wxo, seq, mem_len):
    t = x2.shape[0]
    tm = 512
    per_seq = seq // tm
    row = lambda i: (i, 0)
    return pl.pallas_call(
        _merge_xattn_kernel,
        grid=(t // tm,),
        in_specs=[pl.BlockSpec((tm, D_MODEL), row),
                  pl.BlockSpec((tm, SSD_INNER), row),
                  pl.BlockSpec((tm, ATTN_WIDTH), row),
                  pl.BlockSpec((tm, 2 * D_MODEL), row),
                  pl.BlockSpec((mem_len, 2 * D_MODEL), lambda i: (i // per_seq, 0)),
                  _resident((SSD_INNER, D_MODEL)),
                  _resident((ATTN_WIDTH, D_MODEL)),
                  _resident((D_MODEL, D_MODEL)),
                  _resident((1, D_MODEL)),
                  _resident((D_MODEL, D_MODEL)),
                  _resident((D_MODEL, D_MODEL))],
        out_specs=pl.BlockSpec((tm, D_MODEL), row),
        out_shape=jax.ShapeDtypeStruct((t, D_MODEL), F32),
        scratch_shapes=[pltpu.VMEM((tm, D_MODEL), BF16)],
        compiler_params=_params(1),
        name="merge_xattn",
    )(x2, y, attn, gates, kv, ws, wa, wo, g, wq, wxo)


MXU_TILE = 256
FFN_SPLIT = (FFN_HIDDEN // MXU_TILE + 1) // 2 * MXU_TILE


def _ffn_kernel(x_ref, g_ref, wi_ref, wd_ref, gf_ref, o_ref):
    x = x_ref[...]
    h = _rms(x, g_ref[...]).astype(BF16)
    acc = x
    for lo, hi in ((0, FFN_SPLIT), (FFN_SPLIT, FFN_HIDDEN)):
        gate = _dot(h, wi_ref[:, lo:hi])
        up = _dot(h, wi_ref[:, FFN_HIDDEN + lo:FFN_HIDDEN + hi])
        act = (_silu(gate) * up).astype(BF16)
        acc = acc + _dot(act, wd_ref[lo:hi, :])
    o_ref[...] = _rms(acc, gf_ref[...])


def _ffn(x2, g, wi, wd, gf):
    t = x2.shape[0]
    tm = 512
    row = lambda i: (i, 0)
    return pl.pallas_call(
        _ffn_kernel,
        grid=(t // tm,),
        in_specs=[pl.BlockSpec((tm, D_MODEL), row),
                  _resident((1, D_MODEL)),
                  _resident((D_MODEL, 2 * FFN_HIDDEN)),
                  _resident((FFN_HIDDEN, D_MODEL)),
                  _resident((1, D_MODEL))],
        out_specs=pl.BlockSpec((tm, D_MODEL), row),
        out_shape=jax.ShapeDtypeStruct((t, D_MODEL), F32),
        compiler_params=_params(1),
        name="ffn",
    )(x2, g, wi, wd, gf)


def _pad_rows(a, rows):
    return jnp.pad(a, ((0, rows - a.shape[0]), (0, 0)))


def _pad_cols(a, cols):
    return jnp.pad(a, ((0, 0), (0, cols - a.shape[1])))


def _column_select_matrix():
    j = np.arange(LANES)[:, None]
    blk = np.arange(SSD_HPG * CHUNK)[None, :] // CHUNK
    e = (j == blk).astype(np.float32)
    return jnp.asarray(np.concatenate([e, e], axis=0), dtype=BF16)


def _select_matrix(slot):
    j = np.arange(LANES)[:, None]
    ch = np.arange(SSD_INNER)[None, :]
    e = (j == slot * SSD_HEADS + ch // SSD_HEAD_DIM).astype(np.float32)
    return jnp.asarray(np.concatenate([e, e], axis=0), dtype=BF16)


def _layer(x2, mem2, bsz, seq, mem_len, norm_mix_g, w_in, conv_w, conv_b, dt_bias_fwd, dt_bias_bwd,
           a_log_fwd, a_log_bwd, d_skip, ssd_norm_g, attn_sink, w_branch_ssd, w_branch_attn, w_mix_out,
           norm_xattn_g, norm_mem_g, w_xattn_q, w_xattn_kv, w_xattn_out, norm_ffn_g, w_ffn_in, w_ffn_out,
           final_g):
    nc = seq // CHUNK
    w_a = w_in[:, :WIDTH_A].astype(BF16)
    w_b = w_in[:, START_B:].astype(BF16)

    half = ATTN_HEAD_DIM // 2
    inv_freq = ROPE_THETA ** (-jnp.arange(half, dtype=F32) / half)
    ang = jnp.arange(seq, dtype=F32)[:, None] * inv_freq[None]
    cos = jnp.tile(jnp.cos(ang), (1, LANES // half))
    sin_signed = jnp.tile(jnp.concatenate([-jnp.sin(ang), jnp.sin(ang)], axis=1), (1, LANES // ATTN_HEAD_DIM))

    cw = _pad_rows(conv_w, 8)
    z, act, q, kv_attn, gates, dt = _in_proj(x2, norm_mix_g[None], w_a, w_b, cos, sin_signed, cw, conv_b[None], seq)

    hp = _pad_rows(jnp.stack([_pad_cols(jnp.concatenate([dt_bias_fwd, dt_bias_bwd])[None], LANES)[0],
                              _pad_cols(jnp.concatenate([a_log_fwd, a_log_bwd])[None], LANES)[0]]), 8)
    e2f = _select_matrix(0)
    e2b = _select_matrix(1)
    hb = _ssd_bwd(act, dt, hp, e2b, bsz, nc)
    dskip = jnp.repeat(d_skip, SSD_HEAD_DIM)[None]
    y = _ssd_main(act, dt, z, hb, hp, dskip, ssd_norm_g[None], e2f, _column_select_matrix(), bsz, nc)

    sink_rows = _pad_rows(jnp.repeat(attn_sink * LOG2_E, ATTN_BLOCK).reshape(ATTN_KV_HEADS, ATTN_GQA * ATTN_BLOCK), 8)
    attn = _swa(q, kv_attn, sink_rows, bsz, seq // ATTN_BLOCK)

    kv = _mem_kv(mem2, norm_mem_g[None], w_xattn_kv.astype(BF16))
    x2 = _merge_xattn(x2, y, attn, gates, kv, w_branch_ssd.astype(BF16), w_branch_attn.astype(BF16),
                      w_mix_out.astype(BF16), norm_xattn_g[None], w_xattn_q.astype(BF16),
                      w_xattn_out.astype(BF16), seq, mem_len)

    return _ffn(x2, norm_ffn_g[None], w_ffn_in.astype(BF16), w_ffn_out.astype(BF16), final_g)


def kernel(x, mem, norm_mix_g, w_in, conv_w, conv_b, dt_bias_fwd, dt_bias_bwd, a_log_fwd, a_log_bwd, d_skip,
           ssd_norm_g, attn_sink, w_branch_ssd, w_branch_attn, w_mix_out, norm_xattn_g, norm_mem_g, w_xattn_q,
           w_xattn_kv, w_xattn_out, norm_ffn_g, w_ffn_in, w_ffn_out, norm_final_g):
    bsz, seq, _ = x.shape
    mem_len = mem.shape[1]
    assert w_in.shape[0] == 1, "single-layer stack expected"
    layer = 0
    x2 = x.reshape(bsz * seq, D_MODEL)
    mem2 = mem.reshape(bsz * mem_len, D_MODEL)
    out = _layer(x2, mem2, bsz, seq, mem_len, norm_mix_g[layer], w_in[layer], conv_w[layer], conv_b[layer],
                 dt_bias_fwd[layer], dt_bias_bwd[layer], a_log_fwd[layer], a_log_bwd[layer], d_skip[layer],
                 ssd_norm_g[layer], attn_sink[layer], w_branch_ssd[layer], w_branch_attn[layer],
                 w_mix_out[layer], norm_xattn_g[layer], norm_mem_g[layer], w_xattn_q[layer],
                 w_xattn_kv[layer], w_xattn_out[layer], norm_ffn_g[layer], w_ffn_in[layer],
                 w_ffn_out[layer], norm_final_g[None])
    return out.reshape(bsz, seq, D_MODEL)
```

```python
import functools

import jax
import jax.numpy as jnp
import numpy as np
from jax import lax
from jax.experimental import pallas as pl
from jax.experimental.pallas import tpu as pltpu

F32 = jnp.float32
BF16 = jnp.bfloat16

D_MODEL = 1024
EPS = 1e-6
SSD_INNER = 2048
SSD_HEAD_DIM = 64
SSD_HEADS = 32
SSD_GROUPS = 4
SSD_HPG = 8
SSD_STATE = 128
SSD_CONV = 5
CHUNK = 128
BWD_CHUNKS = 8
MAIN_CHUNKS = 2
DECAY_ROWS = 16
WIDE_ROWS = 4 * CHUNK + DECAY_ROWS
SSD_BC = SSD_GROUPS * SSD_STATE
SSD_XBC = SSD_INNER + 2 * SSD_BC
GROUP_W = SSD_HPG * SSD_HEAD_DIM
ATTN_HEAD_DIM = 64
ATTN_KV_HEADS = 4
ATTN_GQA = 4
ATTN_WIDTH = 1024
ATTN_KV_WIDTH = 256
ATTN_BLOCK = 128
SWA_BLOCKS = 4
ROPE_THETA = 10000.0
XATTN_HEADS = 4
XATTN_HEAD_DIM = 256
FFN_HIDDEN = 2816

LANES = 128
NEG_BIG = -1e30
LOG2_E = 1.4426950408889634

OFF_Z = 0
OFF_XBC = OFF_Z + SSD_INNER
OFF_DT = OFF_XBC + SSD_XBC
WIDTH_A = OFF_DT + LANES
START_B = OFF_DT + 2 * SSD_HEADS
OFF_Q = 0
OFF_K = OFF_Q + ATTN_WIDTH
OFF_G = OFF_K + 2 * ATTN_KV_WIDTH
WIDTH_B = OFF_G + 2 * D_MODEL

VMEM_LIMIT = 56 * 1024 * 1024


def _params(n_axes, flags=None):
    return pltpu.CompilerParams(dimension_semantics=("arbitrary",) * n_axes,
                                vmem_limit_bytes=VMEM_LIMIT, flags=flags)


def _resident(shape):
    nd = len(shape)
    return pl.BlockSpec(shape, lambda *_: (0,) * nd, pipeline_mode=pl.Buffered(1))


def _rms(x, g):
    return x * lax.rsqrt(jnp.mean(x * x, axis=-1, keepdims=True) + EPS) * g


def _silu(x):
    h = 0.5 * x
    return h + h * jnp.tanh(h)


def _dot(a, b):
    return jnp.dot(a, b, preferred_element_type=F32)


def _dot_nt(a, b):
    return lax.dot_general(a, b, (((1,), (1,)), ((), ())), preferred_element_type=F32)


def _rotary(u, cos, sin_signed):
    n = u.shape[1]
    lane = lax.broadcasted_iota(jnp.int32, u.shape, 1)
    first_half = (lane % ATTN_HEAD_DIM) < (ATTN_HEAD_DIM // 2)
    partner = jnp.where(first_half, pltpu.roll(u, n - ATTN_HEAD_DIM // 2, axis=1),
                        pltpu.roll(u, ATTN_HEAD_DIM // 2, axis=1))
    reps = n // LANES
    cos_t = jnp.concatenate([cos] * reps, axis=1) if reps > 1 else cos
    sin_t = jnp.concatenate([sin_signed] * reps, axis=1) if reps > 1 else sin_signed
    return u * cos_t + partner * sin_t


def _in_proj_kernel(x_ref, xp_ref, xn_ref, g_ref, wa_ref, wb_ref, cos_ref, sin_ref, cw_ref, cb_ref,
                    z_ref, act_ref, q_ref, kv_ref, gate_ref, dt_ref, ext_ref, *, per_seq):
    i = pl.program_id(0)
    tm = x_ref.shape[0]
    g = g_ref[...]
    h_ext = _rms(jnp.concatenate([x_ref[...], xp_ref[...], xn_ref[...]], axis=0), g).astype(BF16)
    step = 256
    rblk = 128
    row_blocks = [slice(r, r + rblk) for r in range(0, tm, rblk)]
    has_prev = (i % per_seq) > 0
    has_next = (i % per_seq) < per_seq - 1

    def proj(rs, w_ref, off, width):
        return _dot(h_ext[rs], w_ref[:, off:off + width])

    def xbc_proj(c):
        cs = slice(c, c + step)
        for rs in row_blocks[:-1]:
            ext_ref[8 + rs.start:8 + rs.stop, cs] = proj(rs, wa_ref, OFF_XBC + c, step)
        last = row_blocks[-1]
        u = proj(slice(last.start, tm + 16), wa_ref, OFF_XBC + c, step)
        ext_ref[8 + last.start:8 + tm, cs] = u[:rblk]
        ext_ref[0:8, cs] = jnp.where(has_prev, u[rblk:rblk + 8], 0.0)
        ext_ref[8 + tm:16 + tm, cs] = jnp.where(has_next, u[rblk + 8:], 0.0)

    assert SSD_CONV == 5

    def conv_silu(c):
        cs = slice(c, c + step)
        e = ext_ref[:, cs]
        rows = e.shape[0]
        tap = lambda k: cw_ref[k:k + 1, cs] * e
        down = lambda a: pltpu.roll(a, 1, axis=0)
        up = lambda a: pltpu.roll(a, rows - 1, axis=0)
        left = down(down(tap(0)) + tap(1)) + tap(2)
        right = up(up(tap(4)) + tap(3))
        acc = (left + right)[8:8 + tm] + cb_ref[:, cs]
        act_ref[:, cs] = _silu(acc).astype(BF16)

    def z_proj(c):
        for rs in row_blocks:
            z_ref[rs, c:c + step] = proj(rs, wa_ref, OFF_Z + c, step).astype(BF16)

    def q_proj(c):
        for rs in row_blocks:
            q = _rotary(proj(rs, wb_ref, OFF_Q + c, step), cos_ref[rs, :], sin_ref[rs, :])
            q_ref[rs, c:c + step] = (q * (ATTN_HEAD_DIM ** -0.5 * LOG2_E)).astype(BF16)

    def kv_proj(_):
        for rs in row_blocks:
            kv = proj(rs, wb_ref, OFF_K, 2 * ATTN_KV_WIDTH)
            kv_ref[rs, :ATTN_KV_WIDTH] = _rotary(kv[:, :ATTN_KV_WIDTH], cos_ref[rs, :], sin_ref[rs, :]).astype(BF16)
            kv_ref[rs, ATTN_KV_WIDTH:] = kv[:, ATTN_KV_WIDTH:].astype(BF16)
            dt_lane = lax.broadcasted_iota(jnp.int32, (rblk, LANES), 1)
            dt_ref[rs, :] = jnp.where(dt_lane < 2 * SSD_HEADS, proj(rs, wa_ref, OFF_DT, LANES), 0.0)

    def gate_proj(c):
        for rs in row_blocks:
            gate_ref[rs, c:c + step] = proj(rs, wb_ref, OFF_G + c, step).astype(BF16)

    mxu_tasks = ([(z_proj, c) for c in range(0, SSD_INNER, step)]
                 + [(q_proj, c) for c in range(0, ATTN_WIDTH, step)] + [(kv_proj, 0)]
                 + [(gate_proj, c) for c in range(0, 2 * D_MODEL, step)])
    conv_chunks = list(range(0, SSD_XBC, step))
    xbc_proj(conv_chunks[0])
    for j, c in enumerate(conv_chunks):
        if j + 1 < len(conv_chunks):
            xbc_proj(conv_chunks[j + 1])
        for fn, arg in mxu_tasks[2 * j:2 * j + 2]:
            fn(arg)
        conv_silu(c)
    for fn, arg in mxu_tasks[2 * len(conv_chunks):]:
        fn(arg)


def _in_proj(x2, g, w_a, w_b, cos, sin_signed, conv_w, conv_b, seq):
    t = x2.shape[0]
    tm = 512
    per_seq = seq // tm
    halo_per_tile = tm // 8
    last_halo = t // 8 - 1
    row = lambda i: (i, 0)
    pos = lambda i: (i % per_seq, 0)
    outs = [(SSD_INNER, BF16), (SSD_XBC, BF16), (ATTN_WIDTH, BF16), (2 * ATTN_KV_WIDTH, BF16),
            (2 * D_MODEL, BF16), (LANES, F32)]
    return pl.pallas_call(
        functools.partial(_in_proj_kernel, per_seq=per_seq),
        grid=(t // tm,),
        in_specs=[pl.BlockSpec((tm, D_MODEL), row),
                  pl.BlockSpec((8, D_MODEL), lambda i: (jnp.maximum(i * halo_per_tile - 1, 0), 0)),
                  pl.BlockSpec((8, D_MODEL), lambda i: (jnp.minimum((i + 1) * halo_per_tile, last_halo), 0)),
                  _resident((1, D_MODEL)),
                  _resident((D_MODEL, WIDTH_A)),
                  _resident((D_MODEL, WIDTH_B)),
                  pl.BlockSpec((tm, LANES), pos),
                  pl.BlockSpec((tm, LANES), pos),
                  _resident((8, SSD_XBC)),
                  _resident((1, SSD_XBC))],
        out_specs=[pl.BlockSpec((tm, w), row) for w, _ in outs],
        out_shape=[jax.ShapeDtypeStruct((t, w), d) for w, d in outs],
        scratch_shapes=[pltpu.VMEM((tm + 16, SSD_XBC), F32)],
        compiler_params=_params(1),
        name="in_proj",
    )(x2, x2, x2, g, w_a, w_b, cos, sin_signed, conv_w, conv_b)


def _cumsum_rows(a):
    row = lax.broadcasted_iota(jnp.int32, a.shape, 0)
    v = a
    k = 1
    while k < a.shape[0]:
        v = v + jnp.where(row >= k, pltpu.roll(v, k, axis=0), 0.0)
        k *= 2
    return v


def _expand(v, e2_ref):
    hi = v.astype(BF16)
    lo = (v - hi.astype(F32)).astype(BF16)
    return _dot(jnp.concatenate([hi, lo], axis=1), e2_ref[...])


def _softplus(x):
    return jnp.maximum(x, 0.0) + jnp.log1p(jnp.exp(-jnp.abs(x)))


def _head_scalars(dt_raw, hp_ref):
    dtv = _softplus(dt_raw + hp_ref[0:1, :])
    a = dtv * (-LOG2_E * jnp.exp(hp_ref[1:2, :]))
    return dtv, a


def _chunk_scalars(dt_raw, hp_ref):
    dtv, a = _head_scalars(dt_raw, hp_ref)
    incl = _cumsum_rows(a)
    return dtv, incl, incl - a


def _ssd_bwd_kernel(act_ref, dt_ref, hp_ref, e2b_ref, hb_ref, state_ref):
    c = pl.program_id(1)

    @pl.when(c == 0)
    def _():
        state_ref[...] = jnp.zeros_like(state_ref)

    scales, decays = [], []
    for j in range(BWD_CHUNKS):
        dtv, incl, excl = _chunk_scalars(dt_ref[j * CHUNK:(j + 1) * CHUNK, :], hp_ref)
        scales.append(jnp.exp2(excl) * dtv)
        decays.append(jnp.exp2(incl[CHUNK - DECAY_ROWS:, :]))
    wide = _expand(jnp.concatenate(scales + decays, axis=0), e2b_ref)

    for j in reversed(range(BWD_CHUNKS)):
        rows = slice(j * CHUNK, (j + 1) * CHUNK)
        hb_ref[j] = state_ref[...].astype(BF16)
        xs = act_ref[rows, :SSD_INNER].astype(F32)
        bm = act_ref[rows, SSD_INNER:].astype(F32)
        xw = (xs * wide[rows]).astype(BF16)
        last = BWD_CHUNKS * CHUNK + (j + 1) * DECAY_ROWS - 1
        decay = wide[last:last + 1, :]
        for g in range(SSD_GROUPS):
            gs = slice(g * GROUP_W, (g + 1) * GROUP_W)
            bt = bm[:, g * SSD_STATE:(g + 1) * SSD_STATE].T.astype(BF16)
            state_ref[:, gs] = state_ref[:, gs] * decay[:, gs] + _dot(bt, xw[:, gs])


def _ssd_bwd(act, dt, hp, e2b, bsz, nc):
    width = SSD_INNER + SSD_BC
    steps = nc // BWD_CHUNKS
    block_of = lambda b, c: b * steps + (steps - 1 - c)
    return pl.pallas_call(
        _ssd_bwd_kernel,
        grid=(bsz, steps),
        in_specs=[pl.BlockSpec((BWD_CHUNKS * CHUNK, width), lambda b, c: (block_of(b, c), 0)),
                  pl.BlockSpec((BWD_CHUNKS * CHUNK, LANES), lambda b, c: (block_of(b, c), 0)),
                  _resident((8, LANES)),
                  _resident((2 * LANES, SSD_INNER))],
        out_specs=pl.BlockSpec((BWD_CHUNKS, SSD_STATE, SSD_INNER), lambda b, c: (block_of(b, c), 0, 0)),
        out_shape=jax.ShapeDtypeStruct((bsz * nc, SSD_STATE, SSD_INNER), BF16),
        scratch_shapes=[pltpu.VMEM((SSD_STATE, SSD_INNER), F32)],
        compiler_params=_params(2),
        name="ssd_bwd",
    )(act, dt, hp, e2b)


def _ssd_main_kernel(act_ref, dt_ref, z_ref, hb_ref, hp_ref, dskip_ref, ng_ref, e2f_ref, sel_ref,
                     y_ref, state_ref, ybuf_ref):
    c = pl.program_id(1)

    @pl.when(c == 0)
    def _():
        state_ref[...] = jnp.zeros_like(state_ref)

    lane = lax.broadcasted_iota(jnp.int32, (CHUNK, LANES), 1)
    row = lax.broadcasted_iota(jnp.int32, (CHUNK, CHUNK), 0)
    col = lax.broadcasted_iota(jnp.int32, (CHUNK, CHUNK), 1)
    groups = range(SSD_GROUPS)
    to_fwd_slot = lambda v: pltpu.roll(v, LANES - SSD_HEADS, axis=1)
    prep, wide_in, col_in = [], [], []
    for j in range(MAIN_CHUNKS):
        rows = slice(j * CHUNK, (j + 1) * CHUNK)
        dtv, incl, excl = _chunk_scalars(dt_ref[rows, :], hp_ref)
        tot = incl[CHUNK - 1:CHUNK, :]
        cbs = [_dot_nt(act_ref[rows, SSD_INNER + SSD_BC + g * SSD_STATE:SSD_INNER + SSD_BC + (g + 1) * SSD_STATE],
                       act_ref[rows, SSD_INNER + g * SSD_STATE:SSD_INNER + (g + 1) * SSD_STATE]) for g in groups]
        lane_group = (lane % SSD_HEADS) // SSD_HPG
        cb_diag = jnp.zeros_like(dtv)
        for g in groups:
            dg = jnp.sum(jnp.where(row == col, cbs[g], 0.0), axis=1, keepdims=True)
            cb_diag = jnp.where(lane_group == g, dg, cb_diag)
        wide_in += [jnp.exp2(incl),
                    jnp.exp2(tot - incl) * dtv,
                    to_fwd_slot(jnp.exp2(tot - excl)),
                    to_fwd_slot(cb_diag * dtv),
                    jnp.exp2(incl[CHUNK - DECAY_ROWS:, :])]
        log_dt = jnp.log2(dtv)
        key_f = (incl - log_dt).T
        key_b = (excl + log_dt).T
        vmix = jnp.where(lane < SSD_HEADS, incl, excl)
        hi = vmix.astype(BF16).astype(F32)
        lo = vmix - hi
        for shift in [d * SSD_HEADS + g * SSD_HPG for d in range(2) for g in groups]:
            sh = (LANES - shift) % LANES
            col_in.append(jnp.concatenate([pltpu.roll(hi, sh, axis=1) if sh else hi,
                                           pltpu.roll(lo, sh, axis=1) if sh else lo], axis=1).astype(BF16))
        prep.append((cbs, key_f, key_b))
    wide = _expand(jnp.concatenate(wide_in, axis=0), e2f_ref)
    cols = _dot(jnp.concatenate(col_in, axis=0), sel_ref[...])

    for j in range(MAIN_CHUNKS):
        w0 = j * WIDE_ROWS
        c0 = j * 2 * SSD_GROUPS * CHUNK
        _ssd_main_chunk(j, prep[j], wide[w0:w0 + WIDE_ROWS], cols[c0:c0 + 2 * SSD_GROUPS * CHUNK],
                        act_ref, z_ref, hb_ref, dskip_ref, ng_ref, y_ref, state_ref, ybuf_ref.at[j])


def _ssd_main_chunk(j, prep, wide, cols, act_ref, z_ref, hb_ref, dskip_ref, ng_ref, y_ref, state_ref, ybuf_ref):
    rows = slice(j * CHUNK, (j + 1) * CHUNK)
    cbs, key_f, key_b = prep
    scale_f = wide[0:CHUNK]
    state_scale = wide[CHUNK:2 * CHUNK]
    scale_b = wide[2 * CHUNK:3 * CHUNK]
    self_b = wide[3 * CHUNK:4 * CHUNK]
    decay = wide[WIDE_ROWS - 1:WIDE_ROWS, :]
    xs16 = act_ref[rows, :SSD_INNER]
    xs = xs16.astype(F32)
    bm16 = act_ref[rows, SSD_INNER:SSD_INNER + SSD_BC]
    cm16 = act_ref[rows, SSD_INNER + SSD_BC:]

    row = lax.broadcasted_iota(jnp.int32, (CHUNK, CHUNK), 0)
    col = lax.broadcasted_iota(jnp.int32, (CHUNK, CHUNK), 1)
    lower = row >= col
    first_head = lax.broadcasted_iota(jnp.int32, (CHUNK, 2 * SSD_HEAD_DIM), 1) < SSD_HEAD_DIM

    groups = range(SSD_GROUPS)
    ns = [slice(g * SSD_STATE, (g + 1) * SSD_STATE) for g in groups]
    gs = [slice(g * GROUP_W, (g + 1) * GROUP_W) for g in groups]
    xw = (xs * state_scale).astype(BF16)
    st = [state_ref[:, gs[g]] for g in groups]
    off_f = [_dot(cm16[:, ns[g]], st[g].astype(BF16)) for g in groups]
    off_b = [_dot(cm16[:, ns[g]], hb_ref[j, :, gs[g]]) for g in groups]
    for g in groups:
        bt = bm16[:, ns[g]].astype(F32).T.astype(BF16)
        state_ref[:, gs[g]] = st[g] * decay[:, gs[g]] + _dot(bt, xw[:, gs[g]])

    for g in groups:
        col_f = cols[g * CHUNK:(g + 1) * CHUNK]
        col_b = cols[(SSD_GROUPS + g) * CHUNK:(SSD_GROUPS + g + 1) * CHUNK]
        for pair in range(SSD_HPG // 2):
            ms = []
            for r in (2 * pair, 2 * pair + 1):
                h = g * SSD_HPG + r
                hb_col = SSD_HEADS + h
                rs = slice(r * CHUNK, (r + 1) * CHUNK)
                seg_f = col_f[:, rs] - key_f[h:h + 1, :]
                seg_b = key_b[hb_col:hb_col + 1, :] - col_b[:, rs]
                ms.append((cbs[g] * jnp.exp2(jnp.where(lower, seg_f, seg_b))).astype(BF16))
            ps = slice((g * SSD_HPG + 2 * pair) * SSD_HEAD_DIM, (g * SSD_HPG + 2 * pair + 2) * SSD_HEAD_DIM)
            x2 = xs16[:, ps]
            zero = jnp.zeros_like(x2)
            rhs = jnp.concatenate([jnp.where(first_head, x2, zero), jnp.where(first_head, zero, x2)], axis=0)
            ybuf_ref[:, ps] = _dot(jnp.concatenate(ms, axis=1), rhs)

    zf = z_ref[rows, :].astype(F32)
    zgate = _silu(zf)
    for g in groups:
        y = (ybuf_ref[:, gs[g]] + scale_f[:, gs[g]] * off_f[g] + scale_b[:, gs[g]] * off_b[g]
             + (dskip_ref[:, gs[g]] + self_b[:, gs[g]]) * xs[:, gs[g]])
        v = y * zgate[:, gs[g]]
        v = v * lax.rsqrt(jnp.mean(v * v, axis=-1, keepdims=True) + EPS)
        y_ref[rows, gs[g]] = (v * ng_ref[:, gs[g]]).astype(BF16)


def _ssd_main(act, dt, z, hb, hp, dskip, norm_g, e2f, sel, bsz, nc):
    steps = nc // MAIN_CHUNKS
    rows = MAIN_CHUNKS * CHUNK
    block_of = lambda b, c: b * steps + c
    return pl.pallas_call(
        _ssd_main_kernel,
        grid=(bsz, steps),
        in_specs=[pl.BlockSpec((rows, SSD_XBC), lambda b, c: (block_of(b, c), 0)),
                  pl.BlockSpec((rows, LANES), lambda b, c: (block_of(b, c), 0)),
                  pl.BlockSpec((rows, SSD_INNER), lambda b, c: (block_of(b, c), 0)),
                  pl.BlockSpec((MAIN_CHUNKS, SSD_STATE, SSD_INNER), lambda b, c: (block_of(b, c), 0, 0)),
                  _resident((8, LANES)),
                  _resident((1, SSD_INNER)),
                  _resident((1, SSD_INNER)),
                  _resident((2 * LANES, SSD_INNER)),
                  _resident((2 * LANES, SSD_HPG * CHUNK))],
        out_specs=pl.BlockSpec((rows, SSD_INNER), lambda b, c: (block_of(b, c), 0)),
        out_shape=jax.ShapeDtypeStruct((bsz * nc * CHUNK, SSD_INNER), BF16),
        scratch_shapes=[pltpu.VMEM((SSD_STATE, SSD_INNER), F32),
                        pltpu.VMEM((MAIN_CHUNKS, CHUNK, SSD_INNER), F32)],
        compiler_params=_params(2),
        name="ssd_main",
    )(act, dt, z, hb, hp, dskip, norm_g, e2f, sel)


def _swa_kernel(q_ref, kvp_ref, kvc_ref, kvn_ref, sink_ref, o_ref):
    n = pl.program_id(1)
    last = pl.num_programs(1) - 1
    kj = lax.broadcasted_iota(jnp.int32, (3 * ATTN_BLOCK, ATTN_BLOCK), 0)
    qi = lax.broadcasted_iota(jnp.int32, (3 * ATTN_BLOCK, ATTN_BLOCK), 1)
    rel = kj - qi
    band = (rel >= 0) & (rel <= 2 * ATTN_BLOCK)
    kvcat = jnp.concatenate([kvp_ref[...], kvc_ref[...], kvn_ref[...]], axis=0)
    kcat = kvcat[:, :ATTN_KV_WIDTH]
    vcat_t = kvcat[:, ATTN_KV_WIDTH:].astype(F32).T.astype(BF16)
    scores = []
    for j in range(SWA_BLOCKS):
        valid = band
        if j == 0:
            valid = valid & ((kj >= ATTN_BLOCK) | (n > 0))
        if j == SWA_BLOCKS - 1:
            valid = valid & ((kj < 2 * ATTN_BLOCK) | (n < last))
        bias = jnp.concatenate([jnp.where(valid, 0.0, NEG_BIG)] * ATTN_GQA, axis=1)
        edge_bias = (bias[:ATTN_BLOCK], bias[2 * ATTN_BLOCK:])
        keys = slice(j * ATTN_BLOCK, (j + 3) * ATTN_BLOCK)
        qrows = slice(j * ATTN_BLOCK, (j + 1) * ATTN_BLOCK)
        for kv in range(ATTN_KV_HEADS):
            ks = slice(kv * ATTN_HEAD_DIM, (kv + 1) * ATTN_HEAD_DIM)
            q_stack = jnp.concatenate(
                [q_ref[qrows, (kv * ATTN_GQA + r) * ATTN_HEAD_DIM:(kv * ATTN_GQA + r + 1) * ATTN_HEAD_DIM]
                 for r in range(ATTN_GQA)], axis=0)
            s = _dot_nt(kcat[keys, ks], q_stack)
            scores.append(jnp.concatenate([s[:ATTN_BLOCK] + edge_bias[0], s[ATTN_BLOCK:2 * ATTN_BLOCK],
                                           s[2 * ATTN_BLOCK:] + edge_bias[1]], axis=0))
    probs = []
    for idx, s in enumerate(scores):
        kv = idx % ATTN_KV_HEADS
        sink = sink_ref[kv:kv + 1, :]
        m = jnp.maximum(jnp.max(s, axis=0, keepdims=True), sink)
        p = jnp.exp2(s - m)
        denom = jnp.sum(p, axis=0, keepdims=True) + jnp.exp2(sink - m)
        probs.append((p.astype(BF16), 1.0 / denom))
    for j in range(SWA_BLOCKS):
        keys = slice(j * ATTN_BLOCK, (j + 3) * ATTN_BLOCK)
        outs = []
        for kv in range(ATTN_KV_HEADS):
            ks = slice(kv * ATTN_HEAD_DIM, (kv + 1) * ATTN_HEAD_DIM)
            p16, inv = probs[j * ATTN_KV_HEADS + kv]
            o_t = _dot(vcat_t[ks, keys], p16) * inv
            outs.extend(o_t[:, r * ATTN_BLOCK:(r + 1) * ATTN_BLOCK] for r in range(ATTN_GQA))
        o_ref[j * ATTN_BLOCK:(j + 1) * ATTN_BLOCK, :] = jnp.concatenate(outs, axis=0).T.astype(BF16)


def _swa(q, kv, sink_rows, bsz, nb):
    steps = nb // SWA_BLOCKS
    rows = SWA_BLOCKS * ATTN_BLOCK
    cur = lambda b, n: (b * steps + n, 0)
    prev = lambda b, n: (b * nb + jnp.maximum(SWA_BLOCKS * n - 1, 0), 0)
    nxt = lambda b, n: (b * nb + jnp.minimum(SWA_BLOCKS * (n + 1), nb - 1), 0)
    return pl.pallas_call(
        _swa_kernel,
        grid=(bsz, steps),
        in_specs=[pl.BlockSpec((rows, ATTN_WIDTH), cur),
                  pl.BlockSpec((ATTN_BLOCK, 2 * ATTN_KV_WIDTH), prev),
                  pl.BlockSpec((rows, 2 * ATTN_KV_WIDTH), cur),
                  pl.BlockSpec((ATTN_BLOCK, 2 * ATTN_KV_WIDTH), nxt),
                  _resident((8, ATTN_GQA * ATTN_BLOCK))],
        out_specs=pl.BlockSpec((rows, ATTN_WIDTH), cur),
        out_shape=jax.ShapeDtypeStruct(q.shape, BF16),
        compiler_params=_params(2),
        name="swa",
    )(q, kv, kv, kv, sink_rows)


def _mem_kv_kernel(m_ref, g_ref, w_ref, o_ref):
    h = _rms(m_ref[...], g_ref[...]).astype(BF16)
    o_ref[...] = _dot(h, w_ref[...]).astype(BF16)


def _mem_kv(mem2, g, w_kv):
    t = mem2.shape[0]
    tm = 256
    row = lambda i: (i, 0)
    return pl.pallas_call(
        _mem_kv_kernel,
        grid=(t // tm,),
        in_specs=[pl.BlockSpec((tm, D_MODEL), row),
                  _resident((1, D_MODEL)),
                  _resident((D_MODEL, 2 * D_MODEL))],
        out_specs=pl.BlockSpec((tm, 2 * D_MODEL), row),
        out_shape=jax.ShapeDtypeStruct((t, 2 * D_MODEL), BF16),
        compiler_params=_params(1),
        name="mem_kv",
    )(mem2, g, w_kv)


def _merge_xattn_kernel(x_ref, y_ref, a_ref, gate_ref, kv_ref, ws_ref, wa_ref, wo_ref, g_ref, wq_ref, wxo_ref,
                        o_ref, ctx_ref):
    bs = _dot(y_ref[...], ws_ref[...])
    ba = _dot(a_ref[...], wa_ref[...])
    gate = jax.nn.sigmoid(gate_ref[...].astype(F32))
    mix = (gate[:, :D_MODEL] * bs + gate[:, D_MODEL:] * ba).astype(BF16)
    x = x_ref[...] + _dot(mix, wo_ref[...])

    h = _rms(x, g_ref[...]).astype(BF16)
    q = (_dot(h, wq_ref[...]) * (XATTN_HEAD_DIM ** -0.5 * LOG2_E)).astype(BF16)
    heads = [slice(hd * XATTN_HEAD_DIM, (hd + 1) * XATTN_HEAD_DIM) for hd in range(XATTN_HEADS)]
    scores = [_dot_nt(q[:, ks], kv_ref[:, ks]) for ks in heads]
    probs = []
    for s in scores:
        p = jnp.exp2(s - jnp.max(s, axis=-1, keepdims=True))
        probs.append((p.astype(BF16), 1.0 / jnp.sum(p, axis=-1, keepdims=True)))
    for ks, (p16, inv) in zip(heads, probs):
        vs = slice(D_MODEL + ks.start, D_MODEL + ks.stop)
        ctx_ref[:, ks] = (_dot(p16, kv_ref[:, vs]) * inv).astype(BF16)
    o_ref[...] = x + _dot(ctx_ref[...], wxo_ref[...])


def _merge_xattn(x2, y, attn, gates, kv, ws, wa, wo, g, wq, wxo, seq, mem_len):
    t = x2.shape[0]
    tm = 512
    per_seq = seq // tm
    row = lambda i: (i, 0)
    return pl.pallas_call(
        _merge_xattn_kernel,
        grid=(t // tm,),
        in_specs=[pl.BlockSpec((tm, D_MODEL), row),
                  pl.BlockSpec((tm, SSD_INNER), row),
                  pl.BlockSpec((tm, ATTN_WIDTH), row),
                  pl.BlockSpec((tm, 2 * D_MODEL), row),
                  pl.BlockSpec((mem_len, 2 * D_MODEL), lambda i: (i // per_seq, 0)),
                  _resident((SSD_INNER, D_MODEL)),
                  _resident((ATTN_WIDTH, D_MODEL)),
                  _resident((D_MODEL, D_MODEL)),
                  _resident((1, D_MODEL)),
                  _resident((D_MODEL, D_MODEL)),
                  _resident((D_MODEL, D_MODEL))],
        out_specs=pl.BlockSpec((tm, D_MODEL), row),
        out_shape=jax.ShapeDtypeStruct((t, D_MODEL), F32),
        scratch_shapes=[pltpu.VMEM((tm, D_MODEL), BF16)],
        compiler_params=_params(1),
        name="merge_xattn",
    )(x2, y, attn, gates, kv, ws, wa, wo, g, wq, wxo)


MXU_TILE = 256
FFN_SPLIT = (FFN_HIDDEN // MXU_TILE + 1) // 2 * MXU_TILE


def _ffn_kernel(x_ref, g_ref, wi_ref, wd_ref, gf_ref, o_ref):
    x = x_ref[...]
    h = _rms(x, g_ref[...]).astype(BF16)
    acc = x
    for lo, hi in ((0, FFN_SPLIT), (FFN_SPLIT, FFN_HIDDEN)):
        gate = _dot(h, wi_ref[:, lo:hi])
        up = _dot(h, wi_ref[:, FFN_HIDDEN + lo:FFN_HIDDEN + hi])
        act = (_silu(gate) * up).astype(BF16)
        acc = acc + _dot(act, wd_ref[lo:hi, :])
    o_ref[...] = _rms(acc, gf_ref[...])


def _ffn(x2, g, wi, wd, gf):
    t = x2.shape[0]
    tm = 512
    row = lambda i: (i, 0)
    return pl.pallas_call(
        _ffn_kernel,
        grid=(t // tm,),
        in_specs=[pl.BlockSpec((tm, D_MODEL), row),
                  _resident((1, D_MODEL)),
                  _resident((D_MODEL, 2 * FFN_HIDDEN)),
                  _resident((FFN_HIDDEN, D_MODEL)),
                  _resident((1, D_MODEL))],
        out_specs=pl.BlockSpec((tm, D_MODEL), row),
        out_shape=jax.ShapeDtypeStruct((t, D_MODEL), F32),
        compiler_params=_params(1),
        name="ffn",
    )(x2, g, wi, wd, gf)


def _pad_rows(a, rows):
    return jnp.pad(a, ((0, rows - a.shape[0]), (0, 0)))


def _pad_cols(a, cols):
    return jnp.pad(a, ((0, 0), (0, cols - a.shape[1])))


def _column_select_matrix():
    j = np.arange(LANES)[:, None]
    blk = np.arange(SSD_HPG * CHUNK)[None, :] // CHUNK
    e = (j == blk).astype(np.float32)
    return jnp.asarray(np.concatenate([e, e], axis=0), dtype=BF16)


def _select_matrix(slot):
    j = np.arange(LANES)[:, None]
    ch = np.arange(SSD_INNER)[None, :]
    e = (j == slot * SSD_HEADS + ch // SSD_HEAD_DIM).astype(np.float32)
    return jnp.asarray(np.concatenate([e, e], axis=0), dtype=BF16)


def _layer(x2, mem2, bsz, seq, mem_len, norm_mix_g, w_in, conv_w, conv_b, dt_bias_fwd, dt_bias_bwd,
           a_log_fwd, a_log_bwd, d_skip, ssd_norm_g, attn_sink, w_branch_ssd, w_branch_attn, w_mix_out,
           norm_xattn_g, norm_mem_g, w_xattn_q, w_xattn_kv, w_xattn_out, norm_ffn_g, w_ffn_in, w_ffn_out,
           final_g):
    nc = seq // CHUNK
    w_a = w_in[:, :WIDTH_A].astype(BF16)
    w_b = w_in[:, START_B:].astype(BF16)

    half = ATTN_HEAD_DIM // 2
    inv_freq = ROPE_THETA ** (-jnp.arange(half, dtype=F32) / half)
    ang = jnp.arange(seq, dtype=F32)[:, None] * inv_freq[None]
    cos = jnp.tile(jnp.cos(ang), (1, LANES // half))
    sin_signed = jnp.tile(jnp.concatenate([-jnp.sin(ang), jnp.sin(ang)], axis=1), (1, LANES // ATTN_HEAD_DIM))

    cw = _pad_rows(conv_w, 8)
    z, act, q, kv_attn, gates, dt = _in_proj(x2, norm_mix_g[None], w_a, w_b, cos, sin_signed, cw, conv_b[None], seq)

    hp = _pad_rows(jnp.stack([_pad_cols(jnp.concatenate([dt_bias_fwd, dt_bias_bwd])[None], LANES)[0],
                              _pad_cols(jnp.concatenate([a_log_fwd, a_log_bwd])[None], LANES)[0]]), 8)
    e2f = _select_matrix(0)
    e2b = _select_matrix(1)
    hb = _ssd_bwd(act, dt, hp, e2b, bsz, nc)
    dskip = jnp.repeat(d_skip, SSD_HEAD_DIM)[None]
    y = _ssd_main(act, dt, z, hb, hp, dskip, ssd_norm_g[None], e2f, _column_select_matrix(), bsz, nc)

    sink_rows = _pad_rows(jnp.repeat(attn_sink * LOG2_E, ATTN_BLOCK).reshape(ATTN_KV_HEADS, ATTN_GQA * ATTN_BLOCK), 8)
    attn = _swa(q, kv_attn, sink_rows, bsz, seq // ATTN_BLOCK)

    kv = _mem_kv(mem2, norm_mem_g[None], w_xattn_kv.astype(BF16))
    x2 = _merge_xattn(x2, y, attn, gates, kv, w_branch_ssd.astype(BF16), w_branch_attn.astype(BF16),
                      w_mix_out.astype(BF16), norm_xattn_g[None], w_xattn_q.astype(BF16),
                      w_xattn_out.astype(BF16), seq, mem_len)

    return _ffn(x2, norm_ffn_g[None], w_ffn_in.astype(BF16), w_ffn_out.astype(BF16), final_g)


def kernel(x, mem, norm_mix_g, w_in, conv_w, conv_b, dt_bias_fwd, dt_bias_bwd, a_log_fwd, a_log_bwd, d_skip,
           ssd_norm_g, attn_sink, w_branch_ssd, w_branch_attn, w_mix_out, norm_xattn_g, norm_mem_g, w_xattn_q,
           w_xattn_kv, w_xattn_out, norm_ffn_g, w_ffn_in, w_ffn_out, norm_final_g):
    bsz, seq, _ = x.shape
    mem_len = mem.shape[1]
    assert w_in.shape[0] == 1, "single-layer stack expected"
    layer = 0
    x2 = x.reshape(bsz * seq, D_MODEL)
    mem2 = mem.reshape(bsz * mem_len, D_MODEL)
    out = _layer(x2, mem2, bsz, seq, mem_len, norm_mix_g[layer], w_in.reshape(D_MODEL, -1), conv_w[layer], conv_b[layer],
                 dt_bias_fwd[layer], dt_bias_bwd[layer], a_log_fwd[layer], a_log_bwd[layer], d_skip[layer],
                 ssd_norm_g[layer], attn_sink[layer], w_branch_ssd[layer], w_branch_attn[layer],
                 w_mix_out[layer], norm_xattn_g[layer], norm_mem_g[layer], w_xattn_q[layer],
                 w_xattn_kv[layer], w_xattn_out[layer], norm_ffn_g[layer], w_ffn_in[layer],
                 w_ffn_out[layer], norm_final_g[None])
    return out.reshape(bsz, seq, D_MODEL)
```

```python
import functools

import jax
import jax.numpy as jnp
import numpy as np
from jax import lax
from jax.experimental import pallas as pl
from jax.experimental.pallas import tpu as pltpu

F32 = jnp.float32
BF16 = jnp.bfloat16

D_MODEL = 1024
EPS = 1e-6
SSD_INNER = 2048
SSD_HEAD_DIM = 64
SSD_HEADS = 32
SSD_GROUPS = 4
SSD_HPG = 8
SSD_STATE = 128
SSD_CONV = 5
CHUNK = 128
BWD_CHUNKS = 8
MAIN_CHUNKS = 2
DECAY_ROWS = 16
WIDE_ROWS = 4 * CHUNK + DECAY_ROWS
SSD_BC = SSD_GROUPS * SSD_STATE
SSD_XBC = SSD_INNER + 2 * SSD_BC
GROUP_W = SSD_HPG * SSD_HEAD_DIM
ATTN_HEAD_DIM = 64
ATTN_KV_HEADS = 4
ATTN_GQA = 4
ATTN_WIDTH = 1024
ATTN_KV_WIDTH = 256
ATTN_BLOCK = 128
SWA_BLOCKS = 4
ROPE_THETA = 10000.0
XATTN_HEADS = 4
XATTN_HEAD_DIM = 256
FFN_HIDDEN = 2816

LANES = 128
NEG_BIG = -1e30
LOG2_E = 1.4426950408889634

OFF_Z = 0
OFF_XBC = OFF_Z + SSD_INNER
OFF_DT = OFF_XBC + SSD_XBC
WIDTH_A = OFF_DT + LANES
START_B = OFF_DT + 2 * SSD_HEADS
OFF_Q = 0
OFF_K = OFF_Q + ATTN_WIDTH
OFF_G = OFF_K + 2 * ATTN_KV_WIDTH
WIDTH_B = OFF_G + 2 * D_MODEL

VMEM_LIMIT = 56 * 1024 * 1024


def _params(n_axes, flags=None):
    return pltpu.CompilerParams(dimension_semantics=("arbitrary",) * n_axes,
                                vmem_limit_bytes=VMEM_LIMIT, flags=flags)


def _resident(shape):
    nd = len(shape)
    return pl.BlockSpec(shape, lambda *_: (0,) * nd, pipeline_mode=pl.Buffered(1))


def _rms(x, g):
    return x * lax.rsqrt(jnp.mean(x * x, axis=-1, keepdims=True) + EPS) * g


def _silu(x):
    h = 0.5 * x
    return h + h * jnp.tanh(h)


def _dot(a, b):
    return jnp.dot(a, b, preferred_element_type=F32)


def _dot_nt(a, b):
    return lax.dot_general(a, b, (((1,), (1,)), ((), ())), preferred_element_type=F32)


def _rotary(u, cos, sin_signed):
    n = u.shape[1]
    lane = lax.broadcasted_iota(jnp.int32, u.shape, 1)
    first_half = (lane % ATTN_HEAD_DIM) < (ATTN_HEAD_DIM // 2)
    partner = jnp.where(first_half, pltpu.roll(u, n - ATTN_HEAD_DIM // 2, axis=1),
                        pltpu.roll(u, ATTN_HEAD_DIM // 2, axis=1))
    reps = n // LANES
    cos_t = jnp.concatenate([cos] * reps, axis=1) if reps > 1 else cos
    sin_t = jnp.concatenate([sin_signed] * reps, axis=1) if reps > 1 else sin_signed
    return u * cos_t + partner * sin_t


def _in_proj_kernel(x_ref, xp_ref, xn_ref, g_ref, wa_ref, wb_ref, cos_ref, sin_ref, cw_ref, cb_ref,
                    z_ref, act_ref, q_ref, kv_ref, gate_ref, dt_ref, ext_ref, *, per_seq):
    i = pl.program_id(0)
    tm = x_ref.shape[0]
    g = g_ref[...]
    h_ext = _rms(jnp.concatenate([x_ref[...], xp_ref[...], xn_ref[...]], axis=0), g).astype(BF16)
    step = 256
    rblk = 128
    row_blocks = [slice(r, r + rblk) for r in range(0, tm, rblk)]
    has_prev = (i % per_seq) > 0
    has_next = (i % per_seq) < per_seq - 1

    def proj(rs, w_ref, off, width):
        return _dot(h_ext[rs], w_ref[:, off:off + width])

    def xbc_proj(c):
        cs = slice(c, c + step)
        for rs in row_blocks[:-1]:
            ext_ref[8 + rs.start:8 + rs.stop, cs] = proj(rs, wa_ref, OFF_XBC + c, step)
        last = row_blocks[-1]
        u = proj(slice(last.start, tm + 16), wa_ref, OFF_XBC + c, step)
        ext_ref[8 + last.start:8 + tm, cs] = u[:rblk]
        ext_ref[0:8, cs] = jnp.where(has_prev, u[rblk:rblk + 8], 0.0)
        ext_ref[8 + tm:16 + tm, cs] = jnp.where(has_next, u[rblk + 8:], 0.0)

    assert SSD_CONV == 5

    def conv_silu(c):
        cs = slice(c, c + step)
        e = ext_ref[:, cs]
        rows = e.shape[0]
        tap = lambda k: cw_ref[k:k + 1, cs] * e
        down = lambda a: pltpu.roll(a, 1, axis=0)
        up = lambda a: pltpu.roll(a, rows - 1, axis=0)
        left = down(down(tap(0)) + tap(1)) + tap(2)
        right = up(up(tap(4)) + tap(3))
        acc = (left + right)[8:8 + tm] + cb_ref[:, cs]
        act_ref[:, cs] = _silu(acc).astype(BF16)

    def z_proj(c):
        for rs in row_blocks:
            z_ref[rs, c:c + step] = proj(rs, wa_ref, OFF_Z + c, step).astype(BF16)

    def q_proj(c):
        for rs in row_blocks:
            q = _rotary(proj(rs, wb_ref, OFF_Q + c, step), cos_ref[rs, :], sin_ref[rs, :])
            q_ref[rs, c:c + step] = (q * (ATTN_HEAD_DIM ** -0.5 * LOG2_E)).astype(BF16)

    def kv_proj(_):
        for rs in row_blocks:
            kv = proj(rs, wb_ref, OFF_K, 2 * ATTN_KV_WIDTH)
            kv_ref[rs, :ATTN_KV_WIDTH] = _rotary(kv[:, :ATTN_KV_WIDTH], cos_ref[rs, :], sin_ref[rs, :]).astype(BF16)
            kv_ref[rs, ATTN_KV_WIDTH:] = kv[:, ATTN_KV_WIDTH:].astype(BF16)
            dt_lane = lax.broadcasted_iota(jnp.int32, (rblk, LANES), 1)
            dt_ref[rs, :] = jnp.where(dt_lane < 2 * SSD_HEADS, proj(rs, wa_ref, OFF_DT, LANES), 0.0)

    def gate_proj(c):
        for rs in row_blocks:
            gate_ref[rs, c:c + step] = proj(rs, wb_ref, OFF_G + c, step).astype(BF16)

    mxu_tasks = ([(z_proj, c) for c in range(0, SSD_INNER, step)]
                 + [(q_proj, c) for c in range(0, ATTN_WIDTH, step)] + [(kv_proj, 0)]
                 + [(gate_proj, c) for c in range(0, 2 * D_MODEL, step)])
    conv_chunks = list(range(0, SSD_XBC, step))
    xbc_proj(conv_chunks[0])
    for j, c in enumerate(conv_chunks):
        if j + 1 < len(conv_chunks):
            xbc_proj(conv_chunks[j + 1])
        for fn, arg in mxu_tasks[2 * j:2 * j + 2]:
            fn(arg)
        conv_silu(c)
    for fn, arg in mxu_tasks[2 * len(conv_chunks):]:
        fn(arg)


def _in_proj(x2, g, w_a, w_b, cos, sin_signed, conv_w, conv_b, seq):
    t = x2.shape[0]
    tm = 512
    per_seq = seq // tm
    halo_per_tile = tm // 8
    last_halo = t // 8 - 1
    row = lambda i: (i, 0)
    pos = lambda i: (i % per_seq, 0)
    outs = [(SSD_INNER, BF16), (SSD_XBC, BF16), (ATTN_WIDTH, BF16), (2 * ATTN_KV_WIDTH, BF16),
            (2 * D_MODEL, BF16), (LANES, F32)]
    return pl.pallas_call(
        functools.partial(_in_proj_kernel, per_seq=per_seq),
        grid=(t // tm,),
        in_specs=[pl.BlockSpec((tm, D_MODEL), row),
                  pl.BlockSpec((8, D_MODEL), lambda i: (jnp.maximum(i * halo_per_tile - 1, 0), 0)),
                  pl.BlockSpec((8, D_MODEL), lambda i: (jnp.minimum((i + 1) * halo_per_tile, last_halo), 0)),
                  _resident((1, D_MODEL)),
                  _resident((D_MODEL, WIDTH_A)),
                  _resident((D_MODEL, WIDTH_B)),
                  pl.BlockSpec((tm, LANES), pos),
                  pl.BlockSpec((tm, LANES), pos),
                  _resident((8, SSD_XBC)),
                  _resident((1, SSD_XBC))],
        out_specs=[pl.BlockSpec((tm, w), row) for w, _ in outs],
        out_shape=[jax.ShapeDtypeStruct((t, w), d) for w, d in outs],
        scratch_shapes=[pltpu.VMEM((tm + 16, SSD_XBC), F32)],
        compiler_params=_params(1),
        name="in_proj",
    )(x2, x2, x2, g, w_a, w_b, cos, sin_signed, conv_w, conv_b)


def _cumsum_rows(a):
    row = lax.broadcasted_iota(jnp.int32, a.shape, 0)
    v = a
    k = 1
    while k < a.shape[0]:
        v = v + jnp.where(row >= k, pltpu.roll(v, k, axis=0), 0.0)
        k *= 2
    return v


def _expand(v, e2_ref):
    hi = v.astype(BF16)
    lo = (v - hi.astype(F32)).astype(BF16)
    return _dot(jnp.concatenate([hi, lo], axis=1), e2_ref[...])


def _softplus(x):
    return jnp.maximum(x, 0.0) + jnp.log1p(jnp.exp(-jnp.abs(x)))


def _head_scalars(dt_raw, hp_ref):
    dtv = _softplus(dt_raw + hp_ref[0:1, :])
    a = dtv * (-LOG2_E * jnp.exp(hp_ref[1:2, :]))
    return dtv, a


def _chunk_scalars(dt_raw, hp_ref):
    dtv, a = _head_scalars(dt_raw, hp_ref)
    incl = _cumsum_rows(a)
    return dtv, incl, incl - a


def _ssd_bwd_kernel(act_ref, dt_ref, hp_ref, e2b_ref, hb_ref, state_ref):
    c = pl.program_id(1)

    @pl.when(c == 0)
    def _():
        state_ref[...] = jnp.zeros_like(state_ref)

    scales, decays = [], []
    for j in range(BWD_CHUNKS):
        dtv, incl, excl = _chunk_scalars(dt_ref[j * CHUNK:(j + 1) * CHUNK, :], hp_ref)
        scales.append(jnp.exp2(excl) * dtv)
        decays.append(jnp.exp2(incl[CHUNK - DECAY_ROWS:, :]))
    wide = _expand(jnp.concatenate(scales + decays, axis=0), e2b_ref)

    for j in reversed(range(BWD_CHUNKS)):
        rows = slice(j * CHUNK, (j + 1) * CHUNK)
        hb_ref[j] = state_ref[...].astype(BF16)
        xs = act_ref[rows, :SSD_INNER].astype(F32)
        bm = act_ref[rows, SSD_INNER:].astype(F32)
        xw = (xs * wide[rows]).astype(BF16)
        last = BWD_CHUNKS * CHUNK + (j + 1) * DECAY_ROWS - 1
        decay = wide[last:last + 1, :]
        for g in range(SSD_GROUPS):
            gs = slice(g * GROUP_W, (g + 1) * GROUP_W)
            bt = bm[:, g * SSD_STATE:(g + 1) * SSD_STATE].T.astype(BF16)
            state_ref[:, gs] = state_ref[:, gs] * decay[:, gs] + _dot(bt, xw[:, gs])


def _ssd_bwd(act, dt, hp, e2b, bsz, nc):
    width = SSD_INNER + SSD_BC
    steps = nc // BWD_CHUNKS
    block_of = lambda b, c: b * steps + (steps - 1 - c)
    return pl.pallas_call(
        _ssd_bwd_kernel,
        grid=(bsz, steps),
        in_specs=[pl.BlockSpec((BWD_CHUNKS * CHUNK, width), lambda b, c: (block_of(b, c), 0)),
                  pl.BlockSpec((BWD_CHUNKS * CHUNK, LANES), lambda b, c: (block_of(b, c), 0)),
                  _resident((8, LANES)),
                  _resident((2 * LANES, SSD_INNER))],
        out_specs=pl.BlockSpec((BWD_CHUNKS, SSD_STATE, SSD_INNER), lambda b, c: (block_of(b, c), 0, 0)),
        out_shape=jax.ShapeDtypeStruct((bsz * nc, SSD_STATE, SSD_INNER), BF16),
        scratch_shapes=[pltpu.VMEM((SSD_STATE, SSD_INNER), F32)],
        compiler_params=_params(2),
        name="ssd_bwd",
    )(act, dt, hp, e2b)


def _ssd_main_kernel(act_ref, dt_ref, z_ref, hb_ref, hp_ref, dskip_ref, ng_ref, e2f_ref, sel_ref,
                     y_ref, state_ref, ybuf_ref):
    c = pl.program_id(1)

    @pl.when(c == 0)
    def _():
        state_ref[...] = jnp.zeros_like(state_ref)

    lane = lax.broadcasted_iota(jnp.int32, (CHUNK, LANES), 1)
    row = lax.broadcasted_iota(jnp.int32, (CHUNK, CHUNK), 0)
    col = lax.broadcasted_iota(jnp.int32, (CHUNK, CHUNK), 1)
    groups = range(SSD_GROUPS)
    to_fwd_slot = lambda v: pltpu.roll(v, LANES - SSD_HEADS, axis=1)
    prep, wide_in, col_in = [], [], []
    for j in range(MAIN_CHUNKS):
        rows = slice(j * CHUNK, (j + 1) * CHUNK)
        dtv, incl, excl = _chunk_scalars(dt_ref[rows, :], hp_ref)
        tot = incl[CHUNK - 1:CHUNK, :]
        cbs = [_dot_nt(act_ref[rows, SSD_INNER + SSD_BC + g * SSD_STATE:SSD_INNER + SSD_BC + (g + 1) * SSD_STATE],
                       act_ref[rows, SSD_INNER + g * SSD_STATE:SSD_INNER + (g + 1) * SSD_STATE]) for g in groups]
        lane_group = (lane % SSD_HEADS) // SSD_HPG
        cb_diag = jnp.zeros_like(dtv)
        for g in groups:
            dg = jnp.sum(jnp.where(row == col, cbs[g], 0.0), axis=1, keepdims=True)
            cb_diag = jnp.where(lane_group == g, dg, cb_diag)
        wide_in += [jnp.exp2(incl),
                    jnp.exp2(tot - incl) * dtv,
                    to_fwd_slot(jnp.exp2(tot - excl)),
                    to_fwd_slot(cb_diag * dtv),
                    jnp.exp2(incl[CHUNK - DECAY_ROWS:, :])]
        log_dt = jnp.log2(dtv)
        key_f = (incl - log_dt).T
        key_b = (excl + log_dt).T
        vmix = jnp.where(lane < SSD_HEADS, incl, excl)
        hi = vmix.astype(BF16).astype(F32)
        lo = vmix - hi
        for shift in [d * SSD_HEADS + g * SSD_HPG for d in range(2) for g in groups]:
            sh = (LANES - shift) % LANES
            col_in.append(jnp.concatenate([pltpu.roll(hi, sh, axis=1) if sh else hi,
                                           pltpu.roll(lo, sh, axis=1) if sh else lo], axis=1).astype(BF16))
        prep.append((cbs, key_f, key_b))
    wide = _expand(jnp.concatenate(wide_in, axis=0), e2f_ref)
    cols = _dot(jnp.concatenate(col_in, axis=0), sel_ref[...])

    for j in range(MAIN_CHUNKS):
        w0 = j * WIDE_ROWS
        c0 = j * 2 * SSD_GROUPS * CHUNK
        _ssd_main_chunk(j, prep[j], wide[w0:w0 + WIDE_ROWS], cols[c0:c0 + 2 * SSD_GROUPS * CHUNK],
                        act_ref, z_ref, hb_ref, dskip_ref, ng_ref, y_ref, state_ref, ybuf_ref.at[j])


def _ssd_main_chunk(j, prep, wide, cols, act_ref, z_ref, hb_ref, dskip_ref, ng_ref, y_ref, state_ref, ybuf_ref):
    rows = slice(j * CHUNK, (j + 1) * CHUNK)
    cbs, key_f, key_b = prep
    scale_f = wide[0:CHUNK]
    state_scale = wide[CHUNK:2 * CHUNK]
    scale_b = wide[2 * CHUNK:3 * CHUNK]
    self_b = wide[3 * CHUNK:4 * CHUNK]
    decay = wide[WIDE_ROWS - 1:WIDE_ROWS, :]
    xs16 = act_ref[rows, :SSD_INNER]
    xs = xs16.astype(F32)
    bm16 = act_ref[rows, SSD_INNER:SSD_INNER + SSD_BC]
    cm16 = act_ref[rows, SSD_INNER + SSD_BC:]

    row = lax.broadcasted_iota(jnp.int32, (CHUNK, CHUNK), 0)
    col = lax.broadcasted_iota(jnp.int32, (CHUNK, CHUNK), 1)
    lower = row >= col
    first_head = lax.broadcasted_iota(jnp.int32, (CHUNK, 2 * SSD_HEAD_DIM), 1) < SSD_HEAD_DIM

    groups = range(SSD_GROUPS)
    ns = [slice(g * SSD_STATE, (g + 1) * SSD_STATE) for g in groups]
    gs = [slice(g * GROUP_W, (g + 1) * GROUP_W) for g in groups]
    xw = (xs * state_scale).astype(BF16)
    st = [state_ref[:, gs[g]] for g in groups]
    off_f = [_dot(cm16[:, ns[g]], st[g].astype(BF16)) for g in groups]
    off_b = [_dot(cm16[:, ns[g]], hb_ref[j, :, gs[g]]) for g in groups]
    for g in groups:
        bt = bm16[:, ns[g]].astype(F32).T.astype(BF16)
        state_ref[:, gs[g]] = st[g] * decay[:, gs[g]] + _dot(bt, xw[:, gs[g]])

    for g in groups:
        col_f = cols[g * CHUNK:(g + 1) * CHUNK]
        col_b = cols[(SSD_GROUPS + g) * CHUNK:(SSD_GROUPS + g + 1) * CHUNK]
        for pair in range(SSD_HPG // 2):
            ms = []
            for r in (2 * pair, 2 * pair + 1):
                h = g * SSD_HPG + r
                hb_col = SSD_HEADS + h
                rs = slice(r * CHUNK, (r + 1) * CHUNK)
                seg_f = col_f[:, rs] - key_f[h:h + 1, :]
                seg_b = key_b[hb_col:hb_col + 1, :] - col_b[:, rs]
                ms.append((cbs[g] * jnp.exp2(jnp.where(lower, seg_f, seg_b))).astype(BF16))
            ps = slice((g * SSD_HPG + 2 * pair) * SSD_HEAD_DIM, (g * SSD_HPG + 2 * pair + 2) * SSD_HEAD_DIM)
            x2 = xs16[:, ps]
            zero = jnp.zeros_like(x2)
            rhs = jnp.concatenate([jnp.where(first_head, x2, zero), jnp.where(first_head, zero, x2)], axis=0)
            ybuf_ref[:, ps] = _dot(jnp.concatenate(ms, axis=1), rhs)

    zf = z_ref[rows, :].astype(F32)
    zgate = _silu(zf)
    for g in groups:
        y = (ybuf_ref[:, gs[g]] + scale_f[:, gs[g]] * off_f[g] + scale_b[:, gs[g]] * off_b[g]
             + (dskip_ref[:, gs[g]] + self_b[:, gs[g]]) * xs[:, gs[g]])
        v = y * zgate[:, gs[g]]
        v = v * lax.rsqrt(jnp.mean(v * v, axis=-1, keepdims=True) + EPS)
        y_ref[rows, gs[g]] = (v * ng_ref[:, gs[g]]).astype(BF16)


def _ssd_main(act, dt, z, hb, hp, dskip, norm_g, e2f, sel, bsz, nc):
    steps = nc // MAIN_CHUNKS
    rows = MAIN_CHUNKS * CHUNK
    block_of = lambda b, c: b * steps + c
    return pl.pallas_call(
        _ssd_main_kernel,
        grid=(bsz, steps),
        in_specs=[pl.BlockSpec((rows, SSD_XBC), lambda b, c: (block_of(b, c), 0)),
                  pl.BlockSpec((rows, LANES), lambda b, c: (block_of(b, c), 0)),
                  pl.BlockSpec((rows, SSD_INNER), lambda b, c: (block_of(b, c), 0)),
                  pl.BlockSpec((MAIN_CHUNKS, SSD_STATE, SSD_INNER), lambda b, c: (block_of(b, c), 0, 0)),
                  _resident((8, LANES)),
                  _resident((1, SSD_INNER)),
                  _resident((1, SSD_INNER)),
                  _resident((2 * LANES, SSD_INNER)),
                  _resident((2 * LANES, SSD_HPG * CHUNK))],
        out_specs=pl.BlockSpec((rows, SSD_INNER), lambda b, c: (block_of(b, c), 0)),
        out_shape=jax.ShapeDtypeStruct((bsz * nc * CHUNK, SSD_INNER), BF16),
        scratch_shapes=[pltpu.VMEM((SSD_STATE, SSD_INNER), F32),
                        pltpu.VMEM((MAIN_CHUNKS, CHUNK, SSD_INNER), F32)],
        compiler_params=_params(2),
        name="ssd_main",
    )(act, dt, z, hb, hp, dskip, norm_g, e2f, sel)


def _swa_kernel(q_ref, kvp_ref, kvc_ref, kvn_ref, sink_ref, o_ref):
    n = pl.program_id(1)
    last = pl.num_programs(1) - 1
    kj = lax.broadcasted_iota(jnp.int32, (3 * ATTN_BLOCK, ATTN_BLOCK), 0)
    qi = lax.broadcasted_iota(jnp.int32, (3 * ATTN_BLOCK, ATTN_BLOCK), 1)
    rel = kj - qi
    band = (rel >= 0) & (rel <= 2 * ATTN_BLOCK)
    kvcat = jnp.concatenate([kvp_ref[...], kvc_ref[...], kvn_ref[...]], axis=0)
    kcat = kvcat[:, :ATTN_KV_WIDTH]
    vcat_t = kvcat[:, ATTN_KV_WIDTH:].astype(F32).T.astype(BF16)
    scores = []
    for j in range(SWA_BLOCKS):
        valid = band
        if j == 0:
            valid = valid & ((kj >= ATTN_BLOCK) | (n > 0))
        if j == SWA_BLOCKS - 1:
            valid = valid & ((kj < 2 * ATTN_BLOCK) | (n < last))
        bias = jnp.concatenate([jnp.where(valid, 0.0, NEG_BIG)] * ATTN_GQA, axis=1)
        edge_bias = (bias[:ATTN_BLOCK], bias[2 * ATTN_BLOCK:])
        keys = slice(j * ATTN_BLOCK, (j + 3) * ATTN_BLOCK)
        qrows = slice(j * ATTN_BLOCK, (j + 1) * ATTN_BLOCK)
        for kv in range(ATTN_KV_HEADS):
            ks = slice(kv * ATTN_HEAD_DIM, (kv + 1) * ATTN_HEAD_DIM)
            q_stack = jnp.concatenate(
                [q_ref[qrows, (kv * ATTN_GQA + r) * ATTN_HEAD_DIM:(kv * ATTN_GQA + r + 1) * ATTN_HEAD_DIM]
                 for r in range(ATTN_GQA)], axis=0)
            s = _dot_nt(kcat[keys, ks], q_stack)
            scores.append(jnp.concatenate([s[:ATTN_BLOCK] + edge_bias[0], s[ATTN_BLOCK:2 * ATTN_BLOCK],
                                           s[2 * ATTN_BLOCK:] + edge_bias[1]], axis=0))
    probs = []
    for idx, s in enumerate(scores):
        kv = idx % ATTN_KV_HEADS
        sink = sink_ref[kv:kv + 1, :]
        m = jnp.maximum(jnp.max(s, axis=0, keepdims=True), sink)
        p = jnp.exp2(s - m)
        denom = jnp.sum(p, axis=0, keepdims=True) + jnp.exp2(sink - m)
        probs.append((p.astype(BF16), 1.0 / denom))
    for j in range(SWA_BLOCKS):
        keys = slice(j * ATTN_BLOCK, (j + 3) * ATTN_BLOCK)
        outs = []
        for kv in range(ATTN_KV_HEADS):
            ks = slice(kv * ATTN_HEAD_DIM, (kv + 1) * ATTN_HEAD_DIM)
            p16, inv = probs[j * ATTN_KV_HEADS + kv]
            o_t = _dot(vcat_t[ks, keys], p16) * inv
            outs.extend(o_t[:, r * ATTN_BLOCK:(r + 1) * ATTN_BLOCK] for r in range(ATTN_GQA))
        o_ref[j * ATTN_BLOCK:(j + 1) * ATTN_BLOCK, :] = jnp.concatenate(outs, axis=0).T.astype(BF16)


def _swa(q, kv, sink_rows, bsz, nb):
    steps = nb // SWA_BLOCKS
    rows = SWA_BLOCKS * ATTN_BLOCK
    cur = lambda b, n: (b * steps + n, 0)
    prev = lambda b, n: (b * nb + jnp.maximum(SWA_BLOCKS * n - 1, 0), 0)
    nxt = lambda b, n: (b * nb + jnp.minimum(SWA_BLOCKS * (n + 1), nb - 1), 0)
    return pl.pallas_call(
        _swa_kernel,
        grid=(bsz, steps),
        in_specs=[pl.BlockSpec((rows, ATTN_WIDTH), cur),
                  pl.BlockSpec((ATTN_BLOCK, 2 * ATTN_KV_WIDTH), prev),
                  pl.BlockSpec((rows, 2 * ATTN_KV_WIDTH), cur),
                  pl.BlockSpec((ATTN_BLOCK, 2 * ATTN_KV_WIDTH), nxt),
                  _resident((8, ATTN_GQA * ATTN_BLOCK))],
        out_specs=pl.BlockSpec((rows, ATTN_WIDTH), cur),
        out_shape=jax.ShapeDtypeStruct(q.shape, BF16),
        compiler_params=_params(2),
        name="swa",
    )(q, kv, kv, kv, sink_rows)


def _mem_kv_kernel(m_ref, g_ref, w_ref, o_ref):
    h = _rms(m_ref[...], g_ref[...]).astype(BF16)
    o_ref[...] = _dot(h, w_ref[...]).astype(BF16)


def _mem_kv(mem2, g, w_kv):
    t = mem2.shape[0]
    tm = 256
    row = lambda i: (i, 0)
    return pl.pallas_call(
        _mem_kv_kernel,
        grid=(t // tm,),
        in_specs=[pl.BlockSpec((tm, D_MODEL), row),
                  _resident((1, D_MODEL)),
                  _resident((D_MODEL, 2 * D_MODEL))],
        out_specs=pl.BlockSpec((tm, 2 * D_MODEL), row),
        out_shape=jax.ShapeDtypeStruct((t, 2 * D_MODEL), BF16),
        compiler_params=_params(1),
        name="mem_kv",
    )(mem2, g, w_kv)


def _merge_xattn_kernel(x_ref, y_ref, a_ref, gate_ref, kv_ref, ws_ref, wa_ref, wo_ref, g_ref, wq_ref, wxo_ref,
                        o_ref, ctx_ref):
    bs = _dot(y_ref[...], ws_ref[...])
    ba = _dot(a_ref[...], wa_ref[...])
    gate = jax.nn.sigmoid(gate_ref[...].astype(F32))
    mix = (gate[:, :D_MODEL] * bs + gate[:, D_MODEL:] * ba).astype(BF16)
    x = x_ref[...] + _dot(mix, wo_ref[...])

    h = _rms(x, g_ref[...]).astype(BF16)
    q = (_dot(h, wq_ref[...]) * (XATTN_HEAD_DIM ** -0.5 * LOG2_E)).astype(BF16)
    heads = [slice(hd * XATTN_HEAD_DIM, (hd + 1) * XATTN_HEAD_DIM) for hd in range(XATTN_HEADS)]
    scores = [_dot_nt(q[:, ks], kv_ref[:, ks]) for ks in heads]
    probs = []
    for s in scores:
        p = jnp.exp2(s - jnp.max(s, axis=-1, keepdims=True))
        probs.append((p.astype(BF16), 1.0 / jnp.sum(p, axis=-1, keepdims=True)))
    for ks, (p16, inv) in zip(heads, probs):
        vs = slice(D_MODEL + ks.start, D_MODEL + ks.stop)
        ctx_ref[:, ks] = (_dot(p16, kv_ref[:, vs]) * inv).astype(BF16)
    o_ref[...] = x + _dot(ctx_ref[...], wxo_ref[...])


def _merge_xattn(x2, y, attn, gates, kv, ws, wa, wo, g, wq, wxo, seq, mem_len):
    t = x2.shape[0]
    tm = 512
    per_seq = seq // tm
    row = lambda i: (i, 0)
    return pl.pallas_call(
        _merge_xattn_kernel,
        grid=(t // tm,),
        in_specs=[pl.BlockSpec((tm, D_MODEL), row),
                  pl.BlockSpec((tm, SSD_INNER), row),
                  pl.BlockSpec((tm, ATTN_WIDTH), row),
                  pl.BlockSpec((tm, 2 * D_MODEL), row),
                  pl.BlockSpec((mem_len, 2 * D_MODEL), lambda i: (i // per_seq, 0)),
                  _resident((SSD_INNER, D_MODEL)),
                  _resident((ATTN_WIDTH, D_MODEL)),
                  _resident((D_MODEL, D_MODEL)),
                  _resident((1, D_MODEL)),
                  _resident((D_MODEL, D_MODEL)),
                  _resident((D_MODEL, D_MODEL))],
        out_specs=pl.BlockSpec((tm, D_MODEL), row),
        out_shape=jax.ShapeDtypeStruct((t, D_MODEL), F32),
        scratch_shapes=[pltpu.VMEM((tm, D_MODEL), BF16)],
        compiler_params=_params(1),
        name="merge_xattn",
    )(x2, y, attn, gates, kv, ws, wa, wo, g, wq, wxo)


MXU_TILE = 256
FFN_SPLIT = (FFN_HIDDEN // MXU_TILE + 1) // 2 * MXU_TILE


def _ffn_kernel(x_ref, g_ref, wi_ref, wd_ref, gf_ref, o_ref):
    x = x_ref[...]
    h = _rms(x, g_ref[...]).astype(BF16)
    acc = x
    for lo, hi in ((0, FFN_SPLIT), (FFN_SPLIT, FFN_HIDDEN)):
        gate = _dot(h, wi_ref[:, lo:hi])
        up = _dot(h, wi_ref[:, FFN_HIDDEN + lo:FFN_HIDDEN + hi])
        act = (_silu(gate) * up).astype(BF16)
        acc = acc + _dot(act, wd_ref[lo:hi, :])
    o_ref[...] = _rms(acc, gf_ref[...])


def _ffn(x2, g, wi, wd, gf):
    t = x2.shape[0]
    tm = 512
    row = lambda i: (i, 0)
    return pl.pallas_call(
        _ffn_kernel,
        grid=(t // tm,),
        in_specs=[pl.BlockSpec((tm, D_MODEL), row),
                  _resident((1, D_MODEL)),
                  _resident((D_MODEL, 2 * FFN_HIDDEN)),
                  _resident((FFN_HIDDEN, D_MODEL)),
                  _resident((1, D_MODEL))],
        out_specs=pl.BlockSpec((tm, D_MODEL), row),
        out_shape=jax.ShapeDtypeStruct((t, D_MODEL), F32),
        compiler_params=_params(1),
        name="ffn",
    )(x2, g, wi, wd, gf)


def _pad_rows(a, rows):
    return jnp.pad(a, ((0, rows - a.shape[0]), (0, 0)))


def _pad_cols(a, cols):
    return jnp.pad(a, ((0, 0), (0, cols - a.shape[1])))


def _column_select_matrix():
    j = np.arange(LANES)[:, None]
    blk = np.arange(SSD_HPG * CHUNK)[None, :] // CHUNK
    e = (j == blk).astype(np.float32)
    return jnp.asarray(np.concatenate([e, e], axis=0), dtype=BF16)


def _select_matrix(slot):
    j = np.arange(LANES)[:, None]
    ch = np.arange(SSD_INNER)[None, :]
    e = (j == slot * SSD_HEADS + ch // SSD_HEAD_DIM).astype(np.float32)
    return jnp.asarray(np.concatenate([e, e], axis=0), dtype=BF16)


def _layer(x2, mem2, bsz, seq, mem_len, norm_mix_g, w_in, conv_w, conv_b, dt_bias_fwd, dt_bias_bwd,
           a_log_fwd, a_log_bwd, d_skip, ssd_norm_g, attn_sink, w_branch_ssd, w_branch_attn, w_mix_out,
           norm_xattn_g, norm_mem_g, w_xattn_q, w_xattn_kv, w_xattn_out, norm_ffn_g, w_ffn_in, w_ffn_out,
           final_g):
    nc = seq // CHUNK
    w16 = w_in.astype(BF16)
    w_a = w16[:, :WIDTH_A]
    w_b = w16[:, START_B:]

    half = ATTN_HEAD_DIM // 2
    inv_freq = ROPE_THETA ** (-jnp.arange(half, dtype=F32) / half)
    ang = jnp.arange(seq, dtype=F32)[:, None] * inv_freq[None]
    cos = jnp.tile(jnp.cos(ang), (1, LANES // half))
    sin_signed = jnp.tile(jnp.concatenate([-jnp.sin(ang), jnp.sin(ang)], axis=1), (1, LANES // ATTN_HEAD_DIM))

    cw = _pad_rows(conv_w, 8)
    z, act, q, kv_attn, gates, dt = _in_proj(x2, norm_mix_g[None], w_a, w_b, cos, sin_signed, cw, conv_b[None], seq)

    hp = _pad_rows(jnp.stack([_pad_cols(jnp.concatenate([dt_bias_fwd, dt_bias_bwd])[None], LANES)[0],
                              _pad_cols(jnp.concatenate([a_log_fwd, a_log_bwd])[None], LANES)[0]]), 8)
    e2f = _select_matrix(0)
    e2b = _select_matrix(1)
    hb = _ssd_bwd(act, dt, hp, e2b, bsz, nc)
    dskip = jnp.repeat(d_skip, SSD_HEAD_DIM)[None]
    y = _ssd_main(act, dt, z, hb, hp, dskip, ssd_norm_g[None], e2f, _column_select_matrix(), bsz, nc)

    sink_rows = _pad_rows(jnp.repeat(attn_sink * LOG2_E, ATTN_BLOCK).reshape(ATTN_KV_HEADS, ATTN_GQA * ATTN_BLOCK), 8)
    attn = _swa(q, kv_attn, sink_rows, bsz, seq // ATTN_BLOCK)

    kv = _mem_kv(mem2, norm_mem_g[None], w_xattn_kv.astype(BF16))
    x2 = _merge_xattn(x2, y, attn, gates, kv, w_branch_ssd.astype(BF16), w_branch_attn.astype(BF16),
                      w_mix_out.astype(BF16), norm_xattn_g[None], w_xattn_q.astype(BF16),
                      w_xattn_out.astype(BF16), seq, mem_len)

    return _ffn(x2, norm_ffn_g[None], w_ffn_in.astype(BF16), w_ffn_out.astype(BF16), final_g)


def kernel(x, mem, norm_mix_g, w_in, conv_w, conv_b, dt_bias_fwd, dt_bias_bwd, a_log_fwd, a_log_bwd, d_skip,
           ssd_norm_g, attn_sink, w_branch_ssd, w_branch_attn, w_mix_out, norm_xattn_g, norm_mem_g, w_xattn_q,
           w_xattn_kv, w_xattn_out, norm_ffn_g, w_ffn_in, w_ffn_out, norm_final_g):
    bsz, seq, _ = x.shape
    mem_len = mem.shape[1]
    assert w_in.shape[0] == 1, "single-layer stack expected"
    layer = 0
    x2 = x.reshape(bsz * seq, D_MODEL)
    mem2 = mem.reshape(bsz * mem_len, D_MODEL)
    out = _layer(x2, mem2, bsz, seq, mem_len, norm_mix_g[layer], w_in.reshape(D_MODEL, -1), conv_w[layer], conv_b[layer],
                 dt_bias_fwd[layer], dt_bias_bwd[layer], a_log_fwd[layer], a_log_bwd[layer], d_skip[layer],
                 ssd_norm_g[layer], attn_sink[layer], w_branch_ssd[layer], w_branch_attn[layer],
                 w_mix_out[layer], norm_xattn_g[layer], norm_mem_g[layer], w_xattn_q[layer],
                 w_xattn_kv[layer], w_xattn_out[layer], norm_ffn_g[layer], w_ffn_in[layer],
                 w_ffn_out[layer], norm_final_g[None])
    return out.reshape(bsz, seq, D_MODEL)
```

```python
import functools

import jax
import jax.numpy as jnp
import numpy as np
from jax import lax
from jax.experimental import pallas as pl
from jax.experimental.pallas import tpu as pltpu

F32 = jnp.float32
BF16 = jnp.bfloat16

D_MODEL = 1024
EPS = 1e-6
SSD_INNER = 2048
SSD_HEAD_DIM = 64
SSD_HEADS = 32
SSD_GROUPS = 4
SSD_HPG = 8
SSD_STATE = 128
SSD_CONV = 5
CHUNK = 128
BWD_CHUNKS = 8
MAIN_CHUNKS = 2
DECAY_ROWS = 16
WIDE_ROWS = 4 * CHUNK + DECAY_ROWS
SSD_BC = SSD_GROUPS * SSD_STATE
SSD_XBC = SSD_INNER + 2 * SSD_BC
GROUP_W = SSD_HPG * SSD_HEAD_DIM
ATTN_HEAD_DIM = 64
ATTN_KV_HEADS = 4
ATTN_GQA = 4
ATTN_WIDTH = 1024
ATTN_KV_WIDTH = 256
ATTN_BLOCK = 128
SWA_BLOCKS = 8
ROPE_THETA = 10000.0
XATTN_HEADS = 4
XATTN_HEAD_DIM = 256
FFN_HIDDEN = 2816

LANES = 128
NEG_BIG = -1e30
LOG2_E = 1.4426950408889634

OFF_Z = 0
OFF_XBC = OFF_Z + SSD_INNER
OFF_DT = OFF_XBC + SSD_XBC
WIDTH_A = OFF_DT + LANES
START_B = OFF_DT + 2 * SSD_HEADS
OFF_Q = 0
OFF_K = OFF_Q + ATTN_WIDTH
OFF_G = OFF_K + 2 * ATTN_KV_WIDTH
WIDTH_B = OFF_G + 2 * D_MODEL

VMEM_LIMIT = 56 * 1024 * 1024


def _params(n_axes, flags=None):
    return pltpu.CompilerParams(dimension_semantics=("arbitrary",) * n_axes,
                                vmem_limit_bytes=VMEM_LIMIT, flags=flags)


def _resident(shape):
    nd = len(shape)
    return pl.BlockSpec(shape, lambda *_: (0,) * nd, pipeline_mode=pl.Buffered(1))


def _rms(x, g):
    return x * lax.rsqrt(jnp.mean(x * x, axis=-1, keepdims=True) + EPS) * g


def _silu(x):
    h = 0.5 * x
    return h + h * jnp.tanh(h)


def _dot(a, b):
    return jnp.dot(a, b, preferred_element_type=F32)


def _dot_nt(a, b):
    return lax.dot_general(a, b, (((1,), (1,)), ((), ())), preferred_element_type=F32)


def _rotary(u, cos, sin_signed):
    n = u.shape[1]
    lane = lax.broadcasted_iota(jnp.int32, u.shape, 1)
    first_half = (lane % ATTN_HEAD_DIM) < (ATTN_HEAD_DIM // 2)
    partner = jnp.where(first_half, pltpu.roll(u, n - ATTN_HEAD_DIM // 2, axis=1),
                        pltpu.roll(u, ATTN_HEAD_DIM // 2, axis=1))
    reps = n // LANES
    cos_t = jnp.concatenate([cos] * reps, axis=1) if reps > 1 else cos
    sin_t = jnp.concatenate([sin_signed] * reps, axis=1) if reps > 1 else sin_signed
    return u * cos_t + partner * sin_t


def _in_proj_kernel(x_ref, xp_ref, xn_ref, g_ref, wa_ref, wb_ref, cos_ref, sin_ref, cw_ref, cb_ref,
                    z_ref, act_ref, q_ref, kv_ref, gate_ref, dt_ref, ext_ref, *, per_seq):
    i = pl.program_id(0)
    tm = x_ref.shape[0]
    g = g_ref[...]
    h_ext = _rms(jnp.concatenate([x_ref[...], xp_ref[...], xn_ref[...]], axis=0), g).astype(BF16)
    step = 256
    rblk = 128
    row_blocks = [slice(r, r + rblk) for r in range(0, tm, rblk)]
    has_prev = (i % per_seq) > 0
    has_next = (i % per_seq) < per_seq - 1

    def proj(rs, w_ref, off, width):
        return _dot(h_ext[rs], w_ref[:, off:off + width])

    def xbc_proj(c):
        cs = slice(c, c + step)
        for rs in row_blocks[:-1]:
            ext_ref[8 + rs.start:8 + rs.stop, cs] = proj(rs, wa_ref, OFF_XBC + c, step)
        last = row_blocks[-1]
        u = proj(slice(last.start, tm + 16), wa_ref, OFF_XBC + c, step)
        ext_ref[8 + last.start:8 + tm, cs] = u[:rblk]
        ext_ref[0:8, cs] = jnp.where(has_prev, u[rblk:rblk + 8], 0.0)
        ext_ref[8 + tm:16 + tm, cs] = jnp.where(has_next, u[rblk + 8:], 0.0)

    assert SSD_CONV == 5

    def conv_silu(c):
        cs = slice(c, c + step)
        e = ext_ref[:, cs]
        rows = e.shape[0]
        tap = lambda k: cw_ref[k:k + 1, cs] * e
        down = lambda a: pltpu.roll(a, 1, axis=0)
        up = lambda a: pltpu.roll(a, rows - 1, axis=0)
        left = down(down(tap(0)) + tap(1)) + tap(2)
        right = up(up(tap(4)) + tap(3))
        acc = (left + right)[8:8 + tm] + cb_ref[:, cs]
        act_ref[:, cs] = _silu(acc).astype(BF16)

    def z_proj(c):
        for rs in row_blocks:
            z_ref[rs, c:c + step] = proj(rs, wa_ref, OFF_Z + c, step).astype(BF16)

    def q_proj(c):
        for rs in row_blocks:
            q = _rotary(proj(rs, wb_ref, OFF_Q + c, step), cos_ref[rs, :], sin_ref[rs, :])
            q_ref[rs, c:c + step] = (q * (ATTN_HEAD_DIM ** -0.5 * LOG2_E)).astype(BF16)

    def kv_proj(_):
        for rs in row_blocks:
            kv = proj(rs, wb_ref, OFF_K, 2 * ATTN_KV_WIDTH)
            kv_ref[rs, :ATTN_KV_WIDTH] = _rotary(kv[:, :ATTN_KV_WIDTH], cos_ref[rs, :], sin_ref[rs, :]).astype(BF16)
            kv_ref[rs, ATTN_KV_WIDTH:] = kv[:, ATTN_KV_WIDTH:].astype(BF16)
            dt_lane = lax.broadcasted_iota(jnp.int32, (rblk, LANES), 1)
            dt_ref[rs, :] = jnp.where(dt_lane < 2 * SSD_HEADS, proj(rs, wa_ref, OFF_DT, LANES), 0.0)

    def gate_proj(c):
        for rs in row_blocks:
            gate_ref[rs, c:c + step] = proj(rs, wb_ref, OFF_G + c, step).astype(BF16)

    mxu_tasks = ([(z_proj, c) for c in range(0, SSD_INNER, step)]
                 + [(q_proj, c) for c in range(0, ATTN_WIDTH, step)] + [(kv_proj, 0)]
                 + [(gate_proj, c) for c in range(0, 2 * D_MODEL, step)])
    conv_chunks = list(range(0, SSD_XBC, step))
    xbc_proj(conv_chunks[0])
    for j, c in enumerate(conv_chunks):
        if j + 1 < len(conv_chunks):
            xbc_proj(conv_chunks[j + 1])
        for fn, arg in mxu_tasks[2 * j:2 * j + 2]:
            fn(arg)
        conv_silu(c)
    for fn, arg in mxu_tasks[2 * len(conv_chunks):]:
        fn(arg)


def _in_proj(x2, g, w_a, w_b, cos, sin_signed, conv_w, conv_b, seq):
    t = x2.shape[0]
    tm = 512
    per_seq = seq // tm
    halo_per_tile = tm // 8
    last_halo = t // 8 - 1
    row = lambda i: (i, 0)
    pos = lambda i: (i % per_seq, 0)
    outs = [(SSD_INNER, BF16), (SSD_XBC, BF16), (ATTN_WIDTH, BF16), (2 * ATTN_KV_WIDTH, BF16),
            (2 * D_MODEL, BF16), (LANES, F32)]
    return pl.pallas_call(
        functools.partial(_in_proj_kernel, per_seq=per_seq),
        grid=(t // tm,),
        in_specs=[pl.BlockSpec((tm, D_MODEL), row),
                  pl.BlockSpec((8, D_MODEL), lambda i: (jnp.maximum(i * halo_per_tile - 1, 0), 0)),
                  pl.BlockSpec((8, D_MODEL), lambda i: (jnp.minimum((i + 1) * halo_per_tile, last_halo), 0)),
                  _resident((1, D_MODEL)),
                  _resident((D_MODEL, WIDTH_A)),
                  _resident((D_MODEL, WIDTH_B)),
                  pl.BlockSpec((tm, LANES), pos),
                  pl.BlockSpec((tm, LANES), pos),
                  _resident((8, SSD_XBC)),
                  _resident((1, SSD_XBC))],
        out_specs=[pl.BlockSpec((tm, w), row) for w, _ in outs],
        out_shape=[jax.ShapeDtypeStruct((t, w), d) for w, d in outs],
        scratch_shapes=[pltpu.VMEM((tm + 16, SSD_XBC), F32)],
        compiler_params=_params(1),
        name="in_proj",
    )(x2, x2, x2, g, w_a, w_b, cos, sin_signed, conv_w, conv_b)


def _cumsum_rows(a):
    row = lax.broadcasted_iota(jnp.int32, a.shape, 0)
    v = a
    k = 1
    while k < a.shape[0]:
        v = v + jnp.where(row >= k, pltpu.roll(v, k, axis=0), 0.0)
        k *= 2
    return v


def _expand(v, e2_ref):
    hi = v.astype(BF16)
    lo = (v - hi.astype(F32)).astype(BF16)
    return _dot(jnp.concatenate([hi, lo], axis=1), e2_ref[...])


def _softplus(x):
    return jnp.maximum(x, 0.0) + jnp.log1p(jnp.exp(-jnp.abs(x)))


def _head_scalars(dt_raw, hp_ref):
    dtv = _softplus(dt_raw + hp_ref[0:1, :])
    a = dtv * (-LOG2_E * jnp.exp(hp_ref[1:2, :]))
    return dtv, a


def _chunk_scalars(dt_raw, hp_ref):
    dtv, a = _head_scalars(dt_raw, hp_ref)
    incl = _cumsum_rows(a)
    return dtv, incl, incl - a


def _ssd_bwd_kernel(act_ref, dt_ref, hp_ref, e2b_ref, hb_ref, state_ref):
    c = pl.program_id(1)

    @pl.when(c == 0)
    def _():
        state_ref[...] = jnp.zeros_like(state_ref)

    scales, decays = [], []
    for j in range(BWD_CHUNKS):
        dtv, incl, excl = _chunk_scalars(dt_ref[j * CHUNK:(j + 1) * CHUNK, :], hp_ref)
        scales.append(jnp.exp2(excl) * dtv)
        decays.append(jnp.exp2(incl[CHUNK - DECAY_ROWS:, :]))
    wide = _expand(jnp.concatenate(scales + decays, axis=0), e2b_ref)

    for j in reversed(range(BWD_CHUNKS)):
        rows = slice(j * CHUNK, (j + 1) * CHUNK)
        hb_ref[j] = state_ref[...].astype(BF16)
        xs = act_ref[rows, :SSD_INNER].astype(F32)
        bm = act_ref[rows, SSD_INNER:].astype(F32)
        xw = (xs * wide[rows]).astype(BF16)
        last = BWD_CHUNKS * CHUNK + (j + 1) * DECAY_ROWS - 1
        decay = wide[last:last + 1, :]
        for g in range(SSD_GROUPS):
            gs = slice(g * GROUP_W, (g + 1) * GROUP_W)
            bt = bm[:, g * SSD_STATE:(g + 1) * SSD_STATE].T.astype(BF16)
            state_ref[:, gs] = state_ref[:, gs] * decay[:, gs] + _dot(bt, xw[:, gs])


def _ssd_bwd(act, dt, hp, e2b, bsz, nc):
    width = SSD_INNER + SSD_BC
    steps = nc // BWD_CHUNKS
    block_of = lambda b, c: b * steps + (steps - 1 - c)
    return pl.pallas_call(
        _ssd_bwd_kernel,
        grid=(bsz, steps),
        in_specs=[pl.BlockSpec((BWD_CHUNKS * CHUNK, width), lambda b, c: (block_of(b, c), 0)),
                  pl.BlockSpec((BWD_CHUNKS * CHUNK, LANES), lambda b, c: (block_of(b, c), 0)),
                  _resident((8, LANES)),
                  _resident((2 * LANES, SSD_INNER))],
        out_specs=pl.BlockSpec((BWD_CHUNKS, SSD_STATE, SSD_INNER), lambda b, c: (block_of(b, c), 0, 0)),
        out_shape=jax.ShapeDtypeStruct((bsz * nc, SSD_STATE, SSD_INNER), BF16),
        scratch_shapes=[pltpu.VMEM((SSD_STATE, SSD_INNER), F32)],
        compiler_params=_params(2),
        name="ssd_bwd",
    )(act, dt, hp, e2b)


def _ssd_main_kernel(act_ref, dt_ref, z_ref, hb_ref, hp_ref, dskip_ref, ng_ref, e2f_ref, sel_ref,
                     y_ref, state_ref, ybuf_ref):
    c = pl.program_id(1)

    @pl.when(c == 0)
    def _():
        state_ref[...] = jnp.zeros_like(state_ref)

    lane = lax.broadcasted_iota(jnp.int32, (CHUNK, LANES), 1)
    row = lax.broadcasted_iota(jnp.int32, (CHUNK, CHUNK), 0)
    col = lax.broadcasted_iota(jnp.int32, (CHUNK, CHUNK), 1)
    groups = range(SSD_GROUPS)
    to_fwd_slot = lambda v: pltpu.roll(v, LANES - SSD_HEADS, axis=1)
    prep, wide_in, col_in = [], [], []
    for j in range(MAIN_CHUNKS):
        rows = slice(j * CHUNK, (j + 1) * CHUNK)
        dtv, incl, excl = _chunk_scalars(dt_ref[rows, :], hp_ref)
        tot = incl[CHUNK - 1:CHUNK, :]
        cbs = [_dot_nt(act_ref[rows, SSD_INNER + SSD_BC + g * SSD_STATE:SSD_INNER + SSD_BC + (g + 1) * SSD_STATE],
                       act_ref[rows, SSD_INNER + g * SSD_STATE:SSD_INNER + (g + 1) * SSD_STATE]) for g in groups]
        lane_group = (lane % SSD_HEADS) // SSD_HPG
        cb_diag = jnp.zeros_like(dtv)
        for g in groups:
            dg = jnp.sum(jnp.where(row == col, cbs[g], 0.0), axis=1, keepdims=True)
            cb_diag = jnp.where(lane_group == g, dg, cb_diag)
        wide_in += [jnp.exp2(incl),
                    jnp.exp2(tot - incl) * dtv,
                    to_fwd_slot(jnp.exp2(tot - excl)),
                    to_fwd_slot(cb_diag * dtv),
                    jnp.exp2(incl[CHUNK - DECAY_ROWS:, :])]
        log_dt = jnp.log2(dtv)
        key_f = (incl - log_dt).T
        key_b = (excl + log_dt).T
        vmix = jnp.where(lane < SSD_HEADS, incl, excl)
        hi = vmix.astype(BF16).astype(F32)
        lo = vmix - hi
        for shift in [d * SSD_HEADS + g * SSD_HPG for d in range(2) for g in groups]:
            sh = (LANES - shift) % LANES
            col_in.append(jnp.concatenate([pltpu.roll(hi, sh, axis=1) if sh else hi,
                                           pltpu.roll(lo, sh, axis=1) if sh else lo], axis=1).astype(BF16))
        prep.append((cbs, key_f, key_b))
    wide = _expand(jnp.concatenate(wide_in, axis=0), e2f_ref)
    cols = _dot(jnp.concatenate(col_in, axis=0), sel_ref[...])

    for j in range(MAIN_CHUNKS):
        w0 = j * WIDE_ROWS
        c0 = j * 2 * SSD_GROUPS * CHUNK
        _ssd_main_chunk(j, prep[j], wide[w0:w0 + WIDE_ROWS], cols[c0:c0 + 2 * SSD_GROUPS * CHUNK],
                        act_ref, z_ref, hb_ref, dskip_ref, ng_ref, y_ref, state_ref, ybuf_ref.at[j])


def _ssd_main_chunk(j, prep, wide, cols, act_ref, z_ref, hb_ref, dskip_ref, ng_ref, y_ref, state_ref, ybuf_ref):
    rows = slice(j * CHUNK, (j + 1) * CHUNK)
    cbs, key_f, key_b = prep
    scale_f = wide[0:CHUNK]
    state_scale = wide[CHUNK:2 * CHUNK]
    scale_b = wide[2 * CHUNK:3 * CHUNK]
    self_b = wide[3 * CHUNK:4 * CHUNK]
    decay = wide[WIDE_ROWS - 1:WIDE_ROWS, :]
    xs16 = act_ref[rows, :SSD_INNER]
    xs = xs16.astype(F32)
    bm16 = act_ref[rows, SSD_INNER:SSD_INNER + SSD_BC]
    cm16 = act_ref[rows, SSD_INNER + SSD_BC:]

    row = lax.broadcasted_iota(jnp.int32, (CHUNK, CHUNK), 0)
    col = lax.broadcasted_iota(jnp.int32, (CHUNK, CHUNK), 1)
    lower = row >= col
    first_head = lax.broadcasted_iota(jnp.int32, (CHUNK, 2 * SSD_HEAD_DIM), 1) < SSD_HEAD_DIM

    groups = range(SSD_GROUPS)
    ns = [slice(g * SSD_STATE, (g + 1) * SSD_STATE) for g in groups]
    gs = [slice(g * GROUP_W, (g + 1) * GROUP_W) for g in groups]
    xw = (xs * state_scale).astype(BF16)
    st = [state_ref[:, gs[g]] for g in groups]
    off_f = [_dot(cm16[:, ns[g]], st[g].astype(BF16)) for g in groups]
    off_b = [_dot(cm16[:, ns[g]], hb_ref[j, :, gs[g]]) for g in groups]
    for g in groups:
        bt = bm16[:, ns[g]].astype(F32).T.astype(BF16)
        state_ref[:, gs[g]] = st[g] * decay[:, gs[g]] + _dot(bt, xw[:, gs[g]])

    for g in groups:
        col_f = cols[g * CHUNK:(g + 1) * CHUNK]
        col_b = cols[(SSD_GROUPS + g) * CHUNK:(SSD_GROUPS + g + 1) * CHUNK]
        for pair in range(SSD_HPG // 2):
            ms = []
            for r in (2 * pair, 2 * pair + 1):
                h = g * SSD_HPG + r
                hb_col = SSD_HEADS + h
                rs = slice(r * CHUNK, (r + 1) * CHUNK)
                seg_f = col_f[:, rs] - key_f[h:h + 1, :]
                seg_b = key_b[hb_col:hb_col + 1, :] - col_b[:, rs]
                ms.append((cbs[g] * jnp.exp2(jnp.where(lower, seg_f, seg_b))).astype(BF16))
            ps = slice((g * SSD_HPG + 2 * pair) * SSD_HEAD_DIM, (g * SSD_HPG + 2 * pair + 2) * SSD_HEAD_DIM)
            x2 = xs16[:, ps]
            zero = jnp.zeros_like(x2)
            rhs = jnp.concatenate([jnp.where(first_head, x2, zero), jnp.where(first_head, zero, x2)], axis=0)
            ybuf_ref[:, ps] = _dot(jnp.concatenate(ms, axis=1), rhs)

    zf = z_ref[rows, :].astype(F32)
    zgate = _silu(zf)
    for g in groups:
        y = (ybuf_ref[:, gs[g]] + scale_f[:, gs[g]] * off_f[g] + scale_b[:, gs[g]] * off_b[g]
             + (dskip_ref[:, gs[g]] + self_b[:, gs[g]]) * xs[:, gs[g]])
        v = y * zgate[:, gs[g]]
        v = v * lax.rsqrt(jnp.mean(v * v, axis=-1, keepdims=True) + EPS)
        y_ref[rows, gs[g]] = (v * ng_ref[:, gs[g]]).astype(BF16)


def _ssd_main(act, dt, z, hb, hp, dskip, norm_g, e2f, sel, bsz, nc):
    steps = nc // MAIN_CHUNKS
    rows = MAIN_CHUNKS * CHUNK
    block_of = lambda b, c: b * steps + c
    return pl.pallas_call(
        _ssd_main_kernel,
        grid=(bsz, steps),
        in_specs=[pl.BlockSpec((rows, SSD_XBC), lambda b, c: (block_of(b, c), 0)),
                  pl.BlockSpec((rows, LANES), lambda b, c: (block_of(b, c), 0)),
                  pl.BlockSpec((rows, SSD_INNER), lambda b, c: (block_of(b, c), 0)),
                  pl.BlockSpec((MAIN_CHUNKS, SSD_STATE, SSD_INNER), lambda b, c: (block_of(b, c), 0, 0)),
                  _resident((8, LANES)),
                  _resident((1, SSD_INNER)),
                  _resident((1, SSD_INNER)),
                  _resident((2 * LANES, SSD_INNER)),
                  _resident((2 * LANES, SSD_HPG * CHUNK))],
        out_specs=pl.BlockSpec((rows, SSD_INNER), lambda b, c: (block_of(b, c), 0)),
        out_shape=jax.ShapeDtypeStruct((bsz * nc * CHUNK, SSD_INNER), BF16),
        scratch_shapes=[pltpu.VMEM((SSD_STATE, SSD_INNER), F32),
                        pltpu.VMEM((MAIN_CHUNKS, CHUNK, SSD_INNER), F32)],
        compiler_params=_params(2),
        name="ssd_main",
    )(act, dt, z, hb, hp, dskip, norm_g, e2f, sel)


def _swa_kernel(q_ref, kvp_ref, kvc_ref, kvn_ref, sink_ref, o_ref):
    n = pl.program_id(1)
    last = pl.num_programs(1) - 1
    kj = lax.broadcasted_iota(jnp.int32, (3 * ATTN_BLOCK, ATTN_BLOCK), 0)
    qi = lax.broadcasted_iota(jnp.int32, (3 * ATTN_BLOCK, ATTN_BLOCK), 1)
    rel = kj - qi
    band = (rel >= 0) & (rel <= 2 * ATTN_BLOCK)
    kvcat = jnp.concatenate([kvp_ref[...], kvc_ref[...], kvn_ref[...]], axis=0)
    kcat = kvcat[:, :ATTN_KV_WIDTH]
    vcat_t = kvcat[:, ATTN_KV_WIDTH:].astype(F32).T.astype(BF16)
    scores = []
    for j in range(SWA_BLOCKS):
        valid = band
        if j == 0:
            valid = valid & ((kj >= ATTN_BLOCK) | (n > 0))
        if j == SWA_BLOCKS - 1:
            valid = valid & ((kj < 2 * ATTN_BLOCK) | (n < last))
        bias = jnp.concatenate([jnp.where(valid, 0.0, NEG_BIG)] * ATTN_GQA, axis=1)
        edge_bias = (bias[:ATTN_BLOCK], bias[2 * ATTN_BLOCK:])
        keys = slice(j * ATTN_BLOCK, (j + 3) * ATTN_BLOCK)
        qrows = slice(j * ATTN_BLOCK, (j + 1) * ATTN_BLOCK)
        for kv in range(ATTN_KV_HEADS):
            ks = slice(kv * ATTN_HEAD_DIM, (kv + 1) * ATTN_HEAD_DIM)
            q_stack = jnp.concatenate(
                [q_ref[qrows, (kv * ATTN_GQA + r) * ATTN_HEAD_DIM:(kv * ATTN_GQA + r + 1) * ATTN_HEAD_DIM]
                 for r in range(ATTN_GQA)], axis=0)
            s = _dot_nt(kcat[keys, ks], q_stack)
            scores.append(jnp.concatenate([s[:ATTN_BLOCK] + edge_bias[0], s[ATTN_BLOCK:2 * ATTN_BLOCK],
                                           s[2 * ATTN_BLOCK:] + edge_bias[1]], axis=0))
    probs = []
    for idx, s in enumerate(scores):
        kv = idx % ATTN_KV_HEADS
        sink = sink_ref[kv:kv + 1, :]
        m = jnp.maximum(jnp.max(s, axis=0, keepdims=True), sink)
        p = jnp.exp2(s - m)
        denom = jnp.sum(p, axis=0, keepdims=True) + jnp.exp2(sink - m)
        probs.append((p.astype(BF16), 1.0 / denom))
    for j in range(SWA_BLOCKS):
        keys = slice(j * ATTN_BLOCK, (j + 3) * ATTN_BLOCK)
        outs = []
        for kv in range(ATTN_KV_HEADS):
            ks = slice(kv * ATTN_HEAD_DIM, (kv + 1) * ATTN_HEAD_DIM)
            p16, inv = probs[j * ATTN_KV_HEADS + kv]
            o_t = _dot(vcat_t[ks, keys], p16) * inv
            outs.extend(o_t[:, r * ATTN_BLOCK:(r + 1) * ATTN_BLOCK] for r in range(ATTN_GQA))
        o_ref[j * ATTN_BLOCK:(j + 1) * ATTN_BLOCK, :] = jnp.concatenate(outs, axis=0).T.astype(BF16)


def _swa(q, kv, sink_rows, bsz, nb):
    steps = nb // SWA_BLOCKS
    rows = SWA_BLOCKS * ATTN_BLOCK
    cur = lambda b, n: (b * steps + n, 0)
    prev = lambda b, n: (b * nb + jnp.maximum(SWA_BLOCKS * n - 1, 0), 0)
    nxt = lambda b, n: (b * nb + jnp.minimum(SWA_BLOCKS * (n + 1), nb - 1), 0)
    return pl.pallas_call(
        _swa_kernel,
        grid=(bsz, steps),
        in_specs=[pl.BlockSpec((rows, ATTN_WIDTH), cur),
                  pl.BlockSpec((ATTN_BLOCK, 2 * ATTN_KV_WIDTH), prev),
                  pl.BlockSpec((rows, 2 * ATTN_KV_WIDTH), cur),
                  pl.BlockSpec((ATTN_BLOCK, 2 * ATTN_KV_WIDTH), nxt),
                  _resident((8, ATTN_GQA * ATTN_BLOCK))],
        out_specs=pl.BlockSpec((rows, ATTN_WIDTH), cur),
        out_shape=jax.ShapeDtypeStruct(q.shape, BF16),
        compiler_params=_params(2),
        name="swa",
    )(q, kv, kv, kv, sink_rows)


def _mem_kv_kernel(m_ref, g_ref, w_ref, o_ref):
    h = _rms(m_ref[...], g_ref[...]).astype(BF16)
    o_ref[...] = _dot(h, w_ref[...]).astype(BF16)


def _mem_kv(mem2, g, w_kv):
    t = mem2.shape[0]
    tm = 256
    row = lambda i: (i, 0)
    return pl.pallas_call(
        _mem_kv_kernel,
        grid=(t // tm,),
        in_specs=[pl.BlockSpec((tm, D_MODEL), row),
                  _resident((1, D_MODEL)),
                  _resident((D_MODEL, 2 * D_MODEL))],
        out_specs=pl.BlockSpec((tm, 2 * D_MODEL), row),
        out_shape=jax.ShapeDtypeStruct((t, 2 * D_MODEL), BF16),
        compiler_params=_params(1),
        name="mem_kv",
    )(mem2, g, w_kv)


def _merge_xattn_kernel(x_ref, y_ref, a_ref, gate_ref, kv_ref, ws_ref, wa_ref, wo_ref, g_ref, wq_ref, wxo_ref,
                        o_ref, ctx_ref):
    bs = _dot(y_ref[...], ws_ref[...])
    ba = _dot(a_ref[...], wa_ref[...])
    gate = jax.nn.sigmoid(gate_ref[...].astype(F32))
    mix = (gate[:, :D_MODEL] * bs + gate[:, D_MODEL:] * ba).astype(BF16)
    x = x_ref[...] + _dot(mix, wo_ref[...])

    h = _rms(x, g_ref[...]).astype(BF16)
    q = (_dot(h, wq_ref[...]) * (XATTN_HEAD_DIM ** -0.5 * LOG2_E)).astype(BF16)
    heads = [slice(hd * XATTN_HEAD_DIM, (hd + 1) * XATTN_HEAD_DIM) for hd in range(XATTN_HEADS)]
    scores = [_dot_nt(q[:, ks], kv_ref[:, ks]) for ks in heads]
    probs = []
    for s in scores:
        p = jnp.exp2(s - jnp.max(s, axis=-1, keepdims=True))
        probs.append((p.astype(BF16), 1.0 / jnp.sum(p, axis=-1, keepdims=True)))
    for ks, (p16, inv) in zip(heads, probs):
        vs = slice(D_MODEL + ks.start, D_MODEL + ks.stop)
        ctx_ref[:, ks] = (_dot(p16, kv_ref[:, vs]) * inv).astype(BF16)
    o_ref[...] = x + _dot(ctx_ref[...], wxo_ref[...])


def _merge_xattn(x2, y, attn, gates, kv, ws, wa, wo, g, wq, wxo, seq, mem_len):
    t = x2.shape[0]
    tm = 512
    per_seq = seq // tm
    row = lambda i: (i, 0)
    return pl.pallas_call(
        _merge_xattn_kernel,
        grid=(t // tm,),
        in_specs=[pl.BlockSpec((tm, D_MODEL), row),
                  pl.BlockSpec((tm, SSD_INNER), row),
                  pl.BlockSpec((tm, ATTN_WIDTH), row),
                  pl.BlockSpec((tm, 2 * D_MODEL), row),
                  pl.BlockSpec((mem_len, 2 * D_MODEL), lambda i: (i // per_seq, 0)),
                  _resident((SSD_INNER, D_MODEL)),
                  _resident((ATTN_WIDTH, D_MODEL)),
                  _resident((D_MODEL, D_MODEL)),
                  _resident((1, D_MODEL)),
                  _resident((D_MODEL, D_MODEL)),
                  _resident((D_MODEL, D_MODEL))],
        out_specs=pl.BlockSpec((tm, D_MODEL), row),
        out_shape=jax.ShapeDtypeStruct((t, D_MODEL), F32),
        scratch_shapes=[pltpu.VMEM((tm, D_MODEL), BF16)],
        compiler_params=_params(1),
        name="merge_xattn",
    )(x2, y, attn, gates, kv, ws, wa, wo, g, wq, wxo)


MXU_TILE = 256
FFN_SPLIT = (FFN_HIDDEN // MXU_TILE + 1) // 2 * MXU_TILE


def _ffn_kernel(x_ref, g_ref, wi_ref, wd_ref, gf_ref, o_ref):
    x = x_ref[...]
    h = _rms(x, g_ref[...]).astype(BF16)
    acc = x
    for lo, hi in ((0, FFN_SPLIT), (FFN_SPLIT, FFN_HIDDEN)):
        gate = _dot(h, wi_ref[:, lo:hi])
        up = _dot(h, wi_ref[:, FFN_HIDDEN + lo:FFN_HIDDEN + hi])
        act = (_silu(gate) * up).astype(BF16)
        acc = acc + _dot(act, wd_ref[lo:hi, :])
    o_ref[...] = _rms(acc, gf_ref[...])


def _ffn(x2, g, wi, wd, gf):
    t = x2.shape[0]
    tm = 512
    row = lambda i: (i, 0)
    return pl.pallas_call(
        _ffn_kernel,
        grid=(t // tm,),
        in_specs=[pl.BlockSpec((tm, D_MODEL), row),
                  _resident((1, D_MODEL)),
                  _resident((D_MODEL, 2 * FFN_HIDDEN)),
                  _resident((FFN_HIDDEN, D_MODEL)),
                  _resident((1, D_MODEL))],
        out_specs=pl.BlockSpec((tm, D_MODEL), row),
        out_shape=jax.ShapeDtypeStruct((t, D_MODEL), F32),
        compiler_params=_params(1),
        name="ffn",
    )(x2, g, wi, wd, gf)


def _pad_rows(a, rows):
    return jnp.pad(a, ((0, rows - a.shape[0]), (0, 0)))


def _pad_cols(a, cols):
    return jnp.pad(a, ((0, 0), (0, cols - a.shape[1])))


def _column_select_matrix():
    j = np.arange(LANES)[:, None]
    blk = np.arange(SSD_HPG * CHUNK)[None, :] // CHUNK
    e = (j == blk).astype(np.float32)
    return jnp.asarray(np.concatenate([e, e], axis=0), dtype=BF16)


def _select_matrix(slot):
    j = np.arange(LANES)[:, None]
    ch = np.arange(SSD_INNER)[None, :]
    e = (j == slot * SSD_HEADS + ch // SSD_HEAD_DIM).astype(np.float32)
    return jnp.asarray(np.concatenate([e, e], axis=0), dtype=BF16)


def _layer(x2, mem2, bsz, seq, mem_len, norm_mix_g, w_in, conv_w, conv_b, dt_bias_fwd, dt_bias_bwd,
           a_log_fwd, a_log_bwd, d_skip, ssd_norm_g, attn_sink, w_branch_ssd, w_branch_attn, w_mix_out,
           norm_xattn_g, norm_mem_g, w_xattn_q, w_xattn_kv, w_xattn_out, norm_ffn_g, w_ffn_in, w_ffn_out,
           final_g):
    nc = seq // CHUNK
    w_a = w_in[:, :WIDTH_A].astype(BF16)
    w_b = w_in[:, START_B:].astype(BF16)

    half = ATTN_HEAD_DIM // 2
    inv_freq = ROPE_THETA ** (-jnp.arange(half, dtype=F32) / half)
    ang = jnp.arange(seq, dtype=F32)[:, None] * inv_freq[None]
    cos = jnp.tile(jnp.cos(ang), (1, LANES // half))
    sin_signed = jnp.tile(jnp.concatenate([-jnp.sin(ang), jnp.sin(ang)], axis=1), (1, LANES // ATTN_HEAD_DIM))

    cw = _pad_rows(conv_w, 8)
    z, act, q, kv_attn, gates, dt = _in_proj(x2, norm_mix_g[None], w_a, w_b, cos, sin_signed, cw, conv_b[None], seq)

    hp = _pad_rows(jnp.stack([_pad_cols(jnp.concatenate([dt_bias_fwd, dt_bias_bwd])[None], LANES)[0],
                              _pad_cols(jnp.concatenate([a_log_fwd, a_log_bwd])[None], LANES)[0]]), 8)
    e2f = _select_matrix(0)
    e2b = _select_matrix(1)
    hb = _ssd_bwd(act, dt, hp, e2b, bsz, nc)
    dskip = jnp.repeat(d_skip, SSD_HEAD_DIM)[None]
    y = _ssd_main(act, dt, z, hb, hp, dskip, ssd_norm_g[None], e2f, _column_select_matrix(), bsz, nc)

    sink_rows = _pad_rows(jnp.repeat(attn_sink * LOG2_E, ATTN_BLOCK).reshape(ATTN_KV_HEADS, ATTN_GQA * ATTN_BLOCK), 8)
    attn = _swa(q, kv_attn, sink_rows, bsz, seq // ATTN_BLOCK)

    kv = _mem_kv(mem2, norm_mem_g[None], w_xattn_kv.astype(BF16))
    x2 = _merge_xattn(x2, y, attn, gates, kv, w_branch_ssd.astype(BF16), w_branch_attn.astype(BF16),
                      w_mix_out.astype(BF16), norm_xattn_g[None], w_xattn_q.astype(BF16),
                      w_xattn_out.astype(BF16), seq, mem_len)

    return _ffn(x2, norm_ffn_g[None], w_ffn_in.astype(BF16), w_ffn_out.astype(BF16), final_g)


def kernel(x, mem, norm_mix_g, w_in, conv_w, conv_b, dt_bias_fwd, dt_bias_bwd, a_log_fwd, a_log_bwd, d_skip,
           ssd_norm_g, attn_sink, w_branch_ssd, w_branch_attn, w_mix_out, norm_xattn_g, norm_mem_g, w_xattn_q,
           w_xattn_kv, w_xattn_out, norm_ffn_g, w_ffn_in, w_ffn_out, norm_final_g):
    bsz, seq, _ = x.shape
    mem_len = mem.shape[1]
    assert w_in.shape[0] == 1, "single-layer stack expected"
    layer = 0
    x2 = x.reshape(bsz * seq, D_MODEL)
    mem2 = mem.reshape(bsz * mem_len, D_MODEL)
    out = _layer(x2, mem2, bsz, seq, mem_len, norm_mix_g[layer], w_in[layer], conv_w[layer], conv_b[layer],
                 dt_bias_fwd[layer], dt_bias_bwd[layer], a_log_fwd[layer], a_log_bwd[layer], d_skip[layer],
                 ssd_norm_g[layer], attn_sink[layer], w_branch_ssd[layer], w_branch_attn[layer],
                 w_mix_out[layer], norm_xattn_g[layer], norm_mem_g[layer], w_xattn_q[layer],
                 w_xattn_kv[layer], w_xattn_out[layer], norm_ffn_g[layer], w_ffn_in[layer],
                 w_ffn_out[layer], norm_final_g[None])
    return out.reshape(bsz, seq, D_MODEL)
```

```python
import functools

import jax
import jax.numpy as jnp
import numpy as np
from jax import lax
from jax.experimental import pallas as pl
from jax.experimental.pallas import tpu as pltpu

F32 = jnp.float32
BF16 = jnp.bfloat16

D_MODEL = 1024
EPS = 1e-6
SSD_INNER = 2048
SSD_HEAD_DIM = 64
SSD_HEADS = 32
SSD_GROUPS = 4
SSD_HPG = 8
SSD_STATE = 128
SSD_CONV = 5
CHUNK = 128
BWD_CHUNKS = 8
MAIN_CHUNKS = 2
DECAY_ROWS = 16
WIDE_ROWS = 4 * CHUNK + DECAY_ROWS
SSD_BC = SSD_GROUPS * SSD_STATE
SSD_XBC = SSD_INNER + 2 * SSD_BC
GROUP_W = SSD_HPG * SSD_HEAD_DIM
ATTN_HEAD_DIM = 64
ATTN_KV_HEADS = 4
ATTN_GQA = 4
ATTN_WIDTH = 1024
ATTN_KV_WIDTH = 256
ATTN_BLOCK = 128
SWA_BLOCKS = 8
ROPE_THETA = 10000.0
XATTN_HEADS = 4
XATTN_HEAD_DIM = 256
FFN_HIDDEN = 2816

LANES = 128
NEG_BIG = -1e30
LOG2_E = 1.4426950408889634

OFF_Z = 0
OFF_XBC = OFF_Z + SSD_INNER
OFF_DT = OFF_XBC + SSD_XBC
WIDTH_A = OFF_DT + LANES
START_B = OFF_DT + 2 * SSD_HEADS
OFF_Q = 0
OFF_K = OFF_Q + ATTN_WIDTH
OFF_G = OFF_K + 2 * ATTN_KV_WIDTH
WIDTH_B = OFF_G + 2 * D_MODEL

VMEM_LIMIT = 56 * 1024 * 1024


def _params(n_axes, flags=None):
    return pltpu.CompilerParams(dimension_semantics=("arbitrary",) * n_axes,
                                vmem_limit_bytes=VMEM_LIMIT, flags=flags)


def _resident(shape):
    nd = len(shape)
    return pl.BlockSpec(shape, lambda *_: (0,) * nd, pipeline_mode=pl.Buffered(1))


def _rms(x, g):
    return x * lax.rsqrt(jnp.mean(x * x, axis=-1, keepdims=True) + EPS) * g


def _silu(x):
    h = 0.5 * x
    return h + h * jnp.tanh(h)


def _dot(a, b):
    return jnp.dot(a, b, preferred_element_type=F32)


def _dot_nt(a, b):
    return lax.dot_general(a, b, (((1,), (1,)), ((), ())), preferred_element_type=F32)


def _rotary(u, cos, sin_signed):
    n = u.shape[1]
    lane = lax.broadcasted_iota(jnp.int32, u.shape, 1)
    first_half = (lane % ATTN_HEAD_DIM) < (ATTN_HEAD_DIM // 2)
    partner = jnp.where(first_half, pltpu.roll(u, n - ATTN_HEAD_DIM // 2, axis=1),
                        pltpu.roll(u, ATTN_HEAD_DIM // 2, axis=1))
    reps = n // LANES
    cos_t = jnp.concatenate([cos] * reps, axis=1) if reps > 1 else cos
    sin_t = jnp.concatenate([sin_signed] * reps, axis=1) if reps > 1 else sin_signed
    return u * cos_t + partner * sin_t


def _in_proj_kernel(x_ref, xp_ref, xn_ref, g_ref, wa_ref, wb_ref, cos_ref, sin_ref, cw_ref, cb_ref,
                    z_ref, act_ref, q_ref, kv_ref, gate_ref, dt_ref, ext_ref, *, per_seq):
    i = pl.program_id(0)
    tm = x_ref.shape[0]
    g = g_ref[...]
    h_ext = _rms(jnp.concatenate([x_ref[...], xp_ref[...], xn_ref[...]], axis=0), g).astype(BF16)
    step = 256
    rblk = 128
    row_blocks = [slice(r, r + rblk) for r in range(0, tm, rblk)]
    has_prev = (i % per_seq) > 0
    has_next = (i % per_seq) < per_seq - 1

    def proj(rs, w_ref, off, width):
        return _dot(h_ext[rs], w_ref[:, off:off + width])

    def xbc_proj(c):
        cs = slice(c, c + step)
        for rs in row_blocks[:-1]:
            ext_ref[8 + rs.start:8 + rs.stop, cs] = proj(rs, wa_ref, OFF_XBC + c, step)
        last = row_blocks[-1]
        u = proj(slice(last.start, tm + 16), wa_ref, OFF_XBC + c, step)
        ext_ref[8 + last.start:8 + tm, cs] = u[:rblk]
        ext_ref[0:8, cs] = jnp.where(has_prev, u[rblk:rblk + 8], 0.0)
        ext_ref[8 + tm:16 + tm, cs] = jnp.where(has_next, u[rblk + 8:], 0.0)

    assert SSD_CONV == 5

    def conv_silu(c):
        cs = slice(c, c + step)
        e = ext_ref[:, cs]
        rows = e.shape[0]
        tap = lambda k: cw_ref[k:k + 1, cs] * e
        down = lambda a: pltpu.roll(a, 1, axis=0)
        up = lambda a: pltpu.roll(a, rows - 1, axis=0)
        left = down(down(tap(0)) + tap(1)) + tap(2)
        right = up(up(tap(4)) + tap(3))
        acc = (left + right)[8:8 + tm] + cb_ref[:, cs]
        act_ref[:, cs] = _silu(acc).astype(BF16)

    def z_proj(c):
        for rs in row_blocks:
            z_ref[rs, c:c + step] = proj(rs, wa_ref, OFF_Z + c, step).astype(BF16)

    def q_proj(c):
        for rs in row_blocks:
            q = _rotary(proj(rs, wb_ref, OFF_Q + c, step), cos_ref[rs, :], sin_ref[rs, :])
            q_ref[rs, c:c + step] = (q * (ATTN_HEAD_DIM ** -0.5 * LOG2_E)).astype(BF16)

    def kv_proj(_):
        for rs in row_blocks:
            kv = proj(rs, wb_ref, OFF_K, 2 * ATTN_KV_WIDTH)
            kv_ref[rs, :ATTN_KV_WIDTH] = _rotary(kv[:, :ATTN_KV_WIDTH], cos_ref[rs, :], sin_ref[rs, :]).astype(BF16)
            kv_ref[rs, ATTN_KV_WIDTH:] = kv[:, ATTN_KV_WIDTH:].astype(BF16)
            dt_lane = lax.broadcasted_iota(jnp.int32, (rblk, LANES), 1)
            dt_ref[rs, :] = jnp.where(dt_lane < 2 * SSD_HEADS, proj(rs, wa_ref, OFF_DT, LANES), 0.0)

    def gate_proj(c):
        for rs in row_blocks:
            gate_ref[rs, c:c + step] = proj(rs, wb_ref, OFF_G + c, step).astype(BF16)

    mxu_tasks = ([(z_proj, c) for c in range(0, SSD_INNER, step)]
                 + [(q_proj, c) for c in range(0, ATTN_WIDTH, step)] + [(kv_proj, 0)]
                 + [(gate_proj, c) for c in range(0, 2 * D_MODEL, step)])
    conv_chunks = list(range(0, SSD_XBC, step))
    xbc_proj(conv_chunks[0])
    for j, c in enumerate(conv_chunks):
        if j + 1 < len(conv_chunks):
            xbc_proj(conv_chunks[j + 1])
        for fn, arg in mxu_tasks[2 * j:2 * j + 2]:
            fn(arg)
        conv_silu(c)
    for fn, arg in mxu_tasks[2 * len(conv_chunks):]:
        fn(arg)


def _in_proj(x2, g, w_a, w_b, cos, sin_signed, conv_w, conv_b, seq):
    t = x2.shape[0]
    tm = 512
    per_seq = seq // tm
    halo_per_tile = tm // 8
    last_halo = t // 8 - 1
    row = lambda i: (i, 0)
    pos = lambda i: (i % per_seq, 0)
    outs = [(SSD_INNER, BF16), (SSD_XBC, BF16), (ATTN_WIDTH, BF16), (2 * ATTN_KV_WIDTH, BF16),
            (2 * D_MODEL, BF16), (LANES, F32)]
    return pl.pallas_call(
        functools.partial(_in_proj_kernel, per_seq=per_seq),
        grid=(t // tm,),
        in_specs=[pl.BlockSpec((tm, D_MODEL), row),
                  pl.BlockSpec((8, D_MODEL), lambda i: (jnp.maximum(i * halo_per_tile - 1, 0), 0)),
                  pl.BlockSpec((8, D_MODEL), lambda i: (jnp.minimum((i + 1) * halo_per_tile, last_halo), 0)),
                  _resident((1, D_MODEL)),
                  _resident((D_MODEL, WIDTH_A)),
                  _resident((D_MODEL, WIDTH_B)),
                  pl.BlockSpec((tm, LANES), pos),
                  pl.BlockSpec((tm, LANES), pos),
                  _resident((8, SSD_XBC)),
                  _resident((1, SSD_XBC))],
        out_specs=[pl.BlockSpec((tm, w), row) for w, _ in outs],
        out_shape=[jax.ShapeDtypeStruct((t, w), d) for w, d in outs],
        scratch_shapes=[pltpu.VMEM((tm + 16, SSD_XBC), F32)],
        compiler_params=_params(1),
        name="in_proj",
    )(x2, x2, x2, g, w_a, w_b, cos, sin_signed, conv_w, conv_b)


def _cumsum_rows(a):
    row = lax.broadcasted_iota(jnp.int32, a.shape, 0)
    v = a
    k = 1
    while k < a.shape[0]:
        v = v + jnp.where(row >= k, pltpu.roll(v, k, axis=0), 0.0)
        k *= 2
    return v


def _expand(v, e2_ref):
    hi = v.astype(BF16)
    lo = (v - hi.astype(F32)).astype(BF16)
    return _dot(jnp.concatenate([hi, lo], axis=1), e2_ref[...])


def _softplus(x):
    return jnp.maximum(x, 0.0) + jnp.log1p(jnp.exp(-jnp.abs(x)))


def _head_scalars(dt_raw, hp_ref):
    dtv = _softplus(dt_raw + hp_ref[0:1, :])
    a = dtv * (-LOG2_E * jnp.exp(hp_ref[1:2, :]))
    return dtv, a


def _chunk_scalars(dt_raw, hp_ref):
    dtv, a = _head_scalars(dt_raw, hp_ref)
    incl = _cumsum_rows(a)
    return dtv, incl, incl - a


def _ssd_bwd_kernel(act_ref, dt_ref, hp_ref, e2b_ref, hb_ref, state_ref):
    c = pl.program_id(1)

    @pl.when(c == 0)
    def _():
        state_ref[...] = jnp.zeros_like(state_ref)

    scales, decays = [], []
    for j in range(BWD_CHUNKS):
        dtv, incl, excl = _chunk_scalars(dt_ref[j * CHUNK:(j + 1) * CHUNK, :], hp_ref)
        scales.append(jnp.exp2(excl) * dtv)
        decays.append(jnp.exp2(incl[CHUNK - DECAY_ROWS:, :]))
    wide = _expand(jnp.concatenate(scales + decays, axis=0), e2b_ref)

    for j in reversed(range(BWD_CHUNKS)):
        rows = slice(j * CHUNK, (j + 1) * CHUNK)
        hb_ref[j] = state_ref[...].astype(BF16)
        xs = act_ref[rows, :SSD_INNER].astype(F32)
        bm = act_ref[rows, SSD_INNER:].astype(F32)
        xw = (xs * wide[rows]).astype(BF16)
        last = BWD_CHUNKS * CHUNK + (j + 1) * DECAY_ROWS - 1
        decay = wide[last:last + 1, :]
        for g in range(SSD_GROUPS):
            gs = slice(g * GROUP_W, (g + 1) * GROUP_W)
            bt = bm[:, g * SSD_STATE:(g + 1) * SSD_STATE].T.astype(BF16)
            state_ref[:, gs] = state_ref[:, gs] * decay[:, gs] + _dot(bt, xw[:, gs])


def _ssd_bwd(act, dt, hp, e2b, bsz, nc):
    width = SSD_INNER + SSD_BC
    steps = nc // BWD_CHUNKS
    block_of = lambda b, c: b * steps + (steps - 1 - c)
    return pl.pallas_call(
        _ssd_bwd_kernel,
        grid=(bsz, steps),
        in_specs=[pl.BlockSpec((BWD_CHUNKS * CHUNK, width), lambda b, c: (block_of(b, c), 0)),
                  pl.BlockSpec((BWD_CHUNKS * CHUNK, LANES), lambda b, c: (block_of(b, c), 0)),
                  _resident((8, LANES)),
                  _resident((2 * LANES, SSD_INNER))],
        out_specs=pl.BlockSpec((BWD_CHUNKS, SSD_STATE, SSD_INNER), lambda b, c: (block_of(b, c), 0, 0)),
        out_shape=jax.ShapeDtypeStruct((bsz * nc, SSD_STATE, SSD_INNER), BF16),
        scratch_shapes=[pltpu.VMEM((SSD_STATE, SSD_INNER), F32)],
        compiler_params=_params(2),
        name="ssd_bwd",
    )(act, dt, hp, e2b)


def _ssd_main_kernel(act_ref, dt_ref, hb_ref, hp_ref, dskip_ref, e2f_ref, sel_ref,
                     y_ref, state_ref, ybuf_ref):
    c = pl.program_id(1)

    @pl.when(c == 0)
    def _():
        state_ref[...] = jnp.zeros_like(state_ref)

    lane = lax.broadcasted_iota(jnp.int32, (CHUNK, LANES), 1)
    row = lax.broadcasted_iota(jnp.int32, (CHUNK, CHUNK), 0)
    col = lax.broadcasted_iota(jnp.int32, (CHUNK, CHUNK), 1)
    groups = range(SSD_GROUPS)
    to_fwd_slot = lambda v: pltpu.roll(v, LANES - SSD_HEADS, axis=1)
    prep, wide_in, col_in = [], [], []
    for j in range(MAIN_CHUNKS):
        rows = slice(j * CHUNK, (j + 1) * CHUNK)
        dtv, incl, excl = _chunk_scalars(dt_ref[rows, :], hp_ref)
        tot = incl[CHUNK - 1:CHUNK, :]
        cbs = [_dot_nt(act_ref[rows, SSD_INNER + SSD_BC + g * SSD_STATE:SSD_INNER + SSD_BC + (g + 1) * SSD_STATE],
                       act_ref[rows, SSD_INNER + g * SSD_STATE:SSD_INNER + (g + 1) * SSD_STATE]) for g in groups]
        lane_group = (lane % SSD_HEADS) // SSD_HPG
        cb_diag = jnp.zeros_like(dtv)
        for g in groups:
            dg = jnp.sum(jnp.where(row == col, cbs[g], 0.0), axis=1, keepdims=True)
            cb_diag = jnp.where(lane_group == g, dg, cb_diag)
        wide_in += [jnp.exp2(incl),
                    jnp.exp2(tot - incl) * dtv,
                    to_fwd_slot(jnp.exp2(tot - excl)),
                    to_fwd_slot(cb_diag * dtv),
                    jnp.exp2(incl[CHUNK - DECAY_ROWS:, :])]
        log_dt = jnp.log2(dtv)
        key_f = (incl - log_dt).T
        key_b = (excl + log_dt).T
        vmix = jnp.where(lane < SSD_HEADS, incl, excl)
        hi = vmix.astype(BF16).astype(F32)
        lo = vmix - hi
        for shift in [d * SSD_HEADS + g * SSD_HPG for d in range(2) for g in groups]:
            sh = (LANES - shift) % LANES
            col_in.append(jnp.concatenate([pltpu.roll(hi, sh, axis=1) if sh else hi,
                                           pltpu.roll(lo, sh, axis=1) if sh else lo], axis=1).astype(BF16))
        prep.append((cbs, key_f, key_b))
    wide = _expand(jnp.concatenate(wide_in, axis=0), e2f_ref)
    cols = _dot(jnp.concatenate(col_in, axis=0), sel_ref[...])

    for j in range(MAIN_CHUNKS):
        w0 = j * WIDE_ROWS
        c0 = j * 2 * SSD_GROUPS * CHUNK
        _ssd_main_chunk(j, prep[j], wide[w0:w0 + WIDE_ROWS], cols[c0:c0 + 2 * SSD_GROUPS * CHUNK],
                        act_ref, hb_ref, dskip_ref, y_ref, state_ref, ybuf_ref.at[j])


def _ssd_main_chunk(j, prep, wide, cols, act_ref, hb_ref, dskip_ref, y_ref, state_ref, ybuf_ref):
    rows = slice(j * CHUNK, (j + 1) * CHUNK)
    cbs, key_f, key_b = prep
    scale_f = wide[0:CHUNK]
    state_scale = wide[CHUNK:2 * CHUNK]
    scale_b = wide[2 * CHUNK:3 * CHUNK]
    self_b = wide[3 * CHUNK:4 * CHUNK]
    decay = wide[WIDE_ROWS - 1:WIDE_ROWS, :]
    xs16 = act_ref[rows, :SSD_INNER]
    xs = xs16.astype(F32)
    bm16 = act_ref[rows, SSD_INNER:SSD_INNER + SSD_BC]
    cm16 = act_ref[rows, SSD_INNER + SSD_BC:]

    row = lax.broadcasted_iota(jnp.int32, (CHUNK, CHUNK), 0)
    col = lax.broadcasted_iota(jnp.int32, (CHUNK, CHUNK), 1)
    lower = row >= col
    first_head = lax.broadcasted_iota(jnp.int32, (CHUNK, 2 * SSD_HEAD_DIM), 1) < SSD_HEAD_DIM

    groups = range(SSD_GROUPS)
    ns = [slice(g * SSD_STATE, (g + 1) * SSD_STATE) for g in groups]
    gs = [slice(g * GROUP_W, (g + 1) * GROUP_W) for g in groups]
    xw = (xs * state_scale).astype(BF16)
    st = [state_ref[:, gs[g]] for g in groups]
    off_f = [_dot(cm16[:, ns[g]], st[g].astype(BF16)) for g in groups]
    off_b = [_dot(cm16[:, ns[g]], hb_ref[j, :, gs[g]]) for g in groups]
    for g in groups:
        bt = bm16[:, ns[g]].astype(F32).T.astype(BF16)
        state_ref[:, gs[g]] = st[g] * decay[:, gs[g]] + _dot(bt, xw[:, gs[g]])

    for g in groups:
        col_f = cols[g * CHUNK:(g + 1) * CHUNK]
        col_b = cols[(SSD_GROUPS + g) * CHUNK:(SSD_GROUPS + g + 1) * CHUNK]
        for pair in range(SSD_HPG // 2):
            ms = []
            for r in (2 * pair, 2 * pair + 1):
                h = g * SSD_HPG + r
                hb_col = SSD_HEADS + h
                rs = slice(r * CHUNK, (r + 1) * CHUNK)
                seg_f = col_f[:, rs] - key_f[h:h + 1, :]
                seg_b = key_b[hb_col:hb_col + 1, :] - col_b[:, rs]
                ms.append((cbs[g] * jnp.exp2(jnp.where(lower, seg_f, seg_b))).astype(BF16))
            ps = slice((g * SSD_HPG + 2 * pair) * SSD_HEAD_DIM, (g * SSD_HPG + 2 * pair + 2) * SSD_HEAD_DIM)
            x2 = xs16[:, ps]
            zero = jnp.zeros_like(x2)
            rhs = jnp.concatenate([jnp.where(first_head, x2, zero), jnp.where(first_head, zero, x2)], axis=0)
            ybuf_ref[:, ps] = _dot(jnp.concatenate(ms, axis=1), rhs)

    for g in groups:
        y = (ybuf_ref[:, gs[g]] + scale_f[:, gs[g]] * off_f[g] + scale_b[:, gs[g]] * off_b[g]
             + (dskip_ref[:, gs[g]] + self_b[:, gs[g]]) * xs[:, gs[g]])
        y_ref[rows, gs[g]] = y.astype(BF16)


def _ssd_main(act, dt, hb, hp, dskip, e2f, sel, bsz, nc):
    steps = nc // MAIN_CHUNKS
    rows = MAIN_CHUNKS * CHUNK
    block_of = lambda b, c: b * steps + c
    return pl.pallas_call(
        _ssd_main_kernel,
        grid=(bsz, steps),
        in_specs=[pl.BlockSpec((rows, SSD_XBC), lambda b, c: (block_of(b, c), 0)),
                  pl.BlockSpec((rows, LANES), lambda b, c: (block_of(b, c), 0)),
                  pl.BlockSpec((MAIN_CHUNKS, SSD_STATE, SSD_INNER), lambda b, c: (block_of(b, c), 0, 0)),
                  _resident((8, LANES)),
                  _resident((1, SSD_INNER)),
                  _resident((2 * LANES, SSD_INNER)),
                  _resident((2 * LANES, SSD_HPG * CHUNK))],
        out_specs=pl.BlockSpec((rows, SSD_INNER), lambda b, c: (block_of(b, c), 0)),
        out_shape=jax.ShapeDtypeStruct((bsz * nc * CHUNK, SSD_INNER), BF16),
        scratch_shapes=[pltpu.VMEM((SSD_STATE, SSD_INNER), F32),
                        pltpu.VMEM((MAIN_CHUNKS, CHUNK, SSD_INNER), F32)],
        compiler_params=_params(2),
        name="ssd_main",
    )(act, dt, hb, hp, dskip, e2f, sel)


def _swa_kernel(q_ref, kvp_ref, kvc_ref, kvn_ref, sink_ref, o_ref):
    n = pl.program_id(1)
    last = pl.num_programs(1) - 1
    kj = lax.broadcasted_iota(jnp.int32, (3 * ATTN_BLOCK, ATTN_BLOCK), 0)
    qi = lax.broadcasted_iota(jnp.int32, (3 * ATTN_BLOCK, ATTN_BLOCK), 1)
    rel = kj - qi
    band = (rel >= 0) & (rel <= 2 * ATTN_BLOCK)
    kvcat = jnp.concatenate([kvp_ref[...], kvc_ref[...], kvn_ref[...]], axis=0)
    kcat = kvcat[:, :ATTN_KV_WIDTH]
    vcat_t = kvcat[:, ATTN_KV_WIDTH:].astype(F32).T.astype(BF16)
    scores = []
    for j in range(SWA_BLOCKS):
        valid = band
        if j == 0:
            valid = valid & ((kj >= ATTN_BLOCK) | (n > 0))
        if j == SWA_BLOCKS - 1:
            valid = valid & ((kj < 2 * ATTN_BLOCK) | (n < last))
        bias = jnp.concatenate([jnp.where(valid, 0.0, NEG_BIG)] * ATTN_GQA, axis=1)
        edge_bias = (bias[:ATTN_BLOCK], bias[2 * ATTN_BLOCK:])
        keys = slice(j * ATTN_BLOCK, (j + 3) * ATTN_BLOCK)
        qrows = slice(j * ATTN_BLOCK, (j + 1) * ATTN_BLOCK)
        for kv in range(ATTN_KV_HEADS):
            ks = slice(kv * ATTN_HEAD_DIM, (kv + 1) * ATTN_HEAD_DIM)
            q_stack = jnp.concatenate(
                [q_ref[qrows, (kv * ATTN_GQA + r) * ATTN_HEAD_DIM:(kv * ATTN_GQA + r + 1) * ATTN_HEAD_DIM]
                 for r in range(ATTN_GQA)], axis=0)
            s = _dot_nt(kcat[keys, ks], q_stack)
            scores.append(jnp.concatenate([s[:ATTN_BLOCK] + edge_bias[0], s[ATTN_BLOCK:2 * ATTN_BLOCK],
                                           s[2 * ATTN_BLOCK:] + edge_bias[1]], axis=0))
    probs = []
    for idx, s in enumerate(scores):
        kv = idx % ATTN_KV_HEADS
        sink = sink_ref[kv:kv + 1, :]
        m = jnp.maximum(jnp.max(s, axis=0, keepdims=True), sink)
        p = jnp.exp2(s - m)
        denom = jnp.sum(p, axis=0, keepdims=True) + jnp.exp2(sink - m)
        probs.append((p.astype(BF16), 1.0 / denom))
    for j in range(SWA_BLOCKS):
        keys = slice(j * ATTN_BLOCK, (j + 3) * ATTN_BLOCK)
        outs = []
        for kv in range(ATTN_KV_HEADS):
            ks = slice(kv * ATTN_HEAD_DIM, (kv + 1) * ATTN_HEAD_DIM)
            p16, inv = probs[j * ATTN_KV_HEADS + kv]
            o_t = _dot(vcat_t[ks, keys], p16) * inv
            outs.extend(o_t[:, r * ATTN_BLOCK:(r + 1) * ATTN_BLOCK] for r in range(ATTN_GQA))
        o_ref[j * ATTN_BLOCK:(j + 1) * ATTN_BLOCK, :] = jnp.concatenate(outs, axis=0).T.astype(BF16)


def _swa(q, kv, sink_rows, bsz, nb):
    steps = nb // SWA_BLOCKS
    rows = SWA_BLOCKS * ATTN_BLOCK
    cur = lambda b, n: (b * steps + n, 0)
    prev = lambda b, n: (b * nb + jnp.maximum(SWA_BLOCKS * n - 1, 0), 0)
    nxt = lambda b, n: (b * nb + jnp.minimum(SWA_BLOCKS * (n + 1), nb - 1), 0)
    return pl.pallas_call(
        _swa_kernel,
        grid=(bsz, steps),
        in_specs=[pl.BlockSpec((rows, ATTN_WIDTH), cur),
                  pl.BlockSpec((ATTN_BLOCK, 2 * ATTN_KV_WIDTH), prev),
                  pl.BlockSpec((rows, 2 * ATTN_KV_WIDTH), cur),
                  pl.BlockSpec((ATTN_BLOCK, 2 * ATTN_KV_WIDTH), nxt),
                  _resident((8, ATTN_GQA * ATTN_BLOCK))],
        out_specs=pl.BlockSpec((rows, ATTN_WIDTH), cur),
        out_shape=jax.ShapeDtypeStruct(q.shape, BF16),
        compiler_params=_params(2),
        name="swa",
    )(q, kv, kv, kv, sink_rows)


def _mem_kv_kernel(m_ref, g_ref, w_ref, o_ref):
    h = _rms(m_ref[...], g_ref[...]).astype(BF16)
    o_ref[...] = _dot(h, w_ref[...]).astype(BF16)


def _mem_kv(mem2, g, w_kv):
    t = mem2.shape[0]
    tm = 256
    row = lambda i: (i, 0)
    return pl.pallas_call(
        _mem_kv_kernel,
        grid=(t // tm,),
        in_specs=[pl.BlockSpec((tm, D_MODEL), row),
                  _resident((1, D_MODEL)),
                  _resident((D_MODEL, 2 * D_MODEL))],
        out_specs=pl.BlockSpec((tm, 2 * D_MODEL), row),
        out_shape=jax.ShapeDtypeStruct((t, 2 * D_MODEL), BF16),
        compiler_params=_params(1),
        name="mem_kv",
    )(mem2, g, w_kv)


def _merge_xattn_kernel(x_ref, y_ref, z_ref, a_ref, gate_ref, kv_ref, ng_ref, ws_ref, wa_ref, wo_ref, g_ref, wq_ref,
                        wxo_ref, o_ref, ctx_ref, yn_ref):
    for g in range(SSD_GROUPS):
        gs = slice(g * GROUP_W, (g + 1) * GROUP_W)
        v = y_ref[:, gs].astype(F32) * _silu(z_ref[:, gs].astype(F32))
        v = v * lax.rsqrt(jnp.mean(v * v, axis=-1, keepdims=True) + EPS)
        yn_ref[:, gs] = (v * ng_ref[:, gs]).astype(BF16)
    bs = _dot(yn_ref[...], ws_ref[...])
    ba = _dot(a_ref[...], wa_ref[...])
    gate = jax.nn.sigmoid(gate_ref[...].astype(F32))
    mix = (gate[:, :D_MODEL] * bs + gate[:, D_MODEL:] * ba).astype(BF16)
    x = x_ref[...] + _dot(mix, wo_ref[...])

    h = _rms(x, g_ref[...]).astype(BF16)
    q = (_dot(h, wq_ref[...]) * (XATTN_HEAD_DIM ** -0.5 * LOG2_E)).astype(BF16)
    heads = [slice(hd * XATTN_HEAD_DIM, (hd + 1) * XATTN_HEAD_DIM) for hd in range(XATTN_HEADS)]
    scores = [_dot_nt(q[:, ks], kv_ref[:, ks]) for ks in heads]
    probs = []
    for s in scores:
        p = jnp.exp2(s - jnp.max(s, axis=-1, keepdims=True))
        probs.append((p.astype(BF16), 1.0 / jnp.sum(p, axis=-1, keepdims=True)))
    for ks, (p16, inv) in zip(heads, probs):
        vs = slice(D_MODEL + ks.start, D_MODEL + ks.stop)
        ctx_ref[:, ks] = (_dot(p16, kv_ref[:, vs]) * inv).astype(BF16)
    o_ref[...] = x + _dot(ctx_ref[...], wxo_ref[...])


def _merge_xattn(x2, y, z, attn, gates, kv, ng, ws, wa, wo, g, wq, wxo, seq, mem_len):
    t = x2.shape[0]
    tm = 512
    per_seq = seq // tm
    row = lambda i: (i, 0)
    return pl.pallas_call(
        _merge_xattn_kernel,
        grid=(t // tm,),
        in_specs=[pl.BlockSpec((tm, D_MODEL), row),
                  pl.BlockSpec((tm, SSD_INNER), row),
                  pl.BlockSpec((tm, SSD_INNER), row),
                  pl.BlockSpec((tm, ATTN_WIDTH), row),
                  pl.BlockSpec((tm, 2 * D_MODEL), row),
                  pl.BlockSpec((mem_len, 2 * D_MODEL), lambda i: (i // per_seq, 0)),
                  _resident((1, SSD_INNER)),
                  _resident((SSD_INNER, D_MODEL)),
                  _resident((ATTN_WIDTH, D_MODEL)),
                  _resident((D_MODEL, D_MODEL)),
                  _resident((1, D_MODEL)),
                  _resident((D_MODEL, D_MODEL)),
                  _resident((D_MODEL, D_MODEL))],
        out_specs=pl.BlockSpec((tm, D_MODEL), row),
        out_shape=jax.ShapeDtypeStruct((t, D_MODEL), F32),
        scratch_shapes=[pltpu.VMEM((tm, D_MODEL), BF16), pltpu.VMEM((tm, SSD_INNER), BF16)],
        compiler_params=_params(1),
        name="merge_xattn",
    )(x2, y, z, attn, gates, kv, ng, ws, wa, wo, g, wq, wxo)


MXU_TILE = 256
FFN_SPLIT = (FFN_HIDDEN // MXU_TILE + 1) // 2 * MXU_TILE


def _ffn_kernel(x_ref, g_ref, wi_ref, wd_ref, gf_ref, o_ref):
    x = x_ref[...]
    h = _rms(x, g_ref[...]).astype(BF16)
    acc = x
    for lo, hi in ((0, FFN_SPLIT), (FFN_SPLIT, FFN_HIDDEN)):
        gate = _dot(h, wi_ref[:, lo:hi])
        up = _dot(h, wi_ref[:, FFN_HIDDEN + lo:FFN_HIDDEN + hi])
        act = (_silu(gate) * up).astype(BF16)
        acc = acc + _dot(act, wd_ref[lo:hi, :])
    o_ref[...] = _rms(acc, gf_ref[...])


def _ffn(x2, g, wi, wd, gf):
    t = x2.shape[0]
    tm = 512
    row = lambda i: (i, 0)
    return pl.pallas_call(
        _ffn_kernel,
        grid=(t // tm,),
        in_specs=[pl.BlockSpec((tm, D_MODEL), row),
                  _resident((1, D_MODEL)),
                  _resident((D_MODEL, 2 * FFN_HIDDEN)),
                  _resident((FFN_HIDDEN, D_MODEL)),
                  _resident((1, D_MODEL))],
        out_specs=pl.BlockSpec((tm, D_MODEL), row),
        out_shape=jax.ShapeDtypeStruct((t, D_MODEL), F32),
        compiler_params=_params(1),
        name="ffn",
    )(x2, g, wi, wd, gf)


def _pad_rows(a, rows):
    return jnp.pad(a, ((0, rows - a.shape[0]), (0, 0)))


def _pad_cols(a, cols):
    return jnp.pad(a, ((0, 0), (0, cols - a.shape[1])))


def _column_select_matrix():
    j = np.arange(LANES)[:, None]
    blk = np.arange(SSD_HPG * CHUNK)[None, :] // CHUNK
    e = (j == blk).astype(np.float32)
    return jnp.asarray(np.concatenate([e, e], axis=0), dtype=BF16)


def _select_matrix(slot):
    j = np.arange(LANES)[:, None]
    ch = np.arange(SSD_INNER)[None, :]
    e = (j == slot * SSD_HEADS + ch // SSD_HEAD_DIM).astype(np.float32)
    return jnp.asarray(np.concatenate([e, e], axis=0), dtype=BF16)


def _layer(x2, mem2, bsz, seq, mem_len, norm_mix_g, w_in, conv_w, conv_b, dt_bias_fwd, dt_bias_bwd,
           a_log_fwd, a_log_bwd, d_skip, ssd_norm_g, attn_sink, w_branch_ssd, w_branch_attn, w_mix_out,
           norm_xattn_g, norm_mem_g, w_xattn_q, w_xattn_kv, w_xattn_out, norm_ffn_g, w_ffn_in, w_ffn_out,
           final_g):
    nc = seq // CHUNK
    w_a = w_in[:, :WIDTH_A].astype(BF16)
    w_b = w_in[:, START_B:].astype(BF16)

    half = ATTN_HEAD_DIM // 2
    inv_freq = ROPE_THETA ** (-jnp.arange(half, dtype=F32) / half)
    ang = jnp.arange(seq, dtype=F32)[:, None] * inv_freq[None]
    cos = jnp.tile(jnp.cos(ang), (1, LANES // half))
    sin_signed = jnp.tile(jnp.concatenate([-jnp.sin(ang), jnp.sin(ang)], axis=1), (1, LANES // ATTN_HEAD_DIM))

    cw = _pad_rows(conv_w, 8)
    z, act, q, kv_attn, gates, dt = _in_proj(x2, norm_mix_g[None], w_a, w_b, cos, sin_signed, cw, conv_b[None], seq)

    hp = _pad_rows(jnp.stack([_pad_cols(jnp.concatenate([dt_bias_fwd, dt_bias_bwd])[None], LANES)[0],
                              _pad_cols(jnp.concatenate([a_log_fwd, a_log_bwd])[None], LANES)[0]]), 8)
    e2f = _select_matrix(0)
    e2b = _select_matrix(1)
    hb = _ssd_bwd(act, dt, hp, e2b, bsz, nc)
    dskip = jnp.repeat(d_skip, SSD_HEAD_DIM)[None]
    y = _ssd_main(act, dt, hb, hp, dskip, e2f, _column_select_matrix(), bsz, nc)

    sink_rows = _pad_rows(jnp.repeat(attn_sink * LOG2_E, ATTN_BLOCK).reshape(ATTN_KV_HEADS, ATTN_GQA * ATTN_BLOCK), 8)
    attn = _swa(q, kv_attn, sink_rows, bsz, seq // ATTN_BLOCK)

    kv = _mem_kv(mem2, norm_mem_g[None], w_xattn_kv.astype(BF16))
    x2 = _merge_xattn(x2, y, z, attn, gates, kv, ssd_norm_g[None], w_branch_ssd.astype(BF16), w_branch_attn.astype(BF16),
                      w_mix_out.astype(BF16), norm_xattn_g[None], w_xattn_q.astype(BF16),
                      w_xattn_out.astype(BF16), seq, mem_len)

    return _ffn(x2, norm_ffn_g[None], w_ffn_in.astype(BF16), w_ffn_out.astype(BF16), final_g)


def kernel(x, mem, norm_mix_g, w_in, conv_w, conv_b, dt_bias_fwd, dt_bias_bwd, a_log_fwd, a_log_bwd, d_skip,
           ssd_norm_g, attn_sink, w_branch_ssd, w_branch_attn, w_mix_out, norm_xattn_g, norm_mem_g, w_xattn_q,
           w_xattn_kv, w_xattn_out, norm_ffn_g, w_ffn_in, w_ffn_out, norm_final_g):
    bsz, seq, _ = x.shape
    mem_len = mem.shape[1]
    assert w_in.shape[0] == 1, "single-layer stack expected"
    layer = 0
    x2 = x.reshape(bsz * seq, D_MODEL)
    mem2 = mem.reshape(bsz * mem_len, D_MODEL)
    out = _layer(x2, mem2, bsz, seq, mem_len, norm_mix_g[layer], w_in[layer], conv_w[layer], conv_b[layer],
                 dt_bias_fwd[layer], dt_bias_bwd[layer], a_log_fwd[layer], a_log_bwd[layer], d_skip[layer],
                 ssd_norm_g[layer], attn_sink[layer], w_branch_ssd[layer], w_branch_attn[layer],
                 w_mix_out[layer], norm_xattn_g[layer], norm_mem_g[layer], w_xattn_q[layer],
                 w_xattn_kv[layer], w_xattn_out[layer], norm_ffn_g[layer], w_ffn_in[layer],
                 w_ffn_out[layer], norm_final_g[None])
    return out.reshape(bsz, seq, D_MODEL)
```
